```python
import math
import jax
import jax.numpy as jnp
from jax import lax
import numpy as np

D_MODEL = 1024
BATCH = 8
SEQ = 4096
DEPTH = 4

N_MIXERS = 4
NORM_EPS = 1e-6
NEG_INF = -1e30
F32 = jnp.float32

MOBA_HEADS = 16
MOBA_HEAD_DIM = D_MODEL // MOBA_HEADS
MOBA_BLOCK = 256
MOBA_TOPK = 3
MOBA_Q_CHUNK = 16

RWKV_HEAD_DIM = 64
RWKV_HEADS = D_MODEL // RWKV_HEAD_DIM
RWKV_DECAY_LORA = 64
RWKV_AAA_LORA = 64
RWKV_GATE_LORA = 160
RWKV_GN_EPS = 64e-5
RWKV_N_MIX = 6

SSD_D_INNER = 2 * D_MODEL
SSD_HEAD_DIM = 64
SSD_HEADS = SSD_D_INNER // SSD_HEAD_DIM
SSD_GROUPS = 4
SSD_STATE = 128
SSD_CONV = 4
SSD_CHUNK = 128
SSD_CONV_DIM = SSD_D_INNER + 2 * SSD_GROUPS * SSD_STATE
SSD_IN_DIM = SSD_D_INNER + SSD_CONV_DIM + SSD_HEADS
SSD_NORM_EPS = 1e-5

CONF_KERNEL = 31
CONF_LN_EPS = 1e-5

D_FF = 2816
FFN_CONV = 3

kernel_name = 'hybrid_moba_rwkv7_ssd_conformer'


def _rmsnorm(x, g):
    x32 = x.astype(F32)
    y = x32 * lax.rsqrt(jnp.mean(x32 * x32, axis=-1, keepdims=True) + NORM_EPS)
    return (y * g.astype(F32)).astype(x.dtype)


def _layernorm(x, w, b, eps):
    x32 = x.astype(F32)
    mu = jnp.mean(x32, axis=-1, keepdims=True)
    var = jnp.mean(jnp.square(x32 - mu), axis=-1, keepdims=True)
    return ((x32 - mu) * lax.rsqrt(var + eps) * w.astype(F32) + b.astype(F32)).astype(x.dtype)


def _causal_dwconv(x, w, b):
    k, c = w.shape
    y = lax.conv_general_dilated(x, w[:, None, :], window_strides=(1,), padding=[(k - 1, 0)],
                                 dimension_numbers=('NWC', 'WIO', 'NWC'), feature_group_count=c)
    return y + b


def _moba(x, w_qkv, w_o):
    b, s, d = x.shape
    h, dh, bs, qc = MOBA_HEADS, MOBA_HEAD_DIM, MOBA_BLOCK, MOBA_Q_CHUNK
    qkv = jnp.einsum('bsd,de->bse', x, w_qkv).reshape(b, s, 3, h, dh)
    q = jnp.transpose(qkv[:, :, 0], (0, 2, 1, 3)) * dh ** -0.5
    k = jnp.transpose(qkv[:, :, 1], (0, 2, 1, 3))
    v = jnp.transpose(qkv[:, :, 2], (0, 2, 1, 3))
    n_blk = -(-s // bs)
    pad = n_blk * bs - s
    kb = jnp.pad(k, ((0, 0), (0, 0), (0, pad), (0, 0))).reshape(b, h, n_blk, bs, dh)
    vb = jnp.pad(v, ((0, 0), (0, 0), (0, pad), (0, 0))).reshape(b, h, n_blk, bs, dh)
    k_mean = jnp.mean(kb, axis=3)
    q_blk = jnp.arange(s) // bs
    gate = jnp.einsum('bhsd,bhnd->bhsn', q, k_mean).astype(F32)
    fully_past = jnp.arange(n_blk)[None, :] < q_blk[:, None]
    gate = jnp.where(fully_past, gate, NEG_INF)
    n_sel = min(MOBA_TOPK, n_blk)
    _, sel = lax.top_k(gate, n_sel)
    sel_valid = sel < q_blk[:, None]
    bi = jnp.arange(b)[:, None, None, None]
    hi = jnp.arange(h)[None, :, None, None]

    def chunk(c):
        start = c * qc
        q_c = lax.dynamic_slice_in_dim(q, start, qc, axis=2)
        sel_c = lax.dynamic_slice_in_dim(sel, start, qc, axis=2)
        val_c = lax.dynamic_slice_in_dim(sel_valid, start, qc, axis=2)
        k_sel = kb[bi, hi, sel_c]
        v_sel = vb[bi, hi, sel_c]
        own = start // bs
        k_own = lax.dynamic_index_in_dim(kb, own, axis=2, keepdims=False)
        v_own = lax.dynamic_index_in_dim(vb, own, axis=2, keepdims=False)
        s_sel = jnp.einsum('bhqd,bhqnkd->bhqnk', q_c, k_sel).astype(F32)
        s_sel = jnp.where(val_c[..., None], s_sel, NEG_INF).reshape(b, h, qc, n_sel * bs)
        s_own = jnp.einsum('bhqd,bhkd->bhqk', q_c, k_own).astype(F32)
        q_pos = start + jnp.arange(qc)
        k_pos = own * bs + jnp.arange(bs)
        s_own = jnp.where(k_pos[None, :] <= q_pos[:, None], s_own, NEG_INF)
        p = jax.nn.softmax(jnp.concatenate([s_sel, s_own], axis=-1), axis=-1).astype(v.dtype)
        p_sel = p[..., :n_sel * bs].reshape(b, h, qc, n_sel, bs)
        p_own = p[..., n_sel * bs:]
        return (jnp.einsum('bhqnk,bhqnkd->bhqd', p_sel, v_sel)
                + jnp.einsum('bhqk,bhkd->bhqd', p_own, v_own))

    o = lax.map(chunk, jnp.arange(s // qc))
    o = jnp.transpose(o, (1, 0, 3, 2, 4)).reshape(b, s, d)
    return o @ w_o


def _rwkv7(x, mu, w_rkv, w0, w1, w2, a0, a1, a2, g1, g2, k_k, k_a, r_k, gn_w, gn_b, w_o):
    b, s, d = x.shape
    h, n = RWKV_HEADS, RWKV_HEAD_DIM
    xx = jnp.pad(x, ((0, 0), (1, 0), (0, 0)))[:, :-1] - x
    xm = x[:, :, None, :] + xx[:, :, None, :] * mu
    rkv = jnp.einsum('bsid,ide->bsie', xm[:, :, :3], w_rkv)
    r, k, v = rkv[:, :, 0], rkv[:, :, 1], rkv[:, :, 2]
    xw, xa, xg = xm[:, :, 3], xm[:, :, 4], xm[:, :, 5]
    w = -jax.nn.softplus(-(w0 + jnp.tanh(xw @ w1) @ w2).astype(F32)) - 0.5
    decay = jnp.exp(-jnp.exp(w))
    a = jax.nn.sigmoid(a0 + (xa @ a1) @ a2)
    g = jax.nn.sigmoid(xg @ g1) @ g2
    kk = (k * k_k).reshape(b, s, h, n).astype(F32)
    kk = kk / jnp.maximum(jnp.sqrt(jnp.sum(kk * kk, axis=-1, keepdims=True)), 1e-12)
    k = k * (1.0 + (a - 1.0) * k_a)
    r, k, v, a = (t.reshape(b, s, h, n) for t in (r, k, v, a))

    def to_time(t):
        return jnp.swapaxes(t.astype(F32), 0, 1)

    xs = (to_time(r), to_time(decay.reshape(b, s, h, n)), to_time(k), to_time(v),
          to_time(-kk), to_time(kk * a))

    def step(state, inp):
        r_t, w_t, k_t, v_t, a_t, b_t = inp
        sa = jnp.einsum('bhij,bhj->bhi', state, a_t)
        state = (state * w_t[:, :, None, :] + sa[..., None] * b_t[:, :, None, :]
                 + v_t[..., None] * k_t[:, :, None, :])
        return state, jnp.einsum('bhij,bhj->bhi', state, r_t)

    _, y = lax.scan(step, jnp.zeros((b, h, n, n), F32), xs)
    y = jnp.swapaxes(y, 0, 1)
    mu_y = jnp.mean(y, axis=-1, keepdims=True)
    var_y = jnp.mean(jnp.square(y - mu_y), axis=-1, keepdims=True)
    yn = ((y - mu_y) * lax.rsqrt(var_y + RWKV_GN_EPS)).reshape(b, s, d)
    yn = yn * gn_w.astype(F32) + gn_b.astype(F32)
    bonus = (jnp.sum(r * k * r_k, axis=-1, keepdims=True) * v).reshape(b, s, d).astype(F32)
    return ((yn + bonus).astype(x.dtype) * g) @ w_o


def _segsum(a):
    t = a.shape[-1]
    rep = jnp.broadcast_to(a[..., :, None], a.shape + (t,))
    rep = jnp.where(jnp.tril(jnp.ones((t, t), bool), -1), rep, 0.0)
    cs = jnp.cumsum(rep, axis=-2)
    return jnp.where(jnp.tril(jnp.ones((t, t), bool)), cs, -jnp.inf)


def _ssd_chunked(xd, adt, bm, cm):
    b, s, h, p = xd.shape
    g, n = bm.shape[2], bm.shape[3]
    e = h // g
    lc = SSD_CHUNK
    c = s // lc
    xc = xd.reshape(b, c, lc, g, e, p)
    bc = bm.reshape(b, c, lc, g, n)
    cc = cm.reshape(b, c, lc, g, n)
    ac = jnp.transpose(adt.reshape(b, c, lc, g, e), (0, 3, 4, 1, 2))
    a_cs = jnp.cumsum(ac, axis=-1)
    lmat = jnp.exp(_segsum(ac))
    cb = jnp.einsum('bclgn,bcsgn->bgcls', cc, bc)
    y_diag = jnp.einsum('bgecls,bcsgep->bclgep', cb[:, :, None] * lmat, xc)
    decay_states = jnp.exp(a_cs[..., -1:] - a_cs)
    states = jnp.einsum('bclgn,bgecl,bclgep->bcgepn', bc, decay_states, xc)
    states = jnp.concatenate([jnp.zeros_like(states[:, :1]), states], axis=1)
    chunk_a = jnp.pad(a_cs[..., -1], ((0, 0), (0, 0), (0, 0), (1, 0)))
    decay_chunk = jnp.exp(_segsum(chunk_a))
    states = jnp.einsum('bgezc,bcgepn->bzgepn', decay_chunk, states)[:, :-1]
    y_off = jnp.einsum('bclgn,bcgepn,bgecl->bclgep', cc, states, jnp.exp(a_cs))
    return (y_diag + y_off).reshape(b, s, h, p)


def _mamba2(x, w_in, conv_w, conv_b, dt_bias, a_log, d_skip, norm_w, w_out):
    b, s, _ = x.shape
    di, h, p, g, n = SSD_D_INNER, SSD_HEADS, SSD_HEAD_DIM, SSD_GROUPS, SSD_STATE
    zxbcdt = x @ w_in
    z, xbc, dt = jnp.split(zxbcdt, [di, di + SSD_CONV_DIM], axis=-1)
    xbc = jax.nn.silu(_causal_dwconv(xbc, conv_w, conv_b))
    xs, bm, cm = jnp.split(xbc, [di, di + g * n], axis=-1)
    dt = jax.nn.softplus((dt + dt_bias).astype(F32))
    a = -jnp.exp(a_log.astype(F32))
    xh = xs.reshape(b, s, h, p).astype(F32)
    y = _ssd_chunked(xh * dt[..., None], dt * a,
                     bm.reshape(b, s, g, n).astype(F32), cm.reshape(b, s, g, n).astype(F32))
    y = (y + xh * d_skip.astype(F32)[:, None]).reshape(b, s, di)
    yz = (y * jax.nn.silu(z.astype(F32))).reshape(b, s, g, di // g)
    yz = yz * lax.rsqrt(jnp.mean(yz * yz, axis=-1, keepdims=True) + SSD_NORM_EPS)
    yz = yz.reshape(b, s, di) * norm_w.astype(F32)
    return yz.astype(x.dtype) @ w_out


def _conformer_conv(x, w_pw1, b_pw1, dw_w, dw_b, ln_w, ln_b, w_pw2, b_pw2):
    u = jax.nn.glu(x @ w_pw1 + b_pw1, axis=-1)
    u = _causal_dwconv(u, dw_w, dw_b)
    u = jax.nn.silu(_layernorm(u, ln_w, ln_b, CONF_LN_EPS))
    return u @ w_pw2 + b_pw2


def _conv_ffn(x, w_up, conv_w, conv_b, w_down):
    u = _causal_dwconv(x @ w_up, conv_w, conv_b)
    gate, up = jnp.split(u, 2, axis=-1)
    return (jax.nn.silu(gate) * up) @ w_down


def _count(kind):
    return len(range(kind, DEPTH, N_MIXERS))


def setup_inputs(seed: int = 0) -> dict:
    key = jax.random.key(seed)
    ks = iter(jax.random.split(key, 64))

    def nrm(shape, scale):
        return jax.random.normal(next(ks), shape, F32) * scale

    def uni(shape, lo, hi):
        return jax.random.uniform(next(ks), shape, F32, lo, hi)

    d = D_MODEL
    n_a, n_b, n_c, n_d = (_count(m) for m in range(N_MIXERS))
    dt0 = jnp.exp(uni((n_c, SSD_HEADS), math.log(1e-3), math.log(1e-1)))
    return {
        'x': nrm((BATCH, SEQ, d), 1.0),
        'norm_mix': 1.0 + nrm((DEPTH, d), 0.02),
        'norm_ffn': 1.0 + nrm((DEPTH, d), 0.02),
        'norm_final': 1.0 + nrm((d,), 0.02),
        'ffn_w_up': nrm((DEPTH, d, 2 * D_FF), d ** -0.5),
        'ffn_conv_w': nrm((DEPTH, FFN_CONV, 2 * D_FF), FFN_CONV ** -0.5),
        'ffn_conv_b': nrm((DEPTH, 2 * D_FF), 0.01),
        'ffn_w_down': nrm((DEPTH, D_FF, d), D_FF ** -0.5),
        'moba_w_qkv': nrm((n_a, d, 3 * d), d ** -0.5),
        'moba_w_o': nrm((n_a, d, d), d ** -0.5),
        'rwkv_mu': uni((n_b, RWKV_N_MIX, d), 0.0, 1.0),
        'rwkv_w_rkv': nrm((n_b, 3, d, d), d ** -0.5),
        'rwkv_w0': uni((n_b, d), -6.0, 1.0),
        'rwkv_w1': nrm((n_b, d, RWKV_DECAY_LORA), d ** -0.5),
        'rwkv_w2': nrm((n_b, RWKV_DECAY_LORA, d), 0.1 * RWKV_DECAY_LORA ** -0.5),
        'rwkv_a0': nrm((n_b, d), 0.1),
        'rwkv_a1': nrm((n_b, d, RWKV_AAA_LORA), d ** -0.5),
        'rwkv_a2': nrm((n_b, RWKV_AAA_LORA, d), 0.1 * RWKV_AAA_LORA ** -0.5),
        'rwkv_g1': nrm((n_b, d, RWKV_GATE_LORA), d ** -0.5),
        'rwkv_g2': nrm((n_b, RWKV_GATE_LORA, d), RWKV_GATE_LORA ** -0.5),
        'rwkv_k_k': 0.85 + nrm((n_b, d), 0.02),
        'rwkv_k_a': 1.0 + nrm((n_b, d), 0.02),
        'rwkv_r_k': nrm((n_b, RWKV_HEADS, RWKV_HEAD_DIM), 0.1),
        'rwkv_gn_w': 1.0 + nrm((n_b, d), 0.02),
        'rwkv_gn_b': nrm((n_b, d), 0.01),
        'rwkv_w_o': nrm((n_b, d, d), d ** -0.5),
        'ssd_w_in': nrm((n_c, d, SSD_IN_DIM), d ** -0.5),
        'ssd_conv_w': nrm((n_c, SSD_CONV, SSD_CONV_DIM), SSD_CONV ** -0.5),
        'ssd_conv_b': nrm((n_c, SSD_CONV_DIM), 0.01),
        'ssd_dt_bias': dt0 + jnp.log(-jnp.expm1(-dt0)),
        'ssd_a_log': jnp.log(uni((n_c, SSD_HEADS), 1.0, 16.0)),
        'ssd_d': 1.0 + nrm((n_c, SSD_HEADS), 0.1),
        'ssd_norm_w': 1.0 + nrm((n_c, SSD_D_INNER), 0.02),
        'ssd_w_out': nrm((n_c, SSD_D_INNER, d), SSD_D_INNER ** -0.5),
        'conf_w_pw1': nrm((n_d, d, 2 * d), d ** -0.5),
        'conf_b_pw1': nrm((n_d, 2 * d), 0.01),
        'conf_dw_w': nrm((n_d, CONF_KERNEL, d), CONF_KERNEL ** -0.5),
        'conf_dw_b': nrm((n_d, d), 0.01),
        'conf_ln_w': 1.0 + nrm((n_d, d), 0.02),
        'conf_ln_b': nrm((n_d, d), 0.01),
        'conf_w_pw2': nrm((n_d, d, d), d ** -0.5),
        'conf_b_pw2': nrm((n_d, d), 0.01),
    }


def reference(x, norm_mix, norm_ffn, norm_final, ffn_w_up, ffn_conv_w, ffn_conv_b, ffn_w_down,
              moba_w_qkv, moba_w_o,
              rwkv_mu, rwkv_w_rkv, rwkv_w0, rwkv_w1, rwkv_w2, rwkv_a0, rwkv_a1, rwkv_a2,
              rwkv_g1, rwkv_g2, rwkv_k_k, rwkv_k_a, rwkv_r_k, rwkv_gn_w, rwkv_gn_b, rwkv_w_o,
              ssd_w_in, ssd_conv_w, ssd_conv_b, ssd_dt_bias, ssd_a_log, ssd_d, ssd_norm_w, ssd_w_out,
              conf_w_pw1, conf_b_pw1, conf_dw_w, conf_dw_b, conf_ln_w, conf_ln_b, conf_w_pw2, conf_b_pw2):
    for i in range(DEPTH):
        kind, j = i % N_MIXERS, i // N_MIXERS
        hn = _rmsnorm(x, norm_mix[i])
        if kind == 0:
            mix = _moba(hn, moba_w_qkv[j], moba_w_o[j])
        elif kind == 1:
            mix = _rwkv7(hn, rwkv_mu[j], rwkv_w_rkv[j], rwkv_w0[j], rwkv_w1[j], rwkv_w2[j],
                         rwkv_a0[j], rwkv_a1[j], rwkv_a2[j], rwkv_g1[j], rwkv_g2[j],
                         rwkv_k_k[j], rwkv_k_a[j], rwkv_r_k[j], rwkv_gn_w[j], rwkv_gn_b[j], rwkv_w_o[j])
        elif kind == 2:
            mix = _mamba2(hn, ssd_w_in[j], ssd_conv_w[j], ssd_conv_b[j], ssd_dt_bias[j],
                          ssd_a_log[j], ssd_d[j], ssd_norm_w[j], ssd_w_out[j])
        else:
            mix = _conformer_conv(hn, conf_w_pw1[j], conf_b_pw1[j], conf_dw_w[j], conf_dw_b[j],
                                  conf_ln_w[j], conf_ln_b[j], conf_w_pw2[j], conf_b_pw2[j])
        x = x + mix
        x = x + _conv_ffn(_rmsnorm(x, norm_ffn[i]), ffn_w_up[i], ffn_conv_w[i], ffn_conv_b[i], ffn_w_down[i])
    return _rmsnorm(x, norm_final)
```

```python
import functools

import jax
import jax.numpy as jnp
from jax import lax
from jax.experimental import pallas as pl
from jax.experimental.pallas import tpu as pltpu

F32 = jnp.float32
BF16 = jnp.bfloat16

D_MODEL = 1024
NORM_EPS = 1e-6
NEG_INF = -1e30

MOBA_HEADS = 16
MOBA_HEAD_DIM = 64
MOBA_BLOCK = 256
MOBA_TOPK = 3

RWKV_HEADS = 16
RWKV_HEAD_DIM = 64
RWKV_GN_EPS = 64e-5
RWKV_CHUNK = 64

SSD_D_INNER = 2048
SSD_HEAD_DIM = 64
SSD_HEADS = 32
SSD_GROUPS = 4
SSD_STATE = 128
SSD_CONV = 4
SSD_CHUNK = 128
SSD_CONV_DIM = SSD_D_INNER + 2 * SSD_GROUPS * SSD_STATE
SSD_NORM_EPS = 1e-5

CONF_KERNEL = 31
CONF_LN_EPS = 1e-5
CONF_HALO = 32

D_FF = 2816
FFN_CONV = 3
FFN_CHUNK = 256

SUBLANES = 8
VMEM_LIMIT = 56 * 1024 * 1024


def _rms(x, g, eps=NORM_EPS):
    return x * lax.rsqrt(jnp.mean(x * x, axis=-1, keepdims=True) + eps) * g


def _dot(a, b):
    return jnp.dot(a.astype(BF16), b.astype(BF16), preferred_element_type=F32)


def _dot_nt(a, b):
    return lax.dot_general(a.astype(BF16), b.astype(BF16), (((1,), (1,)), ((), ())),
                           preferred_element_type=F32)


def _dot_tn(a, b):
    return lax.dot_general(a.astype(BF16), b.astype(BF16), (((0,), (0,)), ((), ())),
                           preferred_element_type=F32)


def _split2(a):
    hi = a.astype(BF16)
    lo = (a - hi.astype(F32)).astype(BF16)
    return hi, lo


def _split3(a):
    hi = a.astype(BF16)
    r = a - hi.astype(F32)
    mid = r.astype(BF16)
    lo = (r - mid.astype(F32)).astype(BF16)
    return hi, mid, lo


def _dot_exact_rhs(a, b_bf16):
    hi, mid, lo = _split3(a)
    return (jnp.dot(hi, b_bf16, preferred_element_type=F32)
            + jnp.dot(mid, b_bf16, preferred_element_type=F32)
            + jnp.dot(lo, b_bf16, preferred_element_type=F32))


def _dot_exact_lhs(a_bf16, b):
    hi, mid, lo = _split3(b)
    return (jnp.dot(a_bf16, hi, preferred_element_type=F32)
            + jnp.dot(a_bf16, mid, preferred_element_type=F32)
            + jnp.dot(a_bf16, lo, preferred_element_type=F32))


def _const_spec(shape):
    n = len(shape)
    return pl.BlockSpec(shape, lambda *_: (0,) * n, pipeline_mode=pl.Buffered(1))


def _params(n_axes):
    return pltpu.CompilerParams(dimension_semantics=("arbitrary",) * n_axes,
                                vmem_limit_bytes=VMEM_LIMIT)


def _row(v):
    return v.reshape(1, -1).astype(F32)


def _ffn_body(x_ref, g_ref, wup_ref, cw_ref, cb_ref, wdn_ref, gf_ref, o_ref, halo_ref, ubuf_ref,
              *, tiles_per_seq, final):
    tm = x_ref.shape[0]
    fc = FFN_CHUNK
    hs = SUBLANES

    @pl.when(pl.program_id(0) % tiles_per_seq == 0)
    def _():
        halo_ref[...] = jnp.zeros_like(halo_ref)

    x = x_ref[...]
    h = _rms(x, g_ref[...]).astype(BF16)

    def conv(u, off, slot):
        ubuf_ref[slot, 0:hs, :] = halo_ref[:, off:off + fc]
        ubuf_ref[slot, hs:hs + tm, :] = u
        halo_ref[:, off:off + fc] = u[tm - hs:tm, :]
        return (cw_ref[0:1, off:off + fc] * ubuf_ref[slot, hs - 2:hs - 2 + tm, :]
                + cw_ref[1:2, off:off + fc] * ubuf_ref[slot, hs - 1:hs - 1 + tm, :]
                + cw_ref[2:3, off:off + fc] * u + cb_ref[0:1, off:off + fc])

    acc = jnp.zeros((tm, D_MODEL), F32)
    for c in range(D_FF // fc):
        og, ou = c * fc, D_FF + c * fc
        yg = conv(jnp.dot(h, wup_ref[:, og:og + fc], preferred_element_type=F32), og, 0)
        yu = conv(jnp.dot(h, wup_ref[:, ou:ou + fc], preferred_element_type=F32), ou, 1)
        a = (jax.nn.silu(yg) * yu).astype(BF16)
        acc = acc + jnp.dot(a, wdn_ref[c * fc:(c + 1) * fc, :], preferred_element_type=F32)
    y = x + acc
    if final:
        y = _rms(y, gf_ref[...])
    o_ref[...] = y


def _conv_ffn(x2, seq, g, w_up, conv_w, conv_b, w_down, g_final, final, tm=512):
    rows = x2.shape[0]
    body = functools.partial(_ffn_body, tiles_per_seq=seq // tm, final=final)
    return pl.pallas_call(
        body,
        out_shape=jax.ShapeDtypeStruct((rows, D_MODEL), F32),
        grid=(rows // tm,),
        in_specs=[
            pl.BlockSpec((tm, D_MODEL), lambda i: (i, 0)),
            _const_spec((1, D_MODEL)),
            _const_spec((D_MODEL, 2 * D_FF)),
            _const_spec((FFN_CONV, 2 * D_FF)),
            _const_spec((1, 2 * D_FF)),
            _const_spec((D_FF, D_MODEL)),
            _const_spec((1, D_MODEL)),
        ],
        out_specs=pl.BlockSpec((tm, D_MODEL), lambda i: (i, 0)),
        scratch_shapes=[pltpu.VMEM((SUBLANES, 2 * D_FF), F32),
                        pltpu.VMEM((2, tm + SUBLANES, FFN_CHUNK), F32)],
        compiler_params=_params(1),
        name="conv_ffn",
    )(x2, _row(g), w_up.astype(BF16), conv_w.astype(F32), _row(conv_b), w_down.astype(BF16), _row(g_final))


def _proj_res_body(x_ref, y_ref, w_ref, o_ref):
    o_ref[...] = x_ref[...] + jnp.dot(y_ref[...].astype(BF16), w_ref[...], preferred_element_type=F32)


def _proj_res(x2, y2, w, tm=512):
    rows, k = y2.shape
    return pl.pallas_call(
        _proj_res_body,
        out_shape=jax.ShapeDtypeStruct((rows, D_MODEL), F32),
        grid=(rows // tm,),
        in_specs=[pl.BlockSpec((tm, D_MODEL), lambda i: (i, 0)),
                  pl.BlockSpec((tm, k), lambda i: (i, 0)),
                  _const_spec((k, D_MODEL))],
        out_specs=pl.BlockSpec((tm, D_MODEL), lambda i: (i, 0)),
        compiler_params=_params(1),
        name="proj_res",
    )(x2, y2, w.astype(BF16))


def _moba_qkv_body(x_ref, g_ref, wt_ref, o_ref, *, nc):
    h = _rms(x_ref[0], g_ref[...]).astype(BF16)
    for n in range(3 * D_MODEL // nc):
        o_ref[0, n * nc:(n + 1) * nc, :] = lax.dot_general(
            wt_ref[n * nc:(n + 1) * nc, :], h, (((1,), (1,)), ((), ())),
            preferred_element_type=F32).astype(BF16)


def _moba_attn_body(q_ref, k_ref, v_ref, o_ref, krows_ref, vblk_ref, kmean_ref, sel_ref, *, nb):
    bs = MOBA_BLOCK
    iq = pl.program_id(2)

    @pl.when(iq == 0)
    def _():
        for j in range(nb):
            kr = k_ref[0, :, j * bs:(j + 1) * bs].astype(F32).T
            krows_ref[j] = kr.astype(BF16)
            kmean_ref[j:j + 1, :] = jnp.mean(kr, axis=0, keepdims=True)
            vblk_ref[j] = v_ref[0, :, j * bs:(j + 1) * bs]

    q = q_ref[0] * (MOBA_HEAD_DIM ** -0.5)
    km_hi, km_lo = _split2(kmean_ref[...])
    gate = (jnp.dot(km_hi, q, preferred_element_type=F32)
            + jnp.dot(km_lo, q, preferred_element_type=F32))
    jidx = lax.broadcasted_iota(jnp.int32, (nb, bs), 0)
    cnt = jnp.zeros((nb, bs), jnp.int32)
    for jp in range(nb):
        gj = gate[jp:jp + 1, :]
        beats = (gj > gate) | ((gj == gate) & (jp < jidx))
        cnt = cnt + jnp.where(beats & (jp < iq), 1, 0)
    selm = (cnt < MOBA_TOPK) & (jidx < iq)
    sel_ref[...] = jnp.where(selm, 1.0, 0.0)

    s = jnp.dot(krows_ref[iq], q, preferred_element_type=F32)
    kpos = lax.broadcasted_iota(jnp.int32, (bs, bs), 0)
    qpos = lax.broadcasted_iota(jnp.int32, (bs, bs), 1)
    s = jnp.where(kpos <= qpos, s, NEG_INF)
    m0 = jnp.max(s, axis=0, keepdims=True)
    p = jnp.exp(s - m0)
    l0 = jnp.sum(p, axis=0, keepdims=True)
    acc0 = jnp.dot(vblk_ref[iq], p.astype(BF16), preferred_element_type=F32)

    def step(j, carry):
        m, l, acc = carry
        sj = jnp.dot(krows_ref[j], q, preferred_element_type=F32)
        sj = jnp.where(sel_ref[pl.ds(j, 1), :] > 0.0, sj, NEG_INF)
        m_new = jnp.maximum(m, jnp.max(sj, axis=0, keepdims=True))
        alpha = jnp.exp(m - m_new)
        pj = jnp.exp(sj - m_new)
        l = alpha * l + jnp.sum(pj, axis=0, keepdims=True)
        acc = alpha * acc + jnp.dot(vblk_ref[j], pj.astype(BF16), preferred_element_type=F32)
        return m_new, l, acc

    _, l, acc = lax.fori_loop(0, iq, step, (m0, l0, acc0))
    o_ref[0] = (acc / l).astype(BF16)


def _moba_out_body(x_ref, o_ref_in, w_ref, o_ref):
    o_ref[0] = x_ref[0] + lax.dot_general(o_ref_in[0], w_ref[...], (((0,), (0,)), ((), ())),
                                          preferred_element_type=F32)


def _moba(x, g, w_qkv, w_o, ts=512):
    b, s, d = x.shape
    nb = s // MOBA_BLOCK
    dh = MOBA_HEAD_DIM
    qkvt = pl.pallas_call(
        functools.partial(_moba_qkv_body, nc=512),
        out_shape=jax.ShapeDtypeStruct((b, 3 * d, s), BF16),
        grid=(b, s // ts),
        in_specs=[pl.BlockSpec((1, ts, d), lambda i, j: (i, j, 0)),
                  _const_spec((1, d)),
                  _const_spec((3 * d, d))],
        out_specs=pl.BlockSpec((1, 3 * d, ts), lambda i, j: (i, 0, j)),
        compiler_params=_params(2),
        name="moba_qkv",
    )(x, _row(g), w_qkv.T.astype(BF16))

    h = MOBA_HEADS
    ot = pl.pallas_call(
        functools.partial(_moba_attn_body, nb=nb),
        out_shape=jax.ShapeDtypeStruct((b, d, s), BF16),
        grid=(b, h, nb),
        in_specs=[pl.BlockSpec((1, dh, MOBA_BLOCK), lambda i, j, k: (i, j, k)),
                  pl.BlockSpec((1, dh, s), lambda i, j, k: (i, h + j, 0)),
                  pl.BlockSpec((1, dh, s), lambda i, j, k: (i, 2 * h + j, 0))],
        out_specs=pl.BlockSpec((1, dh, MOBA_BLOCK), lambda i, j, k: (i, j, k)),
        scratch_shapes=[pltpu.VMEM((nb, MOBA_BLOCK, dh), BF16),
                        pltpu.VMEM((nb, dh, MOBA_BLOCK), BF16),
                        pltpu.VMEM((nb, dh), F32),
                        pltpu.VMEM((nb, MOBA_BLOCK), F32)],
        compiler_params=_params(3),
        name="moba_attn",
    )(qkvt, qkvt, qkvt)

    return pl.pallas_call(
        _moba_out_body,
        out_shape=jax.ShapeDtypeStruct((b, s, d), F32),
        grid=(b, s // ts),
        in_specs=[pl.BlockSpec((1, ts, d), lambda i, j: (i, j, 0)),
                  pl.BlockSpec((1, d, ts), lambda i, j: (i, 0, j)),
                  _const_spec((d, d))],
        out_specs=pl.BlockSpec((1, ts, d), lambda i, j: (i, j, 0)),
        compiler_params=_params(2),
        name="moba_out",
    )(x, ot, w_o.astype(BF16))


def _rwkv_body(x_ref, g_ref, mu_ref, wr_ref, wk_ref, wv_ref, w0_ref, w1_ref, w2_ref, a0_ref, a1_ref,
               a2_ref, g1_ref, g2_ref, kk_ref, ka_ref, rk_ref, gnw_ref, gnb_ref, wo_ref, p_ref, pt_ref,
               o_ref,
               hbuf_ref, st_ref, r_s, lw_s, k_s, v_s, a_s, b_s, y_s, gate_s, bonus_s, ch_s,
               *, tiles_per_seq):
    tm = x_ref.shape[0]
    lc = RWKV_CHUNK
    hd = RWKV_HEAD_DIM
    hs = SUBLANES

    @pl.when(pl.program_id(0) % tiles_per_seq == 0)
    def _():
        hbuf_ref[tm:tm + hs, :] = jnp.zeros((hs, D_MODEL), F32)
        st_ref[...] = jnp.zeros_like(st_ref)

    def seg_sum(v):
        hi, lo = _split2(v)
        return (jnp.dot(hi, p_ref[...], preferred_element_type=F32)
                + jnp.dot(lo, p_ref[...], preferred_element_type=F32))

    def seg_bcast(v):
        hi, lo = _split2(v)
        return (jnp.dot(hi, pt_ref[...], preferred_element_type=F32)
                + jnp.dot(lo, pt_ref[...], preferred_element_type=F32))

    x = x_ref[...]
    hn = _rms(x, g_ref[...])
    hbuf_ref[0:hs, :] = hbuf_ref[tm:tm + hs, :]
    hbuf_ref[hs:hs + tm, :] = hn
    xx = hbuf_ref[hs - 1:hs - 1 + tm, :] - hn

    def mix(i):
        return hn + xx * mu_ref[i:i + 1, :]

    r = _dot(mix(0), wr_ref[...])
    k = _dot(mix(1), wk_ref[...])
    v = _dot(mix(2), wv_ref[...])
    wpre = w0_ref[...] + _dot(jnp.tanh(_dot(mix(3), w1_ref[...])), w2_ref[...])
    lw = -jnp.exp(-jax.nn.softplus(-wpre) - 0.5)
    a = jax.nn.sigmoid(a0_ref[...] + _dot(_dot(mix(4), a1_ref[...]), a2_ref[...]))
    gate_s[...] = _dot(jax.nn.sigmoid(_dot(mix(5), g1_ref[...])), g2_ref[...])

    kk = k * kk_ref[...]
    inv = 1.0 / jnp.maximum(jnp.sqrt(seg_sum(kk * kk)), 1e-12)
    kk = kk * seg_bcast(inv)
    k = k * (1.0 + (a - 1.0) * ka_ref[...])
    bonus_s[...] = seg_bcast(seg_sum(r * k * rk_ref[...])) * v
    r_s[...] = r
    lw_s[...] = lw
    k_s[...] = k
    v_s[...] = v
    a_s[...] = -kk
    b_s[...] = kk * a

    ti = lax.broadcasted_iota(jnp.int32, (lc, lc), 0)
    si = lax.broadcasted_iota(jnp.int32, (lc, lc), 1)
    low_strict = si < ti
    low_incl = si <= ti
    tril = jnp.where(low_incl, 1.0, 0.0).astype(BF16)

    def chunk(c, carry):
        rows = pl.ds(pl.multiple_of(c * lc, lc), lc)
        lw_c = lw_s[rows, :]
        lg = _dot_exact_lhs(tril, lw_c)
        lg_last = lg[lc - 1:lc, :]
        g_inv = jnp.exp(-lg)
        g_tail = jnp.exp(lg_last - lg)
        g_tot = jnp.exp(lg_last)
        ch_s[0, 0:lc, :] = a_s[rows, :] * jnp.exp(lg - lw_c)
        ch_s[0, lc:2 * lc, :] = r_s[rows, :] * jnp.exp(lg)
        ch_s[1, 0:lc, :] = b_s[rows, :] * g_inv
        ch_s[1, lc:2 * lc, :] = k_s[rows, :] * g_inv
        ch_s[2, 0:lc, :] = b_s[rows, :] * g_tail
        ch_s[2, lc:2 * lc, :] = k_s[rows, :] * g_tail
        for h in range(RWKV_HEADS):
            sl = slice(h * hd, (h + 1) * hd)
            ar = ch_s[0, :, sl]
            bk = ch_s[1, :, sl]
            gm = _dot_nt(ar, bk)
            sh = st_ref[h]
            ars = _dot_nt(ar, sh)
            n1 = jnp.where(low_strict, gm[0:lc, 0:lc], 0.0)
            mak = jnp.where(low_strict, gm[0:lc, lc:2 * lc], 0.0)
            mrb = jnp.where(low_incl, gm[lc:2 * lc, 0:lc], 0.0)
            mrk = jnp.where(low_incl, gm[lc:2 * lc, lc:2 * lc], 0.0)
            vh = v_s[rows, sl]
            u = ars[0:lc, :] + _dot(mak, vh)
            npow = n1
            u = u + _dot(npow, u)
            steps = 1
            while 2 * steps < lc:
                npow = _dot(npow, npow)
                u = u + _dot(npow, u)
                steps *= 2
            uv = jnp.concatenate([u, vh], axis=0)
            y = ars[lc:2 * lc, :] + _dot(jnp.concatenate([mrb, mrk], axis=1), uv)
            y_s[rows, sl] = y
            st_ref[h] = sh * g_tot[:, sl] + _dot_tn(uv, ch_s[2, :, sl])
        return carry

    lax.fori_loop(0, tm // lc, chunk, 0)

    y = y_s[...]
    mean = seg_bcast(seg_sum(y)) * (1.0 / hd)
    dy = y - mean
    var = seg_bcast(seg_sum(dy * dy)) * (1.0 / hd)
    yn = dy * lax.rsqrt(var + RWKV_GN_EPS) * gnw_ref[...] + gnb_ref[...]
    o_ref[...] = x + _dot((yn + bonus_s[...]) * gate_s[...], wo_ref[...])


def _rwkv7(x2, seq, g, mu, w_rkv, w0, w1, w2, a0, a1, a2, g1, g2, k_k, k_a, r_k, gn_w, gn_b, w_o, tm=256):
    rows = x2.shape[0]
    d = D_MODEL
    head_of = jnp.arange(d) // RWKV_HEAD_DIM
    p = (head_of[:, None] == jnp.arange(128)[None, :]).astype(BF16)
    args = (x2, _row(g), mu.astype(F32), w_rkv[0].astype(BF16), w_rkv[1].astype(BF16), w_rkv[2].astype(BF16),
            _row(w0), w1.astype(BF16), w2.astype(BF16), _row(a0), a1.astype(BF16), a2.astype(BF16),
            g1.astype(BF16), g2.astype(BF16), _row(k_k), _row(k_a), _row(r_k), _row(gn_w), _row(gn_b),
            w_o.astype(BF16), p, p.T)
    in_specs = [pl.BlockSpec((tm, d), lambda i: (i, 0))] + [_const_spec(a.shape) for a in args[1:]]
    big = pltpu.VMEM((tm, d), F32)
    return pl.pallas_call(
        functools.partial(_rwkv_body, tiles_per_seq=seq // tm),
        out_shape=jax.ShapeDtypeStruct((rows, d), F32),
        grid=(rows // tm,),
        in_specs=in_specs,
        out_specs=pl.BlockSpec((tm, d), lambda i: (i, 0)),
        scratch_shapes=[pltpu.VMEM((tm + 2 * SUBLANES, d), F32),
                        pltpu.VMEM((RWKV_HEADS, RWKV_HEAD_DIM, RWKV_HEAD_DIM), F32),
                        big, big, big, big, big, big, big, big, big,
                        pltpu.VMEM((3, 2 * RWKV_CHUNK, d), F32)],
        compiler_params=_params(1),
        name="rwkv7",
    )(*args)


def _ssd_in_body(x_ref, g_ref, w_ref, wdt_ref, zx_ref, dt_ref, *, nc):
    hn = _rms(x_ref[...], g_ref[...])
    h = hn.astype(BF16)
    for n in range(w_ref.shape[1] // nc):
        zx_ref[:, n * nc:(n + 1) * nc] = jnp.dot(h, w_ref[:, n * nc:(n + 1) * nc],
                                                 preferred_element_type=F32)
    dt_ref[...] = jnp.dot(hn, wdt_ref[...], preferred_element_type=F32, precision=lax.Precision.HIGHEST)


def _ssd_core_body(zx_ref, dt_ref, cw_ref, cb_ref, dtb_ref, alog_ref, dskip_ref, nw_ref, e_ref, o_ref,
                   cbuf_ref, state_ref, y_s):
    lc = SSD_CHUNK
    di = SSD_D_INNER
    hs = SUBLANES
    gw = di // SSD_GROUPS
    ns = SSD_STATE

    @pl.when(pl.program_id(1) == 0)
    def _():
        cbuf_ref[lc:lc + hs, :] = jnp.zeros((hs, SSD_CONV_DIM), F32)
        state_ref[...] = jnp.zeros_like(state_ref)

    z = zx_ref[:, 0:di]
    xbc_raw = zx_ref[:, di:di + SSD_CONV_DIM]
    cbuf_ref[0:hs, :] = cbuf_ref[lc:lc + hs, :]
    cbuf_ref[hs:hs + lc, :] = xbc_raw
    xbc = cb_ref[...] + cw_ref[SSD_CONV - 1:SSD_CONV, :] * xbc_raw
    for j in range(SSD_CONV - 1):
        o = hs - (SSD_CONV - 1) + j
        xbc = xbc + cw_ref[j:j + 1, :] * cbuf_ref[o:o + lc, :]
    xbc = jax.nn.silu(xbc)
    xs = xbc[:, 0:di]
    bm = xbc[:, di:di + SSD_GROUPS * ns]
    cm = xbc[:, di + SSD_GROUPS * ns:]

    dt = jax.nn.softplus(dt_ref[...] + dtb_ref[...])
    adt = dt * (-jnp.exp(alog_ref[...]))
    li = lax.broadcasted_iota(jnp.int32, (lc, lc), 0)
    si = lax.broadcasted_iota(jnp.int32, (lc, lc), 1)
    causal = si <= li
    tril = jnp.where(causal, 1.0, 0.0).astype(BF16)
    acs = _dot_exact_lhs(tril, adt)
    acs_t = acs.T
    a_last = acs[lc - 1:lc, :]

    def expand(v):
        hi, lo = _split2(v)
        return (jnp.dot(hi, e_ref[...], preferred_element_type=F32)
                + jnp.dot(lo, e_ref[...], preferred_element_type=F32))

    exp_acs_x = expand(jnp.exp(acs))
    xd = xs * expand(dt)
    xdec = xd * expand(jnp.exp(a_last - acs))

    for g in range(SSD_GROUPS):
        bg = bm[:, g * ns:(g + 1) * ns]
        cg = cm[:, g * ns:(g + 1) * ns]
        cb = _dot_nt(cg, bg)
        gsl = slice(g * gw, (g + 1) * gw)
        st_g = state_ref[:, gsl]
        y_off = _dot(cg, st_g) * exp_acs_x[:, gsl]
        for e in range(SSD_HEADS // SSD_GROUPS):
            h = g * (SSD_HEADS // SSD_GROUPS) + e
            col = jnp.broadcast_to(acs[:, h:h + 1], (lc, lc))
            row = jnp.broadcast_to(acs_t[h:h + 1, :], (lc, lc))
            lm = jnp.where(causal, jnp.exp(jnp.minimum(col - row, 0.0)), 0.0)
            hsl = slice(h * SSD_HEAD_DIM, (h + 1) * SSD_HEAD_DIM)
            y_s[:, hsl] = _dot(cb * lm, xd[:, hsl])
        state_ref[:, gsl] = st_g * exp_acs_x[lc - 1:lc, gsl] + _dot_tn(bg, xdec[:, gsl])
        y_s[:, gsl] = y_s[:, gsl] + y_off

    y = y_s[...] + xs * dskip_ref[...]
    yz = y * jax.nn.silu(z)
    for g in range(SSD_GROUPS):
        gsl = slice(g * gw, (g + 1) * gw)
        yg = yz[:, gsl]
        o_ref[:, gsl] = (yg * lax.rsqrt(jnp.mean(yg * yg, axis=-1, keepdims=True) + SSD_NORM_EPS)
                         * nw_ref[:, gsl])


def _mamba2(x2, batch, seq, g, w_in, conv_w, conv_b, dt_bias, a_log, d_skip, norm_w, w_out, tm=512):
    rows = x2.shape[0]
    d = D_MODEL
    di = SSD_D_INNER
    nzx = di + SSD_CONV_DIM
    zx, dt = pl.pallas_call(
        functools.partial(_ssd_in_body, nc=512),
        out_shape=(jax.ShapeDtypeStruct((rows, nzx), F32), jax.ShapeDtypeStruct((rows, SSD_HEADS), F32)),
        grid=(rows // tm,),
        in_specs=[pl.BlockSpec((tm, d), lambda i: (i, 0)),
                  _const_spec((1, d)),
                  _const_spec((d, nzx)),
                  _const_spec((d, SSD_HEADS))],
        out_specs=(pl.BlockSpec((tm, nzx), lambda i: (i, 0)),
                   pl.BlockSpec((tm, SSD_HEADS), lambda i: (i, 0))),
        compiler_params=_params(1),
        name="ssd_in",
    )(x2, _row(g), w_in[:, :nzx].astype(BF16), w_in[:, nzx:].astype(F32))

    lc = SSD_CHUNK
    nchunk = seq // lc
    expand = (jnp.arange(SSD_HEADS)[:, None] == (jnp.arange(di) // SSD_HEAD_DIM)[None, :]).astype(BF16)
    d_x = jnp.repeat(d_skip.astype(F32), SSD_HEAD_DIM).reshape(1, di)
    args = (zx, dt, conv_w.astype(F32), _row(conv_b), _row(dt_bias), _row(a_log), d_x, _row(norm_w), expand)
    yz = pl.pallas_call(
        _ssd_core_body,
        out_shape=jax.ShapeDtypeStruct((rows, di), F32),
        grid=(batch, nchunk),
        in_specs=[pl.BlockSpec((lc, nzx), lambda b, c: (b * nchunk + c, 0)),
                  pl.BlockSpec((lc, SSD_HEADS), lambda b, c: (b * nchunk + c, 0))]
                 + [_const_spec(a.shape) for a in args[2:]],
        out_specs=pl.BlockSpec((lc, di), lambda b, c: (b * nchunk + c, 0)),
        scratch_shapes=[pltpu.VMEM((lc + 2 * SUBLANES, SSD_CONV_DIM), F32),
                        pltpu.VMEM((SSD_STATE, di), F32),
                        pltpu.VMEM((lc, di), F32)],
        compiler_params=_params(2),
        name="ssd_core",
    )(*args)
    return _proj_res(x2, yz, w_out)


def _conf_body(x_ref, g_ref, w1_ref, b1_ref, dw_ref, dwb_ref, lnw_ref, lnb_ref, w2_ref, b2_ref, o_ref,
               ubuf_ref, c_s, *, tiles_per_seq, rb, lanes):
    tm = x_ref.shape[0]
    d = D_MODEL
    hl = CONF_HALO

    @pl.when(pl.program_id(0) % tiles_per_seq == 0)
    def _():
        ubuf_ref[tm:tm + hl, :] = jnp.zeros((hl, d), F32)

    x = x_ref[...]
    h = _rms(x, g_ref[...]).astype(BF16)
    p = jnp.dot(h, w1_ref[...], preferred_element_type=F32) + b1_ref[...]
    u = p[:, 0:d] * jax.nn.sigmoid(p[:, d:2 * d])
    ubuf_ref[0:hl, :] = ubuf_ref[tm:tm + hl, :]
    ubuf_ref[hl:hl + tm, :] = u
    base = hl - (CONF_KERNEL - 1)
    for r0 in range(0, tm, rb):
        for c0 in range(0, d, lanes):
            acc = jnp.broadcast_to(dwb_ref[:, c0:c0 + lanes], (rb, lanes))
            for j in range(CONF_KERNEL):
                acc = acc + dw_ref[j:j + 1, c0:c0 + lanes] * ubuf_ref[r0 + base + j:r0 + base + j + rb, c0:c0 + lanes]
            c_s[r0:r0 + rb, c0:c0 + lanes] = acc
    y = c_s[...]
    mu = jnp.mean(y, axis=-1, keepdims=True)
    dy = y - mu
    var = jnp.mean(dy * dy, axis=-1, keepdims=True)
    zn = dy * lax.rsqrt(var + CONF_LN_EPS) * lnw_ref[...] + lnb_ref[...]
    o_ref[...] = x + _dot(jax.nn.silu(zn), w2_ref[...]) + b2_ref[...]


def _conformer(x2, seq, g, w_pw1, b_pw1, dw_w, dw_b, ln_w, ln_b, w_pw2, b_pw2, tm=256):
    rows = x2.shape[0]
    d = D_MODEL
    args = (x2, _row(g), w_pw1.astype(BF16), _row(b_pw1), dw_w.astype(F32), _row(dw_b), _row(ln_w),
            _row(ln_b), w_pw2.astype(BF16), _row(b_pw2))
    return pl.pallas_call(
        functools.partial(_conf_body, tiles_per_seq=seq // tm, rb=64, lanes=256),
        out_shape=jax.ShapeDtypeStruct((rows, d), F32),
        grid=(rows // tm,),
        in_specs=[pl.BlockSpec((tm, d), lambda i: (i, 0))] + [_const_spec(a.shape) for a in args[1:]],
        out_specs=pl.BlockSpec((tm, d), lambda i: (i, 0)),
        scratch_shapes=[pltpu.VMEM((tm + CONF_HALO, d), F32), pltpu.VMEM((tm, d), F32)],
        compiler_params=_params(1),
        name="conformer",
    )(*args)


def kernel(x, norm_mix, norm_ffn, norm_final, ffn_w_up, ffn_conv_w, ffn_conv_b, ffn_w_down, moba_w_qkv, moba_w_o, rwkv_mu, rwkv_w_rkv, rwkv_w0, rwkv_w1, rwkv_w2, rwkv_a0, rwkv_a1, rwkv_a2, rwkv_g1, rwkv_g2, rwkv_k_k, rwkv_k_a, rwkv_r_k, rwkv_gn_w, rwkv_gn_b, rwkv_w_o, ssd_w_in, ssd_conv_w, ssd_conv_b, ssd_dt_bias, ssd_a_log, ssd_d, ssd_norm_w, ssd_w_out, conf_w_pw1, conf_b_pw1, conf_dw_w, conf_dw_b, conf_ln_w, conf_ln_b, conf_w_pw2, conf_b_pw2):
    b, s, d = x.shape
    depth = norm_mix.shape[0]
    rows = b * s
    for i in range(depth):
        kind, j = i % 4, i // 4
        if kind == 0:
            x2 = _moba(x.reshape(b, s, d), norm_mix[i], moba_w_qkv[j], moba_w_o[j]).reshape(rows, d)
        elif kind == 1:
            x2 = _rwkv7(x.reshape(rows, d), s, norm_mix[i], rwkv_mu[j], rwkv_w_rkv[j], rwkv_w0[j], rwkv_w1[j],
                        rwkv_w2[j], rwkv_a0[j], rwkv_a1[j], rwkv_a2[j], rwkv_g1[j], rwkv_g2[j], rwkv_k_k[j],
                        rwkv_k_a[j], rwkv_r_k[j], rwkv_gn_w[j], rwkv_gn_b[j], rwkv_w_o[j])
        elif kind == 2:
            x2 = _mamba2(x.reshape(rows, d), b, s, norm_mix[i], ssd_w_in[j], ssd_conv_w[j], ssd_conv_b[j],
                         ssd_dt_bias[j], ssd_a_log[j], ssd_d[j], ssd_norm_w[j], ssd_w_out[j])
        else:
            x2 = _conformer(x.reshape(rows, d), s, norm_mix[i], conf_w_pw1[j], conf_b_pw1[j], conf_dw_w[j],
                            conf_dw_b[j], conf_ln_w[j], conf_ln_b[j], conf_w_pw2[j], conf_b_pw2[j])
        x = _conv_ffn(x2, s, norm_ffn[i], ffn_w_up[i], ffn_conv_w[i], ffn_conv_b[i], ffn_w_down[i],
                      norm_final, final=(i == depth - 1))
    return x.reshape(b, s, d)
```

```python
import functools

import jax
import jax.numpy as jnp
from jax import lax
from jax.experimental import pallas as pl
from jax.experimental.pallas import tpu as pltpu

F32 = jnp.float32
BF16 = jnp.bfloat16

D_MODEL = 1024
NORM_EPS = 1e-6
NEG_INF = -1e30

MOBA_HEADS = 16
MOBA_HEAD_DIM = 64
MOBA_BLOCK = 256
MOBA_TOPK = 3
MOBA_HEAD_GROUP = 4

RWKV_HEADS = 16
RWKV_HEAD_DIM = 64
RWKV_GN_EPS = 64e-5
RWKV_CHUNK = 64
RWKV_HEAD_GROUP = 8

SSD_D_INNER = 2048
SSD_HEAD_DIM = 64
SSD_HEADS = 32
SSD_GROUPS = 4
SSD_STATE = 128
SSD_CONV = 4
SSD_CHUNK = 128
SSD_CONV_DIM = SSD_D_INNER + 2 * SSD_GROUPS * SSD_STATE
SSD_NORM_EPS = 1e-5

CONF_KERNEL = 31
CONF_LN_EPS = 1e-5
CONF_HALO = 32

D_FF = 2816
FFN_CONV = 3
FFN_CHUNK = 256

SUBLANES = 8
VMEM_LIMIT = 56 * 1024 * 1024


def _rms(x, g, eps=NORM_EPS):
    return x * lax.rsqrt(jnp.mean(x * x, axis=-1, keepdims=True) + eps) * g


def _dot(a, b):
    return jnp.dot(a.astype(BF16), b.astype(BF16), preferred_element_type=F32)


def _dot_nt(a, b):
    return lax.dot_general(a.astype(BF16), b.astype(BF16), (((1,), (1,)), ((), ())),
                           preferred_element_type=F32)


def _dot_tn(a, b):
    return lax.dot_general(a.astype(BF16), b.astype(BF16), (((0,), (0,)), ((), ())),
                           preferred_element_type=F32)


def _split2(a):
    hi = a.astype(BF16)
    lo = (a - hi.astype(F32)).astype(BF16)
    return hi, lo


def _split3(a):
    hi = a.astype(BF16)
    r = a - hi.astype(F32)
    mid = r.astype(BF16)
    lo = (r - mid.astype(F32)).astype(BF16)
    return hi, mid, lo


def _dot_exact_rhs(a, b_bf16):
    hi, mid, lo = _split3(a)
    return (jnp.dot(hi, b_bf16, preferred_element_type=F32)
            + jnp.dot(mid, b_bf16, preferred_element_type=F32)
            + jnp.dot(lo, b_bf16, preferred_element_type=F32))


def _dot_exact_lhs(a_bf16, b):
    hi, mid, lo = _split3(b)
    return (jnp.dot(a_bf16, hi, preferred_element_type=F32)
            + jnp.dot(a_bf16, mid, preferred_element_type=F32)
            + jnp.dot(a_bf16, lo, preferred_element_type=F32))


def _const_spec(shape):
    n = len(shape)
    return pl.BlockSpec(shape, lambda *_: (0,) * n, pipeline_mode=pl.Buffered(1))


def _params(n_axes):
    return pltpu.CompilerParams(dimension_semantics=("arbitrary",) * n_axes,
                                vmem_limit_bytes=VMEM_LIMIT)


def _row(v):
    return v.reshape(1, -1).astype(F32)


def _ffn_body(x_ref, g_ref, wup_ref, cw_ref, cb_ref, wdn_ref, gf_ref, o_ref, halo_ref, ubuf_ref,
              *, tiles_per_seq, final):
    tm = x_ref.shape[0]
    fc = FFN_CHUNK
    hs = SUBLANES

    @pl.when(pl.program_id(0) % tiles_per_seq == 0)
    def _():
        halo_ref[...] = jnp.zeros_like(halo_ref)

    x = x_ref[...]
    h = _rms(x, g_ref[...]).astype(BF16)

    def conv(u, off, slot):
        ubuf_ref[slot, 0:hs, :] = halo_ref[:, off:off + fc]
        ubuf_ref[slot, hs:hs + tm, :] = u
        halo_ref[:, off:off + fc] = u[tm - hs:tm, :]
        return (cw_ref[0:1, off:off + fc] * ubuf_ref[slot, hs - 2:hs - 2 + tm, :]
                + cw_ref[1:2, off:off + fc] * ubuf_ref[slot, hs - 1:hs - 1 + tm, :]
                + cw_ref[2:3, off:off + fc] * u + cb_ref[0:1, off:off + fc])

    acc = jnp.zeros((tm, D_MODEL), F32)
    for c in range(D_FF // fc):
        og, ou = c * fc, D_FF + c * fc
        yg = conv(jnp.dot(h, wup_ref[:, og:og + fc], preferred_element_type=F32), og, 0)
        yu = conv(jnp.dot(h, wup_ref[:, ou:ou + fc], preferred_element_type=F32), ou, 1)
        a = (jax.nn.silu(yg) * yu).astype(BF16)
        acc = acc + jnp.dot(a, wdn_ref[c * fc:(c + 1) * fc, :], preferred_element_type=F32)
    y = x + acc
    if final:
        y = _rms(y, gf_ref[...])
    o_ref[...] = y


def _conv_ffn(x2, seq, g, w_up, conv_w, conv_b, w_down, g_final, final, tm=512):
    rows = x2.shape[0]
    body = functools.partial(_ffn_body, tiles_per_seq=seq // tm, final=final)
    return pl.pallas_call(
        body,
        out_shape=jax.ShapeDtypeStruct((rows, D_MODEL), F32),
        grid=(rows // tm,),
        in_specs=[
            pl.BlockSpec((tm, D_MODEL), lambda i: (i, 0)),
            _const_spec((1, D_MODEL)),
            _const_spec((D_MODEL, 2 * D_FF)),
            _const_spec((FFN_CONV, 2 * D_FF)),
            _const_spec((1, 2 * D_FF)),
            _const_spec((D_FF, D_MODEL)),
            _const_spec((1, D_MODEL)),
        ],
        out_specs=pl.BlockSpec((tm, D_MODEL), lambda i: (i, 0)),
        scratch_shapes=[pltpu.VMEM((SUBLANES, 2 * D_FF), F32),
                        pltpu.VMEM((2, tm + SUBLANES, FFN_CHUNK), F32)],
        compiler_params=_params(1),
        name="conv_ffn",
    )(x2, _row(g), w_up.astype(BF16), conv_w.astype(F32), _row(conv_b), w_down.astype(BF16), _row(g_final))


def _proj_res_body(x_ref, y_ref, w_ref, o_ref):
    o_ref[...] = x_ref[...] + jnp.dot(y_ref[...].astype(BF16), w_ref[...], preferred_element_type=F32)


def _proj_res(x2, y2, w, tm=512):
    rows, k = y2.shape
    return pl.pallas_call(
        _proj_res_body,
        out_shape=jax.ShapeDtypeStruct((rows, D_MODEL), F32),
        grid=(rows // tm,),
        in_specs=[pl.BlockSpec((tm, D_MODEL), lambda i: (i, 0)),
                  pl.BlockSpec((tm, k), lambda i: (i, 0)),
                  _const_spec((k, D_MODEL))],
        out_specs=pl.BlockSpec((tm, D_MODEL), lambda i: (i, 0)),
        compiler_params=_params(1),
        name="proj_res",
    )(x2, y2, w.astype(BF16))


def _moba_qkv_body(x_ref, g_ref, wt_ref, o_ref, *, nc):
    h = _rms(x_ref[0], g_ref[...]).astype(BF16)
    for n in range(3 * D_MODEL // nc):
        o_ref[0, n * nc:(n + 1) * nc, :] = lax.dot_general(
            wt_ref[n * nc:(n + 1) * nc, :], h, (((1,), (1,)), ((), ())),
            preferred_element_type=F32).astype(BF16)


def _moba_attn_body(q_ref, k_ref, v_ref, o_ref, krows_ref, vblk_ref, kmean_ref, sel_ref, *, nb):
    bs = MOBA_BLOCK
    dh = MOBA_HEAD_DIM
    hg = MOBA_HEAD_GROUP
    hh = range(hg)
    iq = pl.program_id(2)

    @pl.when(iq == 0)
    def _():
        for j in range(nb):
            kr = k_ref[0, :, j * bs:(j + 1) * bs].astype(F32).T
            for i in hh:
                kri = kr[:, i * dh:(i + 1) * dh]
                krows_ref[i, j] = kri.astype(BF16)
                kmean_ref[i, j:j + 1, :] = jnp.mean(kri, axis=0, keepdims=True)
                vblk_ref[i, j] = v_ref[0, i * dh:(i + 1) * dh, j * bs:(j + 1) * bs]

    q = [q_ref[0, i * dh:(i + 1) * dh, :] * (dh ** -0.5) for i in hh]
    jidx = lax.broadcasted_iota(jnp.int32, (nb, bs), 0)
    for i in hh:
        km_hi, km_lo = _split2(kmean_ref[i])
        gate = (jnp.dot(km_hi, q[i], preferred_element_type=F32)
                + jnp.dot(km_lo, q[i], preferred_element_type=F32))
        cnt = jnp.zeros((nb, bs), jnp.int32)
        for jp in range(nb):
            gj = gate[jp:jp + 1, :]
            beats = (gj > gate) | ((gj == gate) & (jp < jidx))
            cnt = cnt + jnp.where(beats & (jp < iq), 1, 0)
        selm = (cnt < MOBA_TOPK) & (jidx < iq)
        sel_ref[i] = jnp.where(selm, 1.0, 0.0)

    kpos = lax.broadcasted_iota(jnp.int32, (bs, bs), 0)
    qpos = lax.broadcasted_iota(jnp.int32, (bs, bs), 1)
    causal = kpos <= qpos
    s = [jnp.where(causal, jnp.dot(krows_ref[i, iq], q[i], preferred_element_type=F32), NEG_INF) for i in hh]
    m0 = [jnp.max(s[i], axis=0, keepdims=True) for i in hh]
    p = [jnp.exp(s[i] - m0[i]) for i in hh]
    l0 = [jnp.sum(p[i], axis=0, keepdims=True) for i in hh]
    acc0 = [jnp.dot(vblk_ref[i, iq], p[i].astype(BF16), preferred_element_type=F32) for i in hh]

    def step(j, carry):
        m, l, acc = carry
        sj = [jnp.where(sel_ref[i, pl.ds(j, 1), :] > 0.0,
                        jnp.dot(krows_ref[i, j], q[i], preferred_element_type=F32), NEG_INF) for i in hh]
        m_new = [jnp.maximum(m[i], jnp.max(sj[i], axis=0, keepdims=True)) for i in hh]
        alpha = [jnp.exp(m[i] - m_new[i]) for i in hh]
        pj = [jnp.exp(sj[i] - m_new[i]) for i in hh]
        l = [alpha[i] * l[i] + jnp.sum(pj[i], axis=0, keepdims=True) for i in hh]
        acc = [alpha[i] * acc[i] + jnp.dot(vblk_ref[i, j], pj[i].astype(BF16), preferred_element_type=F32)
               for i in hh]
        return m_new, l, acc

    _, l, acc = lax.fori_loop(0, iq, step, (m0, l0, acc0))
    for i in hh:
        o_ref[0, i * dh:(i + 1) * dh, :] = (acc[i] / l[i]).astype(BF16)


def _moba_out_body(x_ref, o_ref_in, w_ref, o_ref):
    o_ref[0] = x_ref[0] + lax.dot_general(o_ref_in[0], w_ref[...], (((0,), (0,)), ((), ())),
                                          preferred_element_type=F32)


def _moba(x, g, w_qkv, w_o, ts=512):
    b, s, d = x.shape
    nb = s // MOBA_BLOCK
    dh = MOBA_HEAD_DIM
    qkvt = pl.pallas_call(
        functools.partial(_moba_qkv_body, nc=512),
        out_shape=jax.ShapeDtypeStruct((b, 3 * d, s), BF16),
        grid=(b, s // ts),
        in_specs=[pl.BlockSpec((1, ts, d), lambda i, j: (i, j, 0)),
                  _const_spec((1, d)),
                  _const_spec((3 * d, d))],
        out_specs=pl.BlockSpec((1, 3 * d, ts), lambda i, j: (i, 0, j)),
        compiler_params=_params(2),
        name="moba_qkv",
    )(x, _row(g), w_qkv.T.astype(BF16))

    hg = MOBA_HEAD_GROUP
    ng = MOBA_HEADS // hg
    gd = hg * dh
    ot = pl.pallas_call(
        functools.partial(_moba_attn_body, nb=nb),
        out_shape=jax.ShapeDtypeStruct((b, d, s), BF16),
        grid=(b, ng, nb),
        in_specs=[pl.BlockSpec((1, gd, MOBA_BLOCK), lambda i, j, k: (i, j, k)),
                  pl.BlockSpec((1, gd, s), lambda i, j, k: (i, ng + j, 0)),
                  pl.BlockSpec((1, gd, s), lambda i, j, k: (i, 2 * ng + j, 0))],
        out_specs=pl.BlockSpec((1, gd, MOBA_BLOCK), lambda i, j, k: (i, j, k)),
        scratch_shapes=[pltpu.VMEM((hg, nb, MOBA_BLOCK, dh), BF16),
                        pltpu.VMEM((hg, nb, dh, MOBA_BLOCK), BF16),
                        pltpu.VMEM((hg, nb, dh), F32),
                        pltpu.VMEM((hg, nb, MOBA_BLOCK), F32)],
        compiler_params=_params(3),
        name="moba_attn",
    )(qkvt, qkvt, qkvt)

    return pl.pallas_call(
        _moba_out_body,
        out_shape=jax.ShapeDtypeStruct((b, s, d), F32),
        grid=(b, s // ts),
        in_specs=[pl.BlockSpec((1, ts, d), lambda i, j: (i, j, 0)),
                  pl.BlockSpec((1, d, ts), lambda i, j: (i, 0, j)),
                  _const_spec((d, d))],
        out_specs=pl.BlockSpec((1, ts, d), lambda i, j: (i, j, 0)),
        compiler_params=_params(2),
        name="moba_out",
    )(x, ot, w_o.astype(BF16))


def _rwkv_body(x_ref, g_ref, mu_ref, wr_ref, wk_ref, wv_ref, w0_ref, w1_ref, w2_ref, a0_ref, a1_ref,
               a2_ref, g1_ref, g2_ref, kk_ref, ka_ref, rk_ref, gnw_ref, gnb_ref, wo_ref, p_ref, pt_ref,
               o_ref,
               hbuf_ref, st_ref, r_s, lw_s, k_s, v_s, a_s, b_s, y_s, gate_s, bonus_s, ch_s,
               *, tiles_per_seq):
    tm = x_ref.shape[0]
    lc = RWKV_CHUNK
    hd = RWKV_HEAD_DIM
    hs = SUBLANES

    @pl.when(pl.program_id(0) % tiles_per_seq == 0)
    def _():
        hbuf_ref[tm:tm + hs, :] = jnp.zeros((hs, D_MODEL), F32)
        st_ref[...] = jnp.zeros_like(st_ref)

    def seg_sum(v):
        hi, lo = _split2(v)
        return (jnp.dot(hi, p_ref[...], preferred_element_type=F32)
                + jnp.dot(lo, p_ref[...], preferred_element_type=F32))

    def seg_bcast(v):
        hi, lo = _split2(v)
        return (jnp.dot(hi, pt_ref[...], preferred_element_type=F32)
                + jnp.dot(lo, pt_ref[...], preferred_element_type=F32))

    x = x_ref[...]
    hn = _rms(x, g_ref[...])
    hbuf_ref[0:hs, :] = hbuf_ref[tm:tm + hs, :]
    hbuf_ref[hs:hs + tm, :] = hn
    xx = hbuf_ref[hs - 1:hs - 1 + tm, :] - hn

    def mix(i):
        return hn + xx * mu_ref[i:i + 1, :]

    r = _dot(mix(0), wr_ref[...])
    k = _dot(mix(1), wk_ref[...])
    v = _dot(mix(2), wv_ref[...])
    wpre = w0_ref[...] + _dot(jnp.tanh(_dot(mix(3), w1_ref[...])), w2_ref[...])
    lw = -jnp.exp(-jax.nn.softplus(-wpre) - 0.5)
    a = jax.nn.sigmoid(a0_ref[...] + _dot(_dot(mix(4), a1_ref[...]), a2_ref[...]))
    gate_s[...] = _dot(jax.nn.sigmoid(_dot(mix(5), g1_ref[...])), g2_ref[...])

    kk = k * kk_ref[...]
    inv = 1.0 / jnp.maximum(jnp.sqrt(seg_sum(kk * kk)), 1e-12)
    kk = kk * seg_bcast(inv)
    k = k * (1.0 + (a - 1.0) * ka_ref[...])
    bonus_s[...] = seg_bcast(seg_sum(r * k * rk_ref[...])) * v
    r_s[...] = r
    lw_s[...] = lw
    k_s[...] = k
    v_s[...] = v
    a_s[...] = -kk
    b_s[...] = kk * a

    ti = lax.broadcasted_iota(jnp.int32, (lc, lc), 0)
    si = lax.broadcasted_iota(jnp.int32, (lc, lc), 1)
    low_strict = si < ti
    tril = jnp.where(si <= ti, 1.0, 0.0).astype(BF16)
    ti2 = lax.broadcasted_iota(jnp.int32, (lc, 2 * lc), 0)
    si2 = lax.broadcasted_iota(jnp.int32, (lc, 2 * lc), 1)
    low_incl2 = jnp.where(si2 >= lc, si2 - lc, si2) <= ti2

    def chunk(c, carry):
        rows = pl.ds(pl.multiple_of(c * lc, lc), lc)
        lw_c = lw_s[rows, :]
        lg = _dot_exact_lhs(tril, lw_c)
        lg_last = lg[lc - 1:lc, :]
        g_inv = jnp.exp(-lg)
        g_tail = jnp.exp(lg_last - lg)
        g_tot = jnp.exp(lg_last)
        ch_s[0, 0:lc, :] = a_s[rows, :] * jnp.exp(lg - lw_c)
        ch_s[0, lc:2 * lc, :] = r_s[rows, :] * jnp.exp(lg)
        ch_s[1, 0:lc, :] = b_s[rows, :] * g_inv
        ch_s[1, lc:2 * lc, :] = k_s[rows, :] * g_inv
        ch_s[2, 0:lc, :] = b_s[rows, :] * g_tail
        ch_s[2, lc:2 * lc, :] = k_s[rows, :] * g_tail
        for h0 in range(0, RWKV_HEADS, RWKV_HEAD_GROUP):
            hh = range(RWKV_HEAD_GROUP)
            sls = [slice((h0 + i) * hd, (h0 + i + 1) * hd) for i in hh]
            ar = [ch_s[0, :, sl].astype(BF16) for sl in sls]
            gm = [_dot_nt(ar[i], ch_s[1, :, sls[i]]) for i in hh]
            sh = [st_ref[h0 + i] for i in hh]
            ars = [_dot_nt(ar[i], sh[i]) for i in hh]
            vh = [v_s[rows, sl].astype(BF16) for sl in sls]
            npow = [jnp.where(low_strict, gm[i][0:lc, 0:lc], 0.0).astype(BF16) for i in hh]
            u = [ars[i][0:lc, :] + _dot(jnp.where(low_strict, gm[i][0:lc, lc:2 * lc], 0.0), vh[i]) for i in hh]
            u = [u[i] + _dot(npow[i], u[i]) for i in hh]
            steps = 1
            while 2 * steps < lc:
                npow = [_dot(npow[i], npow[i]).astype(BF16) for i in hh]
                u = [u[i] + _dot(npow[i], u[i]) for i in hh]
                steps *= 2
            uv = [jnp.concatenate([u[i].astype(BF16), vh[i]], axis=0) for i in hh]
            for i in hh:
                mr = jnp.where(low_incl2, gm[i][lc:2 * lc, :], 0.0)
                y_s[rows, sls[i]] = ars[i][lc:2 * lc, :] + _dot(mr, uv[i])
            for i in hh:
                st_ref[h0 + i] = sh[i] * g_tot[:, sls[i]] + _dot_tn(uv[i], ch_s[2, :, sls[i]])
        return carry

    lax.fori_loop(0, tm // lc, chunk, 0)

    y = y_s[...]
    mean = seg_bcast(seg_sum(y)) * (1.0 / hd)
    dy = y - mean
    var = seg_bcast(seg_sum(dy * dy)) * (1.0 / hd)
    yn = dy * lax.rsqrt(var + RWKV_GN_EPS) * gnw_ref[...] + gnb_ref[...]
    o_ref[...] = x + _dot((yn + bonus_s[...]) * gate_s[...], wo_ref[...])


def _rwkv7(x2, seq, g, mu, w_rkv, w0, w1, w2, a0, a1, a2, g1, g2, k_k, k_a, r_k, gn_w, gn_b, w_o, tm=256):
    rows = x2.shape[0]
    d = D_MODEL
    head_of = jnp.arange(d) // RWKV_HEAD_DIM
    p = (head_of[:, None] == jnp.arange(128)[None, :]).astype(BF16)
    args = (x2, _row(g), mu.astype(F32), w_rkv[0].astype(BF16), w_rkv[1].astype(BF16), w_rkv[2].astype(BF16),
            _row(w0), w1.astype(BF16), w2.astype(BF16), _row(a0), a1.astype(BF16), a2.astype(BF16),
            g1.astype(BF16), g2.astype(BF16), _row(k_k), _row(k_a), _row(r_k), _row(gn_w), _row(gn_b),
            w_o.astype(BF16), p, p.T)
    in_specs = [pl.BlockSpec((tm, d), lambda i: (i, 0))] + [_const_spec(a.shape) for a in args[1:]]
    big = pltpu.VMEM((tm, d), F32)
    return pl.pallas_call(
        functools.partial(_rwkv_body, tiles_per_seq=seq // tm),
        out_shape=jax.ShapeDtypeStruct((rows, d), F32),
        grid=(rows // tm,),
        in_specs=in_specs,
        out_specs=pl.BlockSpec((tm, d), lambda i: (i, 0)),
        scratch_shapes=[pltpu.VMEM((tm + 2 * SUBLANES, d), F32),
                        pltpu.VMEM((RWKV_HEADS, RWKV_HEAD_DIM, RWKV_HEAD_DIM), F32),
                        big, big, big, big, big, big, big, big, big,
                        pltpu.VMEM((3, 2 * RWKV_CHUNK, d), F32)],
        compiler_params=_params(1),
        name="rwkv7",
    )(*args)


def _ssd_in_body(x_ref, g_ref, w_ref, wdt_ref, zx_ref, dt_ref, *, nc):
    hn = _rms(x_ref[...], g_ref[...])
    h = hn.astype(BF16)
    for n in range(w_ref.shape[1] // nc):
        zx_ref[:, n * nc:(n + 1) * nc] = jnp.dot(h, w_ref[:, n * nc:(n + 1) * nc],
                                                 preferred_element_type=F32)
    dt_ref[...] = jnp.dot(hn, wdt_ref[...], preferred_element_type=F32, precision=lax.Precision.HIGHEST)


def _ssd_core_body(zx_ref, dt_ref, cw_ref, cb_ref, dtb_ref, alog_ref, dskip_ref, nw_ref, e_ref, o_ref,
                   cbuf_ref, state_ref, y_s):
    lc = SSD_CHUNK
    di = SSD_D_INNER
    hs = SUBLANES
    gw = di // SSD_GROUPS
    ns = SSD_STATE

    @pl.when(pl.program_id(1) == 0)
    def _():
        cbuf_ref[lc:lc + hs, :] = jnp.zeros((hs, SSD_CONV_DIM), F32)
        state_ref[...] = jnp.zeros_like(state_ref)

    z = zx_ref[:, 0:di]
    xbc_raw = zx_ref[:, di:di + SSD_CONV_DIM]
    cbuf_ref[0:hs, :] = cbuf_ref[lc:lc + hs, :]
    cbuf_ref[hs:hs + lc, :] = xbc_raw
    xbc = cb_ref[...] + cw_ref[SSD_CONV - 1:SSD_CONV, :] * xbc_raw
    for j in range(SSD_CONV - 1):
        o = hs - (SSD_CONV - 1) + j
        xbc = xbc + cw_ref[j:j + 1, :] * cbuf_ref[o:o + lc, :]
    xbc = jax.nn.silu(xbc)
    xs = xbc[:, 0:di]
    bm = xbc[:, di:di + SSD_GROUPS * ns]
    cm = xbc[:, di + SSD_GROUPS * ns:]

    dt = jax.nn.softplus(dt_ref[...] + dtb_ref[...])
    adt = dt * (-jnp.exp(alog_ref[...]))
    li = lax.broadcasted_iota(jnp.int32, (lc, lc), 0)
    si = lax.broadcasted_iota(jnp.int32, (lc, lc), 1)
    causal = si <= li
    tril = jnp.where(causal, 1.0, 0.0).astype(BF16)
    acs = _dot_exact_lhs(tril, adt)
    acs_t = acs.T
    a_last = acs[lc - 1:lc, :]

    def expand(v):
        hi, lo = _split2(v)
        return (jnp.dot(hi, e_ref[...], preferred_element_type=F32)
                + jnp.dot(lo, e_ref[...], preferred_element_type=F32))

    exp_acs_x = expand(jnp.exp(acs))
    xd = xs * expand(dt)
    xdec = xd * expand(jnp.exp(a_last - acs))

    for g in range(SSD_GROUPS):
        bg = bm[:, g * ns:(g + 1) * ns]
        cg = cm[:, g * ns:(g + 1) * ns]
        cb = _dot_nt(cg, bg)
        gsl = slice(g * gw, (g + 1) * gw)
        st_g = state_ref[:, gsl]
        y_off = _dot(cg, st_g) * exp_acs_x[:, gsl]
        for e in range(SSD_HEADS // SSD_GROUPS):
            h = g * (SSD_HEADS // SSD_GROUPS) + e
            col = jnp.broadcast_to(acs[:, h:h + 1], (lc, lc))
            row = jnp.broadcast_to(acs_t[h:h + 1, :], (lc, lc))
            lm = jnp.where(causal, jnp.exp(jnp.minimum(col - row, 0.0)), 0.0)
            hsl = slice(h * SSD_HEAD_DIM, (h + 1) * SSD_HEAD_DIM)
            y_s[:, hsl] = _dot(cb * lm, xd[:, hsl])
        state_ref[:, gsl] = st_g * exp_acs_x[lc - 1:lc, gsl] + _dot_tn(bg, xdec[:, gsl])
        y_s[:, gsl] = y_s[:, gsl] + y_off

    y = y_s[...] + xs * dskip_ref[...]
    yz = y * jax.nn.silu(z)
    for g in range(SSD_GROUPS):
        gsl = slice(g * gw, (g + 1) * gw)
        yg = yz[:, gsl]
        o_ref[:, gsl] = (yg * lax.rsqrt(jnp.mean(yg * yg, axis=-1, keepdims=True) + SSD_NORM_EPS)
                         * nw_ref[:, gsl])


def _mamba2(x2, batch, seq, g, w_in, conv_w, conv_b, dt_bias, a_log, d_skip, norm_w, w_out, tm=512):
    rows = x2.shape[0]
    d = D_MODEL
    di = SSD_D_INNER
    nzx = di + SSD_CONV_DIM
    zx, dt = pl.pallas_call(
        functools.partial(_ssd_in_body, nc=512),
        out_shape=(jax.ShapeDtypeStruct((rows, nzx), F32), jax.ShapeDtypeStruct((rows, SSD_HEADS), F32)),
        grid=(rows // tm,),
        in_specs=[pl.BlockSpec((tm, d), lambda i: (i, 0)),
                  _const_spec((1, d)),
                  _const_spec((d, nzx)),
                  _const_spec((d, SSD_HEADS))],
        out_specs=(pl.BlockSpec((tm, nzx), lambda i: (i, 0)),
                   pl.BlockSpec((tm, SSD_HEADS), lambda i: (i, 0))),
        compiler_params=_params(1),
        name="ssd_in",
    )(x2, _row(g), w_in[:, :nzx].astype(BF16), w_in[:, nzx:].astype(F32))

    lc = SSD_CHUNK
    nchunk = seq // lc
    expand = (jnp.arange(SSD_HEADS)[:, None] == (jnp.arange(di) // SSD_HEAD_DIM)[None, :]).astype(BF16)
    d_x = jnp.repeat(d_skip.astype(F32), SSD_HEAD_DIM).reshape(1, di)
    args = (zx, dt, conv_w.astype(F32), _row(conv_b), _row(dt_bias), _row(a_log), d_x, _row(norm_w), expand)
    yz = pl.pallas_call(
        _ssd_core_body,
        out_shape=jax.ShapeDtypeStruct((rows, di), F32),
        grid=(batch, nchunk),
        in_specs=[pl.BlockSpec((lc, nzx), lambda b, c: (b * nchunk + c, 0)),
                  pl.BlockSpec((lc, SSD_HEADS), lambda b, c: (b * nchunk + c, 0))]
                 + [_const_spec(a.shape) for a in args[2:]],
        out_specs=pl.BlockSpec((lc, di), lambda b, c: (b * nchunk + c, 0)),
        scratch_shapes=[pltpu.VMEM((lc + 2 * SUBLANES, SSD_CONV_DIM), F32),
                        pltpu.VMEM((SSD_STATE, di), F32),
                        pltpu.VMEM((lc, di), F32)],
        compiler_params=_params(2),
        name="ssd_core",
    )(*args)
    return _proj_res(x2, yz, w_out)


def _conf_body(x_ref, g_ref, w1_ref, b1_ref, dw_ref, dwb_ref, lnw_ref, lnb_ref, w2_ref, b2_ref, o_ref,
               ubuf_ref, c_s, *, tiles_per_seq, rb, lanes):
    tm = x_ref.shape[0]
    d = D_MODEL
    hl = CONF_HALO

    @pl.when(pl.program_id(0) % tiles_per_seq == 0)
    def _():
        ubuf_ref[tm:tm + hl, :] = jnp.zeros((hl, d), F32)

    x = x_ref[...]
    h = _rms(x, g_ref[...]).astype(BF16)
    p = jnp.dot(h, w1_ref[...], preferred_element_type=F32) + b1_ref[...]
    u = p[:, 0:d] * jax.nn.sigmoid(p[:, d:2 * d])
    ubuf_ref[0:hl, :] = ubuf_ref[tm:tm + hl, :]
    ubuf_ref[hl:hl + tm, :] = u
    base = hl - (CONF_KERNEL - 1)
    for r0 in range(0, tm, rb):
        for c0 in range(0, d, lanes):
            acc = jnp.broadcast_to(dwb_ref[:, c0:c0 + lanes], (rb, lanes))
            for j in range(CONF_KERNEL):
                acc = acc + dw_ref[j:j + 1, c0:c0 + lanes] * ubuf_ref[r0 + base + j:r0 + base + j + rb, c0:c0 + lanes]
            c_s[r0:r0 + rb, c0:c0 + lanes] = acc
    y = c_s[...]
    mu = jnp.mean(y, axis=-1, keepdims=True)
    dy = y - mu
    var = jnp.mean(dy * dy, axis=-1, keepdims=True)
    zn = dy * lax.rsqrt(var + CONF_LN_EPS) * lnw_ref[...] + lnb_ref[...]
    o_ref[...] = x + _dot(jax.nn.silu(zn), w2_ref[...]) + b2_ref[...]


def _conformer(x2, seq, g, w_pw1, b_pw1, dw_w, dw_b, ln_w, ln_b, w_pw2, b_pw2, tm=256):
    rows = x2.shape[0]
    d = D_MODEL
    args = (x2, _row(g), w_pw1.astype(BF16), _row(b_pw1), dw_w.astype(F32), _row(dw_b), _row(ln_w),
            _row(ln_b), w_pw2.astype(BF16), _row(b_pw2))
    return pl.pallas_call(
        functools.partial(_conf_body, tiles_per_seq=seq // tm, rb=64, lanes=256),
        out_shape=jax.ShapeDtypeStruct((rows, d), F32),
        grid=(rows // tm,),
        in_specs=[pl.BlockSpec((tm, d), lambda i: (i, 0))] + [_const_spec(a.shape) for a in args[1:]],
        out_specs=pl.BlockSpec((tm, d), lambda i: (i, 0)),
        scratch_shapes=[pltpu.VMEM((tm + CONF_HALO, d), F32), pltpu.VMEM((tm, d), F32)],
        compiler_params=_params(1),
        name="conformer",
    )(*args)


def kernel(x, norm_mix, norm_ffn, norm_final, ffn_w_up, ffn_conv_w, ffn_conv_b, ffn_w_down, moba_w_qkv, moba_w_o, rwkv_mu, rwkv_w_rkv, rwkv_w0, rwkv_w1, rwkv_w2, rwkv_a0, rwkv_a1, rwkv_a2, rwkv_g1, rwkv_g2, rwkv_k_k, rwkv_k_a, rwkv_r_k, rwkv_gn_w, rwkv_gn_b, rwkv_w_o, ssd_w_in, ssd_conv_w, ssd_conv_b, ssd_dt_bias, ssd_a_log, ssd_d, ssd_norm_w, ssd_w_out, conf_w_pw1, conf_b_pw1, conf_dw_w, conf_dw_b, conf_ln_w, conf_ln_b, conf_w_pw2, conf_b_pw2):
    b, s, d = x.shape
    depth = norm_mix.shape[0]
    rows = b * s
    for i in range(depth):
        kind, j = i % 4, i // 4
        if kind == 0:
            x2 = _moba(x.reshape(b, s, d), norm_mix[i], moba_w_qkv[j], moba_w_o[j]).reshape(rows, d)
        elif kind == 1:
            x2 = _rwkv7(x.reshape(rows, d), s, norm_mix[i], rwkv_mu[j], rwkv_w_rkv[j], rwkv_w0[j], rwkv_w1[j],
                        rwkv_w2[j], rwkv_a0[j], rwkv_a1[j], rwkv_a2[j], rwkv_g1[j], rwkv_g2[j], rwkv_k_k[j],
                        rwkv_k_a[j], rwkv_r_k[j], rwkv_gn_w[j], rwkv_gn_b[j], rwkv_w_o[j])
        elif kind == 2:
            x2 = _mamba2(x.reshape(rows, d), b, s, norm_mix[i], ssd_w_in[j], ssd_conv_w[j], ssd_conv_b[j],
                         ssd_dt_bias[j], ssd_a_log[j], ssd_d[j], ssd_norm_w[j], ssd_w_out[j])
        else:
            x2 = _conformer(x.reshape(rows, d), s, norm_mix[i], conf_w_pw1[j], conf_b_pw1[j], conf_dw_w[j],
                            conf_dw_b[j], conf_ln_w[j], conf_ln_b[j], conf_w_pw2[j], conf_b_pw2[j])
        x = _conv_ffn(x2, s, norm_ffn[i], ffn_w_up[i], ffn_conv_w[i], ffn_conv_b[i], ffn_w_down[i],
                      norm_final, final=(i == depth - 1))
    return x.reshape(b, s, d)
```

```python
import functools

import jax
import jax.numpy as jnp
from jax import lax
from jax.experimental import pallas as pl
from jax.experimental.pallas import tpu as pltpu

F32 = jnp.float32
BF16 = jnp.bfloat16

D_MODEL = 1024
NORM_EPS = 1e-6
NEG_INF = -1e30
LOG2E = 1.4426950408889634

MOBA_HEADS = 16
MOBA_HEAD_DIM = 64
MOBA_BLOCK = 256
MOBA_TOPK = 3
MOBA_HEAD_GROUP = 8

RWKV_HEADS = 16
RWKV_HEAD_DIM = 64
RWKV_GN_EPS = 64e-5
RWKV_CHUNK = 64
RWKV_HEAD_GROUP = 16

SSD_D_INNER = 2048
SSD_HEAD_DIM = 64
SSD_HEADS = 32
SSD_GROUPS = 4
SSD_STATE = 128
SSD_CONV = 4
SSD_CHUNK = 128
SSD_CONV_DIM = SSD_D_INNER + 2 * SSD_GROUPS * SSD_STATE
SSD_NORM_EPS = 1e-5

CONF_KERNEL = 31
CONF_LN_EPS = 1e-5
CONF_HALO = 32

D_FF = 2816
FFN_CONV = 3
FFN_CHUNK = 256

SUBLANES = 8
VMEM_LIMIT = 56 * 1024 * 1024


def _rms(x, g, eps=NORM_EPS):
    return x * lax.rsqrt(jnp.mean(x * x, axis=-1, keepdims=True) + eps) * g


def _dot(a, b):
    return jnp.dot(a.astype(BF16), b.astype(BF16), preferred_element_type=F32)


def _dot_nt(a, b):
    return lax.dot_general(a.astype(BF16), b.astype(BF16), (((1,), (1,)), ((), ())),
                           preferred_element_type=F32)


def _dot_tn(a, b):
    return lax.dot_general(a.astype(BF16), b.astype(BF16), (((0,), (0,)), ((), ())),
                           preferred_element_type=F32)


def _split2(a):
    hi = a.astype(BF16)
    lo = (a - hi.astype(F32)).astype(BF16)
    return hi, lo


def _split3(a):
    hi = a.astype(BF16)
    r = a - hi.astype(F32)
    mid = r.astype(BF16)
    lo = (r - mid.astype(F32)).astype(BF16)
    return hi, mid, lo


def _dot_exact_rhs(a, b_bf16):
    hi, mid, lo = _split3(a)
    return (jnp.dot(hi, b_bf16, preferred_element_type=F32)
            + jnp.dot(mid, b_bf16, preferred_element_type=F32)
            + jnp.dot(lo, b_bf16, preferred_element_type=F32))


def _dot_exact_lhs(a_bf16, b):
    hi, mid, lo = _split3(b)
    return (jnp.dot(a_bf16, hi, preferred_element_type=F32)
            + jnp.dot(a_bf16, mid, preferred_element_type=F32)
            + jnp.dot(a_bf16, lo, preferred_element_type=F32))


def _const_spec(shape):
    n = len(shape)
    return pl.BlockSpec(shape, lambda *_: (0,) * n, pipeline_mode=pl.Buffered(1))


def _params(n_axes):
    return pltpu.CompilerParams(dimension_semantics=("arbitrary",) * n_axes,
                                vmem_limit_bytes=VMEM_LIMIT)


def _row(v):
    return v.reshape(1, -1).astype(F32)


def _ffn_body(x_ref, g_ref, wup_ref, cw_ref, cb_ref, wdn_ref, gf_ref, o_ref, halo_ref,
              *, tiles_per_seq, final):
    tm = x_ref.shape[0]
    fc = FFN_CHUNK
    hs = SUBLANES

    @pl.when(pl.program_id(0) % tiles_per_seq == 0)
    def _():
        halo_ref[...] = jnp.zeros_like(halo_ref)

    x = x_ref[...]
    h = _rms(x, g_ref[...]).astype(BF16)

    r8 = lax.broadcasted_iota(jnp.int32, (hs, fc), 0)

    def conv(u, off):
        prev = halo_ref[:, off:off + fc]
        halo_ref[:, off:off + fc] = u[tm - hs:tm, :]
        w0 = cw_ref[0:1, off:off + fc]
        w1 = cw_ref[1:2, off:off + fc]
        u1 = pltpu.roll(u, 1, 0)
        u2 = pltpu.roll(u, 2, 0)
        y = w0 * u2 + w1 * u1 + cw_ref[2:3, off:off + fc] * u + cb_ref[0:1, off:off + fc]
        fix = (w0 * jnp.where(r8 < 2, pltpu.roll(prev, 2, 0) - u2[0:hs, :], 0.0)
               + w1 * jnp.where(r8 < 1, pltpu.roll(prev, 1, 0) - u1[0:hs, :], 0.0))
        return jnp.concatenate([y[0:hs, :] + fix, y[hs:, :]], axis=0)

    def up(c):
        og, ou = c * fc, D_FF + c * fc
        return (jnp.dot(h, wup_ref[:, og:og + fc], preferred_element_type=F32),
                jnp.dot(h, wup_ref[:, ou:ou + fc], preferred_element_type=F32))

    acc = jnp.zeros((tm, D_MODEL), F32)
    n_chunks = D_FF // fc
    nxt = up(0)
    for c in range(n_chunks):
        ug, uu = nxt
        if c + 1 < n_chunks:
            nxt = up(c + 1)
        yg = conv(ug, c * fc)
        yu = conv(uu, D_FF + c * fc)
        a = (jax.nn.silu(yg) * yu).astype(BF16)
        acc = acc + jnp.dot(a, wdn_ref[c * fc:(c + 1) * fc, :], preferred_element_type=F32)
    y = x + acc
    if final:
        y = _rms(y, gf_ref[...])
    o_ref[...] = y


def _conv_ffn(x2, seq, g, w_up, conv_w, conv_b, w_down, g_final, final, tm=512):
    rows = x2.shape[0]
    body = functools.partial(_ffn_body, tiles_per_seq=seq // tm, final=final)
    return pl.pallas_call(
        body,
        out_shape=jax.ShapeDtypeStruct((rows, D_MODEL), F32),
        grid=(rows // tm,),
        in_specs=[
            pl.BlockSpec((tm, D_MODEL), lambda i: (i, 0)),
            _const_spec((1, D_MODEL)),
            _const_spec((D_MODEL, 2 * D_FF)),
            _const_spec((FFN_CONV, 2 * D_FF)),
            _const_spec((1, 2 * D_FF)),
            _const_spec((D_FF, D_MODEL)),
            _const_spec((1, D_MODEL)),
        ],
        out_specs=pl.BlockSpec((tm, D_MODEL), lambda i: (i, 0)),
        scratch_shapes=[pltpu.VMEM((SUBLANES, 2 * D_FF), F32)],
        compiler_params=_params(1),
        name="conv_ffn",
    )(x2, _row(g), w_up.astype(BF16), conv_w.astype(F32), _row(conv_b), w_down.astype(BF16), _row(g_final))


def _proj_res_body(x_ref, y_ref, w_ref, o_ref):
    o_ref[...] = x_ref[...] + jnp.dot(y_ref[...].astype(BF16), w_ref[...], preferred_element_type=F32)


def _proj_res(x2, y2, w, tm=512):
    rows, k = y2.shape
    return pl.pallas_call(
        _proj_res_body,
        out_shape=jax.ShapeDtypeStruct((rows, D_MODEL), F32),
        grid=(rows // tm,),
        in_specs=[pl.BlockSpec((tm, D_MODEL), lambda i: (i, 0)),
                  pl.BlockSpec((tm, k), lambda i: (i, 0)),
                  _const_spec((k, D_MODEL))],
        out_specs=pl.BlockSpec((tm, D_MODEL), lambda i: (i, 0)),
        compiler_params=_params(1),
        name="proj_res",
    )(x2, y2, w.astype(BF16))


def _moba_qkv_body(x_ref, g_ref, wt_ref, o_ref, *, nc):
    h = _rms(x_ref[0], g_ref[...]).astype(BF16)
    for n in range(3 * D_MODEL // nc):
        o_ref[0, n * nc:(n + 1) * nc, :] = lax.dot_general(
            wt_ref[n * nc:(n + 1) * nc, :], h, (((1,), (1,)), ((), ())),
            preferred_element_type=F32).astype(BF16)


def _moba_attn_body(q_ref, k_ref, v_ref, o_ref, kaug_ref, vblk_ref, kmean_ref, *, nb):
    bs = MOBA_BLOCK
    dh = MOBA_HEAD_DIM
    hg = MOBA_HEAD_GROUP
    hh = range(hg)
    iq = pl.program_id(2)

    @pl.when(iq == 0)
    def _():
        blk = lax.broadcasted_iota(jnp.int32, (bs, nb), 1)
        for j in range(nb):
            kr = k_ref[0, :, j * bs:(j + 1) * bs].astype(F32).T
            onehot = jnp.where(blk == j, 1.0, 0.0).astype(BF16)
            for i in hh:
                kri = kr[:, i * dh:(i + 1) * dh]
                kaug_ref[i, j] = jnp.concatenate([kri.astype(BF16), onehot], axis=1)
                kmean_ref[i, j:j + 1, :] = jnp.mean(kri, axis=0, keepdims=True)
                vblk_ref[i, j] = v_ref[0, i * dh:(i + 1) * dh, j * bs:(j + 1) * bs]

    jidx = lax.broadcasted_iota(jnp.int32, (nb, bs), 0)
    qs, qaug = [], []
    for i in hh:
        q_raw = q_ref[0, i * dh:(i + 1) * dh, :]
        km_hi, km_lo = _split2(kmean_ref[i])
        gate = (jnp.dot(km_hi, q_raw, preferred_element_type=F32)
                + jnp.dot(km_lo, q_raw, preferred_element_type=F32))
        cnt = jnp.zeros((nb, bs), jnp.int32)
        for jp in range(nb):
            gj = gate[jp:jp + 1, :]
            beats = (gj > gate) | ((gj == gate) & (jp < jidx))
            cnt = cnt + jnp.where(beats & (jp < iq), 1, 0)
        selm = (cnt < MOBA_TOPK) & (jidx < iq)
        qs.append(q_raw * (dh ** -0.5 * LOG2E))
        qaug.append(jnp.concatenate([qs[i], jnp.where(selm, 0.0, NEG_INF).astype(BF16)], axis=0))

    kpos = lax.broadcasted_iota(jnp.int32, (bs, bs), 0)
    qpos = lax.broadcasted_iota(jnp.int32, (bs, bs), 1)
    causal = kpos <= qpos
    s = [jnp.where(causal, jnp.dot(kaug_ref[i, iq][:, 0:dh], qs[i], preferred_element_type=F32), NEG_INF)
         for i in hh]
    m0 = [jnp.max(s[i], axis=0, keepdims=True) for i in hh]
    p = [jnp.exp2(s[i] - m0[i]) for i in hh]
    l0 = [jnp.sum(p[i], axis=0, keepdims=True) for i in hh]
    acc0 = [jnp.dot(vblk_ref[i, iq], p[i].astype(BF16), preferred_element_type=F32) for i in hh]

    def step(j, carry):
        m, l, acc = carry
        sj = [jnp.dot(kaug_ref[i, j], qaug[i], preferred_element_type=F32) for i in hh]
        m_new = [jnp.maximum(m[i], jnp.max(sj[i], axis=0, keepdims=True)) for i in hh]
        alpha = [jnp.exp2(m[i] - m_new[i]) for i in hh]
        pj = [jnp.exp2(sj[i] - m_new[i]) for i in hh]
        l = [alpha[i] * l[i] + jnp.sum(pj[i], axis=0, keepdims=True) for i in hh]
        acc = [alpha[i] * acc[i] + jnp.dot(vblk_ref[i, j], pj[i].astype(BF16), preferred_element_type=F32)
               for i in hh]
        return m_new, l, acc

    _, l, acc = lax.fori_loop(0, iq, step, (m0, l0, acc0))
    for i in hh:
        o_ref[0, i * dh:(i + 1) * dh, :] = (acc[i] / l[i]).astype(BF16)


def _moba_out_body(x_ref, o_ref_in, w_ref, o_ref):
    o_ref[0] = x_ref[0] + lax.dot_general(o_ref_in[0], w_ref[...], (((0,), (0,)), ((), ())),
                                          preferred_element_type=F32)


def _moba(x, g, w_qkv, w_o, ts=512):
    b, s, d = x.shape
    nb = s // MOBA_BLOCK
    dh = MOBA_HEAD_DIM
    qkvt = pl.pallas_call(
        functools.partial(_moba_qkv_body, nc=512),
        out_shape=jax.ShapeDtypeStruct((b, 3 * d, s), BF16),
        grid=(b, s // ts),
        in_specs=[pl.BlockSpec((1, ts, d), lambda i, j: (i, j, 0)),
                  _const_spec((1, d)),
                  _const_spec((3 * d, d))],
        out_specs=pl.BlockSpec((1, 3 * d, ts), lambda i, j: (i, 0, j)),
        compiler_params=_params(2),
        name="moba_qkv",
    )(x, _row(g), w_qkv.T.astype(BF16))

    hg = MOBA_HEAD_GROUP
    ng = MOBA_HEADS // hg
    gd = hg * dh
    ot = pl.pallas_call(
        functools.partial(_moba_attn_body, nb=nb),
        out_shape=jax.ShapeDtypeStruct((b, d, s), BF16),
        grid=(b, ng, nb),
        in_specs=[pl.BlockSpec((1, gd, MOBA_BLOCK), lambda i, j, k: (i, j, k)),
                  pl.BlockSpec((1, gd, s), lambda i, j, k: (i, ng + j, 0)),
                  pl.BlockSpec((1, gd, s), lambda i, j, k: (i, 2 * ng + j, 0))],
        out_specs=pl.BlockSpec((1, gd, MOBA_BLOCK), lambda i, j, k: (i, j, k)),
        scratch_shapes=[pltpu.VMEM((hg, nb, MOBA_BLOCK, dh + nb), BF16),
                        pltpu.VMEM((hg, nb, dh, MOBA_BLOCK), BF16),
                        pltpu.VMEM((hg, nb, dh), F32)],
        compiler_params=_params(3),
        name="moba_attn",
    )(qkvt, qkvt, qkvt)

    return pl.pallas_call(
        _moba_out_body,
        out_shape=jax.ShapeDtypeStruct((b, s, d), F32),
        grid=(b, s // ts),
        in_specs=[pl.BlockSpec((1, ts, d), lambda i, j: (i, j, 0)),
                  pl.BlockSpec((1, d, ts), lambda i, j: (i, 0, j)),
                  _const_spec((d, d))],
        out_specs=pl.BlockSpec((1, ts, d), lambda i, j: (i, j, 0)),
        compiler_params=_params(2),
        name="moba_out",
    )(x, ot, w_o.astype(BF16))


def _rwkv_body(x_ref, g_ref, mu_ref, wr_ref, wk_ref, wv_ref, w0_ref, w1_ref, w2_ref, a0_ref, a1_ref,
               a2_ref, g1_ref, g2_ref, kk_ref, ka_ref, rk_ref, gnw_ref, gnb_ref, wo_ref, p_ref, pt_ref,
               o_ref,
               hbuf_ref, st_ref, r_s, lw_s, k_s, v_s, a_s, b_s, y_s, gate_s, bonus_s, ch_s,
               *, tiles_per_seq):
    tm = x_ref.shape[0]
    lc = RWKV_CHUNK
    hd = RWKV_HEAD_DIM
    hs = SUBLANES

    @pl.when(pl.program_id(0) % tiles_per_seq == 0)
    def _():
        hbuf_ref[tm:tm + hs, :] = jnp.zeros((hs, D_MODEL), F32)
        st_ref[...] = jnp.zeros_like(st_ref)

    def seg_sum(v):
        hi, lo = _split2(v)
        return (jnp.dot(hi, p_ref[...], preferred_element_type=F32)
                + jnp.dot(lo, p_ref[...], preferred_element_type=F32))

    def seg_bcast(v):
        hi, lo = _split2(v)
        return (jnp.dot(hi, pt_ref[...], preferred_element_type=F32)
                + jnp.dot(lo, pt_ref[...], preferred_element_type=F32))

    x = x_ref[...]
    hn = _rms(x, g_ref[...])
    hbuf_ref[0:hs, :] = hbuf_ref[tm:tm + hs, :]
    hbuf_ref[hs:hs + tm, :] = hn
    xx = hbuf_ref[hs - 1:hs - 1 + tm, :] - hn

    def mix(i):
        return hn + xx * mu_ref[i:i + 1, :]

    r = _dot(mix(0), wr_ref[...])
    k = _dot(mix(1), wk_ref[...])
    v = _dot(mix(2), wv_ref[...])
    wpre = w0_ref[...] + _dot(jnp.tanh(_dot(mix(3), w1_ref[...])), w2_ref[...])
    lw = -jnp.exp(-jax.nn.softplus(-wpre) - 0.5)
    a = jax.nn.sigmoid(a0_ref[...] + _dot(_dot(mix(4), a1_ref[...]), a2_ref[...]))
    gate_s[...] = _dot(jax.nn.sigmoid(_dot(mix(5), g1_ref[...])), g2_ref[...])

    kk = k * kk_ref[...]
    inv = 1.0 / jnp.maximum(jnp.sqrt(seg_sum(kk * kk)), 1e-12)
    kk = kk * seg_bcast(inv)
    k = k * (1.0 + (a - 1.0) * ka_ref[...])
    bonus_s[...] = seg_bcast(seg_sum(r * k * rk_ref[...])) * v
    r_s[...] = r
    lw_s[...] = lw
    k_s[...] = k
    v_s[...] = v
    a_s[...] = -kk
    b_s[...] = kk * a

    ti = lax.broadcasted_iota(jnp.int32, (lc, lc), 0)
    si = lax.broadcasted_iota(jnp.int32, (lc, lc), 1)
    low_strict = si < ti
    tril = jnp.where(si <= ti, 1.0, 0.0).astype(BF16)
    ti2 = lax.broadcasted_iota(jnp.int32, (lc, 2 * lc), 0)
    si2 = lax.broadcasted_iota(jnp.int32, (lc, 2 * lc), 1)
    low_incl2 = jnp.where(si2 >= lc, si2 - lc, si2) <= ti2
    keep_u = si2 >= lc

    def chunk(c, carry):
        rows = pl.ds(pl.multiple_of(c * lc, lc), lc)
        lw_c = lw_s[rows, :]
        lg = _dot_exact_lhs(tril, lw_c)
        lg_last = lg[lc - 1:lc, :]
        g_inv = jnp.exp(-lg)
        g_tail = jnp.exp(lg_last - lg)
        g_tot = jnp.exp(lg_last)
        ch_s[0, 0:lc, :] = a_s[rows, :] * jnp.exp(lg - lw_c)
        ch_s[0, lc:2 * lc, :] = r_s[rows, :] * jnp.exp(lg)
        ch_s[1, 0:lc, :] = b_s[rows, :] * g_inv
        ch_s[1, lc:2 * lc, :] = k_s[rows, :] * g_inv
        ch_s[2, 0:lc, :] = b_s[rows, :] * g_tail
        ch_s[2, lc:2 * lc, :] = k_s[rows, :] * g_tail
        for h0 in range(0, RWKV_HEADS, RWKV_HEAD_GROUP):
            hh = range(RWKV_HEAD_GROUP)
            sls = [slice((h0 + i) * hd, (h0 + i + 1) * hd) for i in hh]
            sh = [st_ref[h0 + i] for i in hh]
            gm = [_dot_nt(ch_s[0, :, sls[i]],
                          jnp.concatenate([ch_s[1, :, sls[i]].astype(BF16), sh[i].astype(BF16)], axis=0))
                  for i in hh]
            vh = [v_s[rows, sl].astype(BF16) for sl in sls]
            u0 = [gm[i][0:lc, 2 * lc:] + _dot(jnp.where(low_strict, gm[i][0:lc, lc:2 * lc], 0.0), vh[i])
                  for i in hh]
            w = [jnp.concatenate([jnp.where(low_strict, gm[i][0:lc, 0:lc], 0.0), u0[i]], axis=1) for i in hh]
            steps = 1
            while steps < lc:
                r = [_dot(w[i][:, 0:lc], w[i]) for i in hh]
                w = [r[i] + jnp.where(keep_u, w[i], 0.0) for i in hh]
                steps *= 2
            uv = [jnp.concatenate([w[i][:, lc:].astype(BF16), vh[i]], axis=0) for i in hh]
            for i in hh:
                mr = jnp.where(low_incl2, gm[i][lc:2 * lc, 0:2 * lc], 0.0)
                y_s[rows, sls[i]] = gm[i][lc:2 * lc, 2 * lc:] + _dot(mr, uv[i])
            for i in hh:
                st_ref[h0 + i] = sh[i] * g_tot[:, sls[i]] + _dot_tn(uv[i], ch_s[2, :, sls[i]])
        return carry

    lax.fori_loop(0, tm // lc, chunk, 0)

    y = y_s[...]
    mean = seg_bcast(seg_sum(y)) * (1.0 / hd)
    dy = y - mean
    var = seg_bcast(seg_sum(dy * dy)) * (1.0 / hd)
    yn = dy * lax.rsqrt(var + RWKV_GN_EPS) * gnw_ref[...] + gnb_ref[...]
    o_ref[...] = x + _dot((yn + bonus_s[...]) * gate_s[...], wo_ref[...])


def _rwkv7(x2, seq, g, mu, w_rkv, w0, w1, w2, a0, a1, a2, g1, g2, k_k, k_a, r_k, gn_w, gn_b, w_o, tm=256):
    rows = x2.shape[0]
    d = D_MODEL
    head_of = jnp.arange(d) // RWKV_HEAD_DIM
    p = (head_of[:, None] == jnp.arange(128)[None, :]).astype(BF16)
    args = (x2, _row(g), mu.astype(F32), w_rkv[0].astype(BF16), w_rkv[1].astype(BF16), w_rkv[2].astype(BF16),
            _row(w0), w1.astype(BF16), w2.astype(BF16), _row(a0), a1.astype(BF16), a2.astype(BF16),
            g1.astype(BF16), g2.astype(BF16), _row(k_k), _row(k_a), _row(r_k), _row(gn_w), _row(gn_b),
            w_o.astype(BF16), p, p.T)
    in_specs = [pl.BlockSpec((tm, d), lambda i: (i, 0))] + [_const_spec(a.shape) for a in args[1:]]
    big = pltpu.VMEM((tm, d), F32)
    return pl.pallas_call(
        functools.partial(_rwkv_body, tiles_per_seq=seq // tm),
        out_shape=jax.ShapeDtypeStruct((rows, d), F32),
        grid=(rows // tm,),
        in_specs=in_specs,
        out_specs=pl.BlockSpec((tm, d), lambda i: (i, 0)),
        scratch_shapes=[pltpu.VMEM((tm + 2 * SUBLANES, d), F32),
                        pltpu.VMEM((RWKV_HEADS, RWKV_HEAD_DIM, RWKV_HEAD_DIM), F32),
                        big, big, big, big, big, big, big, big, big,
                        pltpu.VMEM((3, 2 * RWKV_CHUNK, d), F32)],
        compiler_params=_params(1),
        name="rwkv7",
    )(*args)


def _ssd_in_body(x_ref, g_ref, w_ref, wdt_ref, zx_ref, dt_ref, *, nc):
    hn = _rms(x_ref[...], g_ref[...])
    h = hn.astype(BF16)
    for n in range(w_ref.shape[1] // nc):
        zx_ref[:, n * nc:(n + 1) * nc] = jnp.dot(h, w_ref[:, n * nc:(n + 1) * nc],
                                                 preferred_element_type=F32)
    h_lo = (hn - h.astype(F32)).astype(BF16)
    w_hi, w_lo = _split2(wdt_ref[...])
    dt_ref[...] = (jnp.dot(h, w_hi, preferred_element_type=F32)
                   + jnp.dot(h, w_lo, preferred_element_type=F32)
                   + jnp.dot(h_lo, w_hi, preferred_element_type=F32))


def _ssd_core_body(zx_ref, dt_ref, cw_ref, cb_ref, dtb_ref, alog_ref, dskip_ref, nw_ref, e_ref, o_ref,
                   cbuf_ref, state_ref, y_s):
    lc = SSD_CHUNK
    di = SSD_D_INNER
    hs = SUBLANES
    gw = di // SSD_GROUPS
    ns = SSD_STATE

    @pl.when(pl.program_id(1) == 0)
    def _():
        cbuf_ref[lc:lc + hs, :] = jnp.zeros((hs, SSD_CONV_DIM), F32)
        state_ref[...] = jnp.zeros_like(state_ref)

    z = zx_ref[:, 0:di]
    xbc_raw = zx_ref[:, di:di + SSD_CONV_DIM]
    cbuf_ref[0:hs, :] = cbuf_ref[lc:lc + hs, :]
    cbuf_ref[hs:hs + lc, :] = xbc_raw
    xbc = cb_ref[...] + cw_ref[SSD_CONV - 1:SSD_CONV, :] * xbc_raw
    win = cbuf_ref[0:hs + lc, :]
    for j in range(SSD_CONV - 1):
        back = SSD_CONV - 1 - j
        xbc = xbc + cw_ref[j:j + 1, :] * pltpu.roll(win, back, 0)[hs:hs + lc, :]
    xbc = jax.nn.silu(xbc)
    xs = xbc[:, 0:di]
    bm = xbc[:, di:di + SSD_GROUPS * ns]
    cm = xbc[:, di + SSD_GROUPS * ns:]

    dt = jax.nn.softplus(dt_ref[...] + dtb_ref[...])
    adt = dt * (-jnp.exp(alog_ref[...]))
    li = lax.broadcasted_iota(jnp.int32, (lc, lc), 0)
    si = lax.broadcasted_iota(jnp.int32, (lc, lc), 1)
    causal = si <= li
    tril = jnp.where(causal, 1.0, 0.0).astype(BF16)
    acs = _dot_exact_lhs(tril, adt)
    acs_t = acs.T
    a_last = acs[lc - 1:lc, :]

    def expand(v):
        hi, lo = _split2(v)
        return (jnp.dot(hi, e_ref[...], preferred_element_type=F32)
                + jnp.dot(lo, e_ref[...], preferred_element_type=F32))

    exp_acs_x = expand(jnp.exp(acs))
    xd = xs * expand(dt)
    xdec = xd * expand(jnp.exp(a_last - acs))

    for g in range(SSD_GROUPS):
        bg = bm[:, g * ns:(g + 1) * ns]
        cg = cm[:, g * ns:(g + 1) * ns]
        cb = _dot_nt(cg, bg)
        gsl = slice(g * gw, (g + 1) * gw)
        st_g = state_ref[:, gsl]
        y_off = _dot(cg, st_g) * exp_acs_x[:, gsl]
        for e in range(SSD_HEADS // SSD_GROUPS):
            h = g * (SSD_HEADS // SSD_GROUPS) + e
            col = jnp.broadcast_to(acs[:, h:h + 1], (lc, lc))
            row = jnp.broadcast_to(acs_t[h:h + 1, :], (lc, lc))
            lm = jnp.where(causal, jnp.exp(jnp.minimum(col - row, 0.0)), 0.0)
            hsl = slice(h * SSD_HEAD_DIM, (h + 1) * SSD_HEAD_DIM)
            y_s[:, hsl] = _dot(cb * lm, xd[:, hsl])
        state_ref[:, gsl] = st_g * exp_acs_x[lc - 1:lc, gsl] + _dot_tn(bg, xdec[:, gsl])
        y_s[:, gsl] = y_s[:, gsl] + y_off

    y = y_s[...] + xs * dskip_ref[...]
    yz = y * jax.nn.silu(z)
    for g in range(SSD_GROUPS):
        gsl = slice(g * gw, (g + 1) * gw)
        yg = yz[:, gsl]
        o_ref[:, gsl] = (yg * lax.rsqrt(jnp.mean(yg * yg, axis=-1, keepdims=True) + SSD_NORM_EPS)
                         * nw_ref[:, gsl])


def _mamba2(x2, batch, seq, g, w_in, conv_w, conv_b, dt_bias, a_log, d_skip, norm_w, w_out, tm=512):
    rows = x2.shape[0]
    d = D_MODEL
    di = SSD_D_INNER
    nzx = di + SSD_CONV_DIM
    zx, dt = pl.pallas_call(
        functools.partial(_ssd_in_body, nc=512),
        out_shape=(jax.ShapeDtypeStruct((rows, nzx), F32), jax.ShapeDtypeStruct((rows, SSD_HEADS), F32)),
        grid=(rows // tm,),
        in_specs=[pl.BlockSpec((tm, d), lambda i: (i, 0)),
                  _const_spec((1, d)),
                  _const_spec((d, nzx)),
                  _const_spec((d, SSD_HEADS))],
        out_specs=(pl.BlockSpec((tm, nzx), lambda i: (i, 0)),
                   pl.BlockSpec((tm, SSD_HEADS), lambda i: (i, 0))),
        compiler_params=_params(1),
        name="ssd_in",
    )(x2, _row(g), w_in[:, :nzx].astype(BF16), w_in[:, nzx:].astype(F32))

    lc = SSD_CHUNK
    nchunk = seq // lc
    expand = (jnp.arange(SSD_HEADS)[:, None] == (jnp.arange(di) // SSD_HEAD_DIM)[None, :]).astype(BF16)
    d_x = jnp.repeat(d_skip.astype(F32), SSD_HEAD_DIM).reshape(1, di)
    args = (zx, dt, conv_w.astype(F32), _row(conv_b), _row(dt_bias), _row(a_log), d_x, _row(norm_w), expand)
    yz = pl.pallas_call(
        _ssd_core_body,
        out_shape=jax.ShapeDtypeStruct((rows, di), F32),
        grid=(batch, nchunk),
        in_specs=[pl.BlockSpec((lc, nzx), lambda b, c: (b * nchunk + c, 0)),
                  pl.BlockSpec((lc, SSD_HEADS), lambda b, c: (b * nchunk + c, 0))]
                 + [_const_spec(a.shape) for a in args[2:]],
        out_specs=pl.BlockSpec((lc, di), lambda b, c: (b * nchunk + c, 0)),
        scratch_shapes=[pltpu.VMEM((lc + 2 * SUBLANES, SSD_CONV_DIM), F32),
                        pltpu.VMEM((SSD_STATE, di), F32),
                        pltpu.VMEM((lc, di), F32)],
        compiler_params=_params(2),
        name="ssd_core",
    )(*args)
    return _proj_res(x2, yz, w_out)


def _conf_body(x_ref, g_ref, w1_ref, b1_ref, dw_ref, dwb_ref, lnw_ref, lnb_ref, w2_ref, b2_ref, o_ref,
               ubuf_ref, c_s, *, tiles_per_seq, rb, lanes):
    tm = x_ref.shape[0]
    d = D_MODEL
    hl = CONF_HALO

    @pl.when(pl.program_id(0) % tiles_per_seq == 0)
    def _():
        ubuf_ref[tm:tm + hl, :] = jnp.zeros((hl, d), F32)

    x = x_ref[...]
    h = _rms(x, g_ref[...]).astype(BF16)
    p = jnp.dot(h, w1_ref[...], preferred_element_type=F32) + b1_ref[...]
    u = p[:, 0:d] * jax.nn.sigmoid(p[:, d:2 * d])
    ubuf_ref[0:hl, :] = ubuf_ref[tm:tm + hl, :]
    ubuf_ref[hl:hl + tm, :] = u
    base = hl - (CONF_KERNEL - 1)
    for r0 in range(0, tm, rb):
        for c0 in range(0, d, lanes):
            acc = jnp.broadcast_to(dwb_ref[:, c0:c0 + lanes], (rb, lanes))
            win = ubuf_ref[r0:r0 + rb + hl, c0:c0 + lanes]
            for res in range(SUBLANES):
                offs = [o for o in range(base, base + CONF_KERNEL) if o % SUBLANES == res]
                slab = win if res == 0 else pltpu.roll(win, rb + hl - res, 0)
                for o in offs:
                    a0 = o - res
                    acc = acc + dw_ref[o - base:o - base + 1, c0:c0 + lanes] * slab[a0:a0 + rb, :]
            c_s[r0:r0 + rb, c0:c0 + lanes] = acc
    y = c_s[...]
    mu = jnp.mean(y, axis=-1, keepdims=True)
    dy = y - mu
    var = jnp.mean(dy * dy, axis=-1, keepdims=True)
    zn = dy * lax.rsqrt(var + CONF_LN_EPS) * lnw_ref[...] + lnb_ref[...]
    o_ref[...] = x + _dot(jax.nn.silu(zn), w2_ref[...]) + b2_ref[...]


def _conformer(x2, seq, g, w_pw1, b_pw1, dw_w, dw_b, ln_w, ln_b, w_pw2, b_pw2, tm=256):
    rows = x2.shape[0]
    d = D_MODEL
    args = (x2, _row(g), w_pw1.astype(BF16), _row(b_pw1), dw_w.astype(F32), _row(dw_b), _row(ln_w),
            _row(ln_b), w_pw2.astype(BF16), _row(b_pw2))
    return pl.pallas_call(
        functools.partial(_conf_body, tiles_per_seq=seq // tm, rb=64, lanes=128),
        out_shape=jax.ShapeDtypeStruct((rows, d), F32),
        grid=(rows // tm,),
        in_specs=[pl.BlockSpec((tm, d), lambda i: (i, 0))] + [_const_spec(a.shape) for a in args[1:]],
        out_specs=pl.BlockSpec((tm, d), lambda i: (i, 0)),
        scratch_shapes=[pltpu.VMEM((tm + CONF_HALO, d), F32), pltpu.VMEM((tm, d), F32)],
        compiler_params=_params(1),
        name="conformer",
    )(*args)


def kernel(x, norm_mix, norm_ffn, norm_final, ffn_w_up, ffn_conv_w, ffn_conv_b, ffn_w_down, moba_w_qkv, moba_w_o, rwkv_mu, rwkv_w_rkv, rwkv_w0, rwkv_w1, rwkv_w2, rwkv_a0, rwkv_a1, rwkv_a2, rwkv_g1, rwkv_g2, rwkv_k_k, rwkv_k_a, rwkv_r_k, rwkv_gn_w, rwkv_gn_b, rwkv_w_o, ssd_w_in, ssd_conv_w, ssd_conv_b, ssd_dt_bias, ssd_a_log, ssd_d, ssd_norm_w, ssd_w_out, conf_w_pw1, conf_b_pw1, conf_dw_w, conf_dw_b, conf_ln_w, conf_ln_b, conf_w_pw2, conf_b_pw2):
    b, s, d = x.shape
    depth = norm_mix.shape[0]
    rows = b * s
    for i in range(depth):
        kind, j = i % 4, i // 4
        if kind == 0:
            x2 = _moba(x.reshape(b, s, d), norm_mix[i], moba_w_qkv[j], moba_w_o[j]).reshape(rows, d)
        elif kind == 1:
            x2 = _rwkv7(x.reshape(rows, d), s, norm_mix[i], rwkv_mu[j], rwkv_w_rkv[j], rwkv_w0[j], rwkv_w1[j],
                        rwkv_w2[j], rwkv_a0[j], rwkv_a1[j], rwkv_a2[j], rwkv_g1[j], rwkv_g2[j], rwkv_k_k[j],
                        rwkv_k_a[j], rwkv_r_k[j], rwkv_gn_w[j], rwkv_gn_b[j], rwkv_w_o[j])
        elif kind == 2:
            x2 = _mamba2(x.reshape(rows, d), b, s, norm_mix[i], ssd_w_in[j], ssd_conv_w[j], ssd_conv_b[j],
                         ssd_dt_bias[j], ssd_a_log[j], ssd_d[j], ssd_norm_w[j], ssd_w_out[j])
        else:
            x2 = _conformer(x.reshape(rows, d), s, norm_mix[i], conf_w_pw1[j], conf_b_pw1[j], conf_dw_w[j],
                            conf_dw_b[j], conf_ln_w[j], conf_ln_b[j], conf_w_pw2[j], conf_b_pw2[j])
        x = _conv_ffn(x2, s, norm_ffn[i], ffn_w_up[i], ffn_conv_w[i], ffn_conv_b[i], ffn_w_down[i],
                      norm_final, final=(i == depth - 1))
    return x.reshape(b, s, d)
```

```python
import functools

import jax
import jax.numpy as jnp
from jax import lax
from jax.experimental import pallas as pl
from jax.experimental.pallas import tpu as pltpu

F32 = jnp.float32
BF16 = jnp.bfloat16

D_MODEL = 1024
NORM_EPS = 1e-6
NEG_INF = -1e30
LOG2E = 1.4426950408889634

MOBA_HEADS = 16
MOBA_HEAD_DIM = 64
MOBA_BLOCK = 256
MOBA_TOPK = 3
MOBA_ONES_ROWS = 16
MOBA_HEAD_GROUP = 8

RWKV_HEADS = 16
RWKV_HEAD_DIM = 64
RWKV_GN_EPS = 64e-5
RWKV_CHUNK = 64
RWKV_HEAD_GROUP = 16

SSD_D_INNER = 2048
SSD_HEAD_DIM = 64
SSD_HEADS = 32
SSD_GROUPS = 4
SSD_STATE = 128
SSD_CONV = 4
SSD_CHUNK = 128
SSD_CONV_DIM = SSD_D_INNER + 2 * SSD_GROUPS * SSD_STATE
SSD_NORM_EPS = 1e-5

CONF_KERNEL = 31
CONF_LN_EPS = 1e-5
CONF_HALO = 32

D_FF = 2816
FFN_CONV = 3
FFN_CHUNK = 256

SUBLANES = 8
VMEM_LIMIT = 56 * 1024 * 1024


def _rms(x, g, eps=NORM_EPS):
    return x * lax.rsqrt(jnp.mean(x * x, axis=-1, keepdims=True) + eps) * g


def _dot(a, b):
    return jnp.dot(a.astype(BF16), b.astype(BF16), preferred_element_type=F32)


def _dot_nt(a, b):
    return lax.dot_general(a.astype(BF16), b.astype(BF16), (((1,), (1,)), ((), ())),
                           preferred_element_type=F32)


def _dot_tn(a, b):
    return lax.dot_general(a.astype(BF16), b.astype(BF16), (((0,), (0,)), ((), ())),
                           preferred_element_type=F32)


def _split2(a):
    hi = a.astype(BF16)
    lo = (a - hi.astype(F32)).astype(BF16)
    return hi, lo


def _split3(a):
    hi = a.astype(BF16)
    r = a - hi.astype(F32)
    mid = r.astype(BF16)
    lo = (r - mid.astype(F32)).astype(BF16)
    return hi, mid, lo


def _dot_exact_rhs(a, b_bf16):
    hi, mid, lo = _split3(a)
    return (jnp.dot(hi, b_bf16, preferred_element_type=F32)
            + jnp.dot(mid, b_bf16, preferred_element_type=F32)
            + jnp.dot(lo, b_bf16, preferred_element_type=F32))


def _dot_exact_lhs(a_bf16, b):
    hi, mid, lo = _split3(b)
    return (jnp.dot(a_bf16, hi, preferred_element_type=F32)
            + jnp.dot(a_bf16, mid, preferred_element_type=F32)
            + jnp.dot(a_bf16, lo, preferred_element_type=F32))


def _const_spec(shape):
    n = len(shape)
    return pl.BlockSpec(shape, lambda *_: (0,) * n, pipeline_mode=pl.Buffered(1))


def _params(n_axes):
    return pltpu.CompilerParams(dimension_semantics=("arbitrary",) * n_axes,
                                vmem_limit_bytes=VMEM_LIMIT)


def _row(v):
    return v.reshape(1, -1).astype(F32)


def _ffn_body(x_ref, g_ref, wup_ref, cw_ref, cb_ref, wdn_ref, gf_ref, o_ref, halo_ref, act_ref,
              *, tiles_per_seq, final):
    tm = x_ref.shape[0]
    fc = FFN_CHUNK
    hs = SUBLANES

    @pl.when(pl.program_id(0) % tiles_per_seq == 0)
    def _():
        halo_ref[...] = jnp.zeros_like(halo_ref)

    x = x_ref[...]
    h = _rms(x, g_ref[...]).astype(BF16)

    r8 = lax.broadcasted_iota(jnp.int32, (hs, fc), 0)

    def conv(u, off):
        prev = halo_ref[:, off:off + fc]
        halo_ref[:, off:off + fc] = u[tm - hs:tm, :]
        w0 = cw_ref[0:1, off:off + fc]
        w1 = cw_ref[1:2, off:off + fc]
        u1 = pltpu.roll(u, 1, 0)
        u2 = pltpu.roll(u, 2, 0)
        y = w0 * u2 + w1 * u1 + cw_ref[2:3, off:off + fc] * u + cb_ref[0:1, off:off + fc]
        fix = (w0 * jnp.where(r8 < 2, pltpu.roll(prev, 2, 0) - u2[0:hs, :], 0.0)
               + w1 * jnp.where(r8 < 1, pltpu.roll(prev, 1, 0) - u1[0:hs, :], 0.0))
        return jnp.concatenate([y[0:hs, :] + fix, y[hs:, :]], axis=0)

    def up(c):
        og, ou = c * fc, D_FF + c * fc
        return (jnp.dot(h, wup_ref[:, og:og + fc], preferred_element_type=F32),
                jnp.dot(h, wup_ref[:, ou:ou + fc], preferred_element_type=F32))

    n_chunks = D_FF // fc
    nxt = up(0)
    for c in range(n_chunks):
        ug, uu = nxt
        if c + 1 < n_chunks:
            nxt = up(c + 1)
        yg = conv(ug, c * fc)
        yu = conv(uu, D_FF + c * fc)
        act_ref[:, c * fc:(c + 1) * fc] = (jax.nn.silu(yg) * yu).astype(BF16)
    y = x + jnp.dot(act_ref[...], wdn_ref[...], preferred_element_type=F32)
    if final:
        y = _rms(y, gf_ref[...])
    o_ref[...] = y


def _conv_ffn(x2, seq, g, w_up, conv_w, conv_b, w_down, g_final, final, tm=512):
    rows = x2.shape[0]
    body = functools.partial(_ffn_body, tiles_per_seq=seq // tm, final=final)
    return pl.pallas_call(
        body,
        out_shape=jax.ShapeDtypeStruct((rows, D_MODEL), F32),
        grid=(rows // tm,),
        in_specs=[
            pl.BlockSpec((tm, D_MODEL), lambda i: (i, 0)),
            _const_spec((1, D_MODEL)),
            _const_spec((D_MODEL, 2 * D_FF)),
            _const_spec((FFN_CONV, 2 * D_FF)),
            _const_spec((1, 2 * D_FF)),
            _const_spec((D_FF, D_MODEL)),
            _const_spec((1, D_MODEL)),
        ],
        out_specs=pl.BlockSpec((tm, D_MODEL), lambda i: (i, 0)),
        scratch_shapes=[pltpu.VMEM((SUBLANES, 2 * D_FF), F32), pltpu.VMEM((tm, D_FF), BF16)],
        compiler_params=_params(1),
        name="conv_ffn",
    )(x2, _row(g), w_up.astype(BF16), conv_w.astype(F32), _row(conv_b), w_down.astype(BF16), _row(g_final))


def _proj_res_body(x_ref, y_ref, w_ref, o_ref):
    o_ref[...] = x_ref[...] + jnp.dot(y_ref[...].astype(BF16), w_ref[...], preferred_element_type=F32)


def _proj_res(x2, y2, w, tm=512):
    rows, k = y2.shape
    return pl.pallas_call(
        _proj_res_body,
        out_shape=jax.ShapeDtypeStruct((rows, D_MODEL), F32),
        grid=(rows // tm,),
        in_specs=[pl.BlockSpec((tm, D_MODEL), lambda i: (i, 0)),
                  pl.BlockSpec((tm, k), lambda i: (i, 0)),
                  _const_spec((k, D_MODEL))],
        out_specs=pl.BlockSpec((tm, D_MODEL), lambda i: (i, 0)),
        compiler_params=_params(1),
        name="proj_res",
    )(x2, y2, w.astype(BF16))


def _moba_qkv_body(x_ref, g_ref, wt_ref, o_ref, *, nc):
    h = _rms(x_ref[0], g_ref[...]).astype(BF16)
    for n in range(3 * D_MODEL // nc):
        o_ref[0, n * nc:(n + 1) * nc, :] = lax.dot_general(
            wt_ref[n * nc:(n + 1) * nc, :], h, (((1,), (1,)), ((), ())),
            preferred_element_type=F32).astype(BF16)


def _moba_attn_body(q_ref, k_ref, v_ref, o_ref, kaug_ref, vblk_ref, vpair_ref, kmean_ref, *, nb):
    bs = MOBA_BLOCK
    dh = MOBA_HEAD_DIM
    hg = MOBA_HEAD_GROUP
    hh = range(hg)
    iq = pl.program_id(2)

    @pl.when(iq == 0)
    def _():
        blk = lax.broadcasted_iota(jnp.int32, (bs, nb), 1)
        for j in range(nb):
            kr = k_ref[0, :, j * bs:(j + 1) * bs].astype(F32).T
            onehot = jnp.where(blk == j, 1.0, 0.0).astype(BF16)
            for i in hh:
                kri = kr[:, i * dh:(i + 1) * dh]
                kaug_ref[i, j // 2, (j % 2) * bs:(j % 2 + 1) * bs, :] = jnp.concatenate(
                    [kri.astype(BF16), onehot], axis=1)
                kmean_ref[i, j:j + 1, :] = jnp.mean(kri, axis=0, keepdims=True)
                vblk_ref[i, j, 0:dh, :] = v_ref[0, i * dh:(i + 1) * dh, j * bs:(j + 1) * bs]
                vblk_ref[i, j, dh:dh + MOBA_ONES_ROWS, :] = jnp.ones((MOBA_ONES_ROWS, bs), BF16)
                if j % 2 == 0:
                    vpair_ref[i, j // 2, 0:dh, :] = v_ref[0, i * dh:(i + 1) * dh, j * bs:(j + 2) * bs]
                    vpair_ref[i, j // 2, dh:dh + MOBA_ONES_ROWS, :] = jnp.ones((MOBA_ONES_ROWS, 2 * bs), BF16)

    jidx = lax.broadcasted_iota(jnp.int32, (nb, bs), 0)
    qs, qaug = [], []
    for i in hh:
        q_raw = q_ref[0, i * dh:(i + 1) * dh, :]
        km_hi, km_lo = _split2(kmean_ref[i])
        gate = (jnp.dot(km_hi, q_raw, preferred_element_type=F32)
                + jnp.dot(km_lo, q_raw, preferred_element_type=F32))
        cnt = jnp.zeros((nb, bs), jnp.int32)
        for jp in range(nb):
            gj = gate[jp:jp + 1, :]
            beats = (gj > gate) | ((gj == gate) & (jp < jidx))
            cnt = cnt + jnp.where(beats & (jp < iq), 1, 0)
        selm = (cnt < MOBA_TOPK) & (jidx < iq)
        qs.append(q_raw * (dh ** -0.5 * LOG2E))
        qaug.append(jnp.concatenate([qs[i], jnp.where(selm, 0.0, NEG_INF).astype(BF16)], axis=0))

    kpos = lax.broadcasted_iota(jnp.int32, (bs, bs), 0)
    qpos = lax.broadcasted_iota(jnp.int32, (bs, bs), 1)
    causal = kpos <= qpos
    own_rows = pl.ds(pl.multiple_of((iq % 2) * bs, bs), bs)
    s = [jnp.where(causal, jnp.dot(kaug_ref[i, iq // 2, own_rows, :][:, 0:dh], qs[i],
                                   preferred_element_type=F32), NEG_INF)
         for i in hh]
    m0 = [jnp.max(s[i], axis=0, keepdims=True) for i in hh]
    p = [jnp.exp2(s[i] - m0[i]) for i in hh]
    acc0 = [jnp.dot(vblk_ref[i, iq], p[i].astype(BF16), preferred_element_type=F32) for i in hh]

    def step(j, carry):
        m, acc = carry
        sj = [jnp.dot(kaug_ref[i, j], qaug[i], preferred_element_type=F32) for i in hh]
        m_new = [jnp.maximum(m[i], jnp.max(sj[i], axis=0, keepdims=True)) for i in hh]
        alpha = [jnp.exp2(m[i] - m_new[i]) for i in hh]
        pj = [jnp.exp2(sj[i] - m_new[i]) for i in hh]
        acc = [alpha[i] * acc[i] + jnp.dot(vpair_ref[i, j], pj[i].astype(BF16), preferred_element_type=F32)
               for i in hh]
        return m_new, acc

    _, acc = lax.fori_loop(0, (iq + 1) // 2, step, (m0, acc0))
    for i in hh:
        o_ref[0, i * dh:(i + 1) * dh, :] = (acc[i][0:dh, :] / acc[i][dh:dh + 1, :]).astype(BF16)


def _moba_out_body(x_ref, o_ref_in, w_ref, o_ref):
    o_ref[0] = x_ref[0] + lax.dot_general(o_ref_in[0], w_ref[...], (((0,), (0,)), ((), ())),
                                          preferred_element_type=F32)


def _moba(x, g, w_qkv, w_o, ts=512):
    b, s, d = x.shape
    nb = s // MOBA_BLOCK
    dh = MOBA_HEAD_DIM
    qkvt = pl.pallas_call(
        functools.partial(_moba_qkv_body, nc=512),
        out_shape=jax.ShapeDtypeStruct((b, 3 * d, s), BF16),
        grid=(b, s // ts),
        in_specs=[pl.BlockSpec((1, ts, d), lambda i, j: (i, j, 0)),
                  _const_spec((1, d)),
                  _const_spec((3 * d, d))],
        out_specs=pl.BlockSpec((1, 3 * d, ts), lambda i, j: (i, 0, j)),
        compiler_params=_params(2),
        name="moba_qkv",
    )(x, _row(g), w_qkv.T.astype(BF16))

    hg = MOBA_HEAD_GROUP
    ng = MOBA_HEADS // hg
    gd = hg * dh
    ot = pl.pallas_call(
        functools.partial(_moba_attn_body, nb=nb),
        out_shape=jax.ShapeDtypeStruct((b, d, s), BF16),
        grid=(b, ng, nb),
        in_specs=[pl.BlockSpec((1, gd, MOBA_BLOCK), lambda i, j, k: (i, j, k)),
                  pl.BlockSpec((1, gd, s), lambda i, j, k: (i, ng + j, 0)),
                  pl.BlockSpec((1, gd, s), lambda i, j, k: (i, 2 * ng + j, 0))],
        out_specs=pl.BlockSpec((1, gd, MOBA_BLOCK), lambda i, j, k: (i, j, k)),
        scratch_shapes=[pltpu.VMEM((hg, nb // 2, 2 * MOBA_BLOCK, dh + nb), BF16),
                        pltpu.VMEM((hg, nb, dh + MOBA_ONES_ROWS, MOBA_BLOCK), BF16),
                        pltpu.VMEM((hg, nb // 2, dh + MOBA_ONES_ROWS, 2 * MOBA_BLOCK), BF16),
                        pltpu.VMEM((hg, nb, dh), F32)],
        compiler_params=_params(3),
        name="moba_attn",
    )(qkvt, qkvt, qkvt)

    return pl.pallas_call(
        _moba_out_body,
        out_shape=jax.ShapeDtypeStruct((b, s, d), F32),
        grid=(b, s // ts),
        in_specs=[pl.BlockSpec((1, ts, d), lambda i, j: (i, j, 0)),
                  pl.BlockSpec((1, d, ts), lambda i, j: (i, 0, j)),
                  _const_spec((d, d))],
        out_specs=pl.BlockSpec((1, ts, d), lambda i, j: (i, j, 0)),
        compiler_params=_params(2),
        name="moba_out",
    )(x, ot, w_o.astype(BF16))


def _rwkv_body(x_ref, g_ref, mu_ref, wr_ref, wk_ref, wv_ref, w0_ref, w1_ref, w2_ref, a0_ref, a1_ref,
               a2_ref, g1_ref, g2_ref, kk_ref, ka_ref, rk_ref, gnw_ref, gnb_ref, wo_ref, p_ref, pt_ref,
               o_ref,
               hbuf_ref, st_ref, r_s, lw_s, k_s, v_s, a_s, b_s, y_s, gate_s, bonus_s,
               ch0, t0, mv0, mr0, gt0, ch1, t1, mv1, mr1, gt1, *, tiles_per_seq):
    bufs = ((ch0, t0, mv0, mr0, gt0), (ch1, t1, mv1, mr1, gt1))
    tm = x_ref.shape[0]
    lc = RWKV_CHUNK
    hd = RWKV_HEAD_DIM
    hs = SUBLANES

    @pl.when(pl.program_id(0) % tiles_per_seq == 0)
    def _():
        hbuf_ref[tm:tm + hs, :] = jnp.zeros((hs, D_MODEL), F32)
        st_ref[...] = jnp.zeros_like(st_ref)

    def seg_sum(v):
        hi, lo = _split2(v)
        return (jnp.dot(hi, p_ref[...], preferred_element_type=F32)
                + jnp.dot(lo, p_ref[...], preferred_element_type=F32))

    def seg_bcast(v):
        hi, lo = _split2(v)
        return (jnp.dot(hi, pt_ref[...], preferred_element_type=F32)
                + jnp.dot(lo, pt_ref[...], preferred_element_type=F32))

    x = x_ref[...]
    hn = _rms(x, g_ref[...])
    hbuf_ref[0:hs, :] = hbuf_ref[tm:tm + hs, :]
    hbuf_ref[hs:hs + tm, :] = hn
    xx = hbuf_ref[hs - 1:hs - 1 + tm, :] - hn

    def mix(i):
        return hn + xx * mu_ref[i:i + 1, :]

    r = _dot(mix(0), wr_ref[...])
    k = _dot(mix(1), wk_ref[...])
    v = _dot(mix(2), wv_ref[...])
    wpre = w0_ref[...] + _dot(jnp.tanh(_dot(mix(3), w1_ref[...])), w2_ref[...])
    lw = -jnp.exp(-jax.nn.softplus(-wpre) - 0.5)
    a = jax.nn.sigmoid(a0_ref[...] + _dot(_dot(mix(4), a1_ref[...]), a2_ref[...]))
    gate_s[...] = _dot(jax.nn.sigmoid(_dot(mix(5), g1_ref[...])), g2_ref[...])

    kk = k * kk_ref[...]
    inv = 1.0 / jnp.maximum(jnp.sqrt(seg_sum(kk * kk)), 1e-12)
    kk = kk * seg_bcast(inv)
    k = k * (1.0 + (a - 1.0) * ka_ref[...])
    bonus_s[...] = seg_bcast(seg_sum(r * k * rk_ref[...])) * v
    r_s[...] = r
    lw_s[...] = lw
    k_s[...] = k
    v_s[...] = v
    a_s[...] = -kk
    b_s[...] = kk * a

    ti = lax.broadcasted_iota(jnp.int32, (lc, lc), 0)
    si = lax.broadcasted_iota(jnp.int32, (lc, lc), 1)
    low_strict = si < ti
    tril = jnp.where(si <= ti, 1.0, 0.0).astype(BF16)
    ti2 = lax.broadcasted_iota(jnp.int32, (lc, 2 * lc), 0)
    si2 = lax.broadcasted_iota(jnp.int32, (lc, 2 * lc), 1)
    low_incl2 = jnp.where(si2 >= lc, si2 - lc, si2) <= ti2
    keep_u = si2 >= lc
    low_strict2 = si2 < ti2

    hh = range(RWKV_HEADS)
    sls = [slice(h * hd, (h + 1) * hd) for h in hh]
    eye2 = jnp.where(si2 - lc == ti2, 1.0, 0.0)

    def prepare(c):
        ch, t_s, mv_s, mr_s, gt_s = bufs[c % 2]
        rows = slice(c * lc, (c + 1) * lc)
        lw_c = lw_s[rows, :]
        lg = _dot_exact_lhs(tril, lw_c)
        lg_last = lg[lc - 1:lc, :]
        g_inv = jnp.exp(-lg)
        g_tail = jnp.exp(lg_last - lg)
        gt_s[...] = jnp.broadcast_to(jnp.exp(lg_last), gt_s.shape)
        ch[0, 0:lc, :] = a_s[rows, :] * jnp.exp(lg - lw_c)
        ch[0, lc:2 * lc, :] = r_s[rows, :] * jnp.exp(lg)
        ch[1, 0:lc, :] = b_s[rows, :] * g_inv
        ch[1, lc:2 * lc, :] = k_s[rows, :] * g_inv
        ch[2, 0:lc, :] = b_s[rows, :] * g_tail
        ch[2, lc:2 * lc, :] = k_s[rows, :] * g_tail
        gm = [_dot_nt(ch[0, :, sls[h]], ch[1, :, sls[h]]) for h in hh]
        for h in hh:
            mr_s[h] = jnp.where(low_incl2, gm[h][lc:2 * lc, :], 0.0).astype(BF16)
            mv_s[h] = _dot(jnp.where(low_strict, gm[h][0:lc, lc:2 * lc], 0.0), v_s[rows, sls[h]])
        w = [jnp.where(keep_u, eye2, jnp.where(low_strict2, gm[h][0:lc, :], 0.0)) for h in hh]
        steps = 1
        while steps < lc:
            r = [_dot(w[h][:, 0:lc], w[h]) for h in hh]
            w = [r[h] + jnp.where(keep_u, w[h], 0.0) for h in hh]
            steps *= 2
        for h in hh:
            t_s[h] = w[h][:, lc:].astype(BF16)

    def advance(c):
        ch, t_s, mv_s, mr_s, gt_s = bufs[c % 2]
        rows = slice(c * lc, (c + 1) * lc)
        sh = [st_ref[h] for h in hh]
        ars = [_dot_nt(ch[0, :, sls[h]], sh[h]) for h in hh]
        u = [jnp.dot(t_s[h], (ars[h][0:lc, :] + mv_s[h]).astype(BF16), preferred_element_type=F32)
             for h in hh]
        uv = [jnp.concatenate([u[h].astype(BF16), v_s[rows, sls[h]].astype(BF16)], axis=0) for h in hh]
        for h in hh:
            y_s[rows, sls[h]] = ars[h][lc:2 * lc, :] + jnp.dot(mr_s[h], uv[h], preferred_element_type=F32)
        for h in hh:
            st_ref[h] = sh[h] * gt_s[0:1, sls[h]] + _dot_tn(uv[h], ch[2, :, sls[h]])

    n_chunks = tm // lc
    prepare(0)
    for c in range(n_chunks):
        if c + 1 < n_chunks:
            prepare(c + 1)
        advance(c)

    y = y_s[...]
    mean = seg_bcast(seg_sum(y)) * (1.0 / hd)
    dy = y - mean
    var = seg_bcast(seg_sum(dy * dy)) * (1.0 / hd)
    yn = dy * lax.rsqrt(var + RWKV_GN_EPS) * gnw_ref[...] + gnb_ref[...]
    o_ref[...] = x + _dot((yn + bonus_s[...]) * gate_s[...], wo_ref[...])


def _rwkv7(x2, seq, g, mu, w_rkv, w0, w1, w2, a0, a1, a2, g1, g2, k_k, k_a, r_k, gn_w, gn_b, w_o, tm=256):
    rows = x2.shape[0]
    d = D_MODEL
    head_of = jnp.arange(d) // RWKV_HEAD_DIM
    p = (head_of[:, None] == jnp.arange(128)[None, :]).astype(BF16)
    args = (x2, _row(g), mu.astype(F32), w_rkv[0].astype(BF16), w_rkv[1].astype(BF16), w_rkv[2].astype(BF16),
            _row(w0), w1.astype(BF16), w2.astype(BF16), _row(a0), a1.astype(BF16), a2.astype(BF16),
            g1.astype(BF16), g2.astype(BF16), _row(k_k), _row(k_a), _row(r_k), _row(gn_w), _row(gn_b),
            w_o.astype(BF16), p, p.T)
    in_specs = [pl.BlockSpec((tm, d), lambda i: (i, 0))] + [_const_spec(a.shape) for a in args[1:]]
    big = pltpu.VMEM((tm, d), F32)
    lc = RWKV_CHUNK
    return pl.pallas_call(
        functools.partial(_rwkv_body, tiles_per_seq=seq // tm),
        out_shape=jax.ShapeDtypeStruct((rows, d), F32),
        grid=(rows // tm,),
        in_specs=in_specs,
        out_specs=pl.BlockSpec((tm, d), lambda i: (i, 0)),
        scratch_shapes=[pltpu.VMEM((tm + 2 * SUBLANES, d), F32),
                        pltpu.VMEM((RWKV_HEADS, RWKV_HEAD_DIM, RWKV_HEAD_DIM), F32),
                        big, big, big, big, big, big, big, big, big,
                        ] + 2 * [pltpu.VMEM((3, 2 * lc, d), F32),
                                 pltpu.VMEM((RWKV_HEADS, lc, lc), BF16),
                                 pltpu.VMEM((RWKV_HEADS, lc, RWKV_HEAD_DIM), F32),
                                 pltpu.VMEM((RWKV_HEADS, lc, 2 * lc), BF16),
                                 pltpu.VMEM((SUBLANES, d), F32)],
        compiler_params=_params(1),
        name="rwkv7",
    )(*args)


def _ssd_in_body(x_ref, g_ref, w_ref, wdt_ref, zx_ref, dt_ref, *, nc):
    hn = _rms(x_ref[...], g_ref[...])
    h = hn.astype(BF16)
    for n in range(w_ref.shape[1] // nc):
        zx_ref[:, n * nc:(n + 1) * nc] = jnp.dot(h, w_ref[:, n * nc:(n + 1) * nc],
                                                 preferred_element_type=F32)
    h_lo = (hn - h.astype(F32)).astype(BF16)
    w_hi, w_lo = _split2(wdt_ref[...])
    dt_ref[...] = (jnp.dot(h, w_hi, preferred_element_type=F32)
                   + jnp.dot(h, w_lo, preferred_element_type=F32)
                   + jnp.dot(h_lo, w_hi, preferred_element_type=F32))


def _ssd_core_body(zx_ref, dt_ref, cw_ref, cb_ref, dtb_ref, alog_ref, dskip_ref, nw_ref, e_ref, o_ref,
                   cbuf_ref, state_ref, y_s):
    lc = SSD_CHUNK
    di = SSD_D_INNER
    hs = SUBLANES
    gw = di // SSD_GROUPS
    ns = SSD_STATE

    @pl.when(pl.program_id(1) == 0)
    def _():
        cbuf_ref[lc:lc + hs, :] = jnp.zeros((hs, SSD_CONV_DIM), F32)
        state_ref[...] = jnp.zeros_like(state_ref)

    z = zx_ref[:, 0:di]
    xbc_raw = zx_ref[:, di:di + SSD_CONV_DIM]
    cbuf_ref[0:hs, :] = cbuf_ref[lc:lc + hs, :]
    cbuf_ref[hs:hs + lc, :] = xbc_raw
    xbc = cb_ref[...] + cw_ref[SSD_CONV - 1:SSD_CONV, :] * xbc_raw
    win = cbuf_ref[0:hs + lc, :]
    for j in range(SSD_CONV - 1):
        back = SSD_CONV - 1 - j
        xbc = xbc + cw_ref[j:j + 1, :] * pltpu.roll(win, back, 0)[hs:hs + lc, :]
    xbc = jax.nn.silu(xbc)
    xs = xbc[:, 0:di]
    bm = xbc[:, di:di + SSD_GROUPS * ns]
    cm = xbc[:, di + SSD_GROUPS * ns:]

    dt = jax.nn.softplus(dt_ref[...] + dtb_ref[...])
    adt = dt * (-jnp.exp(alog_ref[...]))
    li = lax.broadcasted_iota(jnp.int32, (lc, lc), 0)
    si = lax.broadcasted_iota(jnp.int32, (lc, lc), 1)
    causal = si <= li
    tril = jnp.where(causal, 1.0, 0.0).astype(BF16)
    acs = _dot_exact_lhs(tril, adt)
    acs_t = acs.T
    a_last = acs[lc - 1:lc, :]

    def expand(v):
        hi, lo = _split2(v)
        return (jnp.dot(hi, e_ref[...], preferred_element_type=F32)
                + jnp.dot(lo, e_ref[...], preferred_element_type=F32))

    exp_acs_x = expand(jnp.exp(acs))
    xd = xs * expand(dt)
    xdec = xd * expand(jnp.exp(a_last - acs))

    for g in range(SSD_GROUPS):
        bg = bm[:, g * ns:(g + 1) * ns]
        cg = cm[:, g * ns:(g + 1) * ns]
        cb = _dot_nt(cg, bg)
        gsl = slice(g * gw, (g + 1) * gw)
        st_g = state_ref[:, gsl]
        y_off = _dot(cg, st_g) * exp_acs_x[:, gsl]
        for e in range(SSD_HEADS // SSD_GROUPS):
            h = g * (SSD_HEADS // SSD_GROUPS) + e
            col = jnp.broadcast_to(acs[:, h:h + 1], (lc, lc))
            row = jnp.broadcast_to(acs_t[h:h + 1, :], (lc, lc))
            lm = jnp.where(causal, jnp.exp(jnp.minimum(col - row, 0.0)), 0.0)
            hsl = slice(h * SSD_HEAD_DIM, (h + 1) * SSD_HEAD_DIM)
            y_s[:, hsl] = _dot(cb * lm, xd[:, hsl])
        state_ref[:, gsl] = st_g * exp_acs_x[lc - 1:lc, gsl] + _dot_tn(bg, xdec[:, gsl])
        y_s[:, gsl] = y_s[:, gsl] + y_off

    y = y_s[...] + xs * dskip_ref[...]
    yz = y * jax.nn.silu(z)
    for g in range(SSD_GROUPS):
        gsl = slice(g * gw, (g + 1) * gw)
        yg = yz[:, gsl]
        o_ref[:, gsl] = (yg * lax.rsqrt(jnp.mean(yg * yg, axis=-1, keepdims=True) + SSD_NORM_EPS)
                         * nw_ref[:, gsl])


def _mamba2(x2, batch, seq, g, w_in, conv_w, conv_b, dt_bias, a_log, d_skip, norm_w, w_out, tm=512):
    rows = x2.shape[0]
    d = D_MODEL
    di = SSD_D_INNER
    nzx = di + SSD_CONV_DIM
    zx, dt = pl.pallas_call(
        functools.partial(_ssd_in_body, nc=512),
        out_shape=(jax.ShapeDtypeStruct((rows, nzx), F32), jax.ShapeDtypeStruct((rows, SSD_HEADS), F32)),
        grid=(rows // tm,),
        in_specs=[pl.BlockSpec((tm, d), lambda i: (i, 0)),
                  _const_spec((1, d)),
                  _const_spec((d, nzx)),
                  _const_spec((d, SSD_HEADS))],
        out_specs=(pl.BlockSpec((tm, nzx), lambda i: (i, 0)),
                   pl.BlockSpec((tm, SSD_HEADS), lambda i: (i, 0))),
        compiler_params=_params(1),
        name="ssd_in",
    )(x2, _row(g), w_in[:, :nzx].astype(BF16), w_in[:, nzx:].astype(F32))

    lc = SSD_CHUNK
    nchunk = seq // lc
    expand = (jnp.arange(SSD_HEADS)[:, None] == (jnp.arange(di) // SSD_HEAD_DIM)[None, :]).astype(BF16)
    d_x = jnp.repeat(d_skip.astype(F32), SSD_HEAD_DIM).reshape(1, di)
    args = (zx, dt, conv_w.astype(F32), _row(conv_b), _row(dt_bias), _row(a_log), d_x, _row(norm_w), expand)
    yz = pl.pallas_call(
        _ssd_core_body,
        out_shape=jax.ShapeDtypeStruct((rows, di), F32),
        grid=(batch, nchunk),
        in_specs=[pl.BlockSpec((lc, nzx), lambda b, c: (b * nchunk + c, 0)),
                  pl.BlockSpec((lc, SSD_HEADS), lambda b, c: (b * nchunk + c, 0))]
                 + [_const_spec(a.shape) for a in args[2:]],
        out_specs=pl.BlockSpec((lc, di), lambda b, c: (b * nchunk + c, 0)),
        scratch_shapes=[pltpu.VMEM((lc + 2 * SUBLANES, SSD_CONV_DIM), F32),
                        pltpu.VMEM((SSD_STATE, di), F32),
                        pltpu.VMEM((lc, di), F32)],
        compiler_params=_params(2),
        name="ssd_core",
    )(*args)
    return _proj_res(x2, yz, w_out)


def _conf_body(x_ref, g_ref, w1_ref, b1_ref, dw_ref, dwb_ref, lnw_ref, lnb_ref, w2_ref, b2_ref, o_ref,
               ubuf_ref, c_s, *, tiles_per_seq, rb, lanes):
    tm = x_ref.shape[0]
    d = D_MODEL
    hl = CONF_HALO

    @pl.when(pl.program_id(0) % tiles_per_seq == 0)
    def _():
        ubuf_ref[tm:tm + hl, :] = jnp.zeros((hl, d), F32)

    x = x_ref[...]
    h = _rms(x, g_ref[...]).astype(BF16)
    p = jnp.dot(h, w1_ref[...], preferred_element_type=F32) + b1_ref[...]
    u = p[:, 0:d] * jax.nn.sigmoid(p[:, d:2 * d])
    ubuf_ref[0:hl, :] = ubuf_ref[tm:tm + hl, :]
    ubuf_ref[hl:hl + tm, :] = u
    base = hl - (CONF_KERNEL - 1)
    for r0 in range(0, tm, rb):
        for c0 in range(0, d, lanes):
            acc = jnp.broadcast_to(dwb_ref[:, c0:c0 + lanes], (rb, lanes))
            win = ubuf_ref[r0:r0 + rb + hl, c0:c0 + lanes]
            for res in range(SUBLANES):
                offs = [o for o in range(base, base + CONF_KERNEL) if o % SUBLANES == res]
                slab = win if res == 0 else pltpu.roll(win, rb + hl - res, 0)
                for o in offs:
                    a0 = o - res
                    acc = acc + dw_ref[o - base:o - base + 1, c0:c0 + lanes] * slab[a0:a0 + rb, :]
            c_s[r0:r0 + rb, c0:c0 + lanes] = acc
    y = c_s[...]
    mu = jnp.mean(y, axis=-1, keepdims=True)
    dy = y - mu
    var = jnp.mean(dy * dy, axis=-1, keepdims=True)
    zn = dy * lax.rsqrt(var + CONF_LN_EPS) * lnw_ref[...] + lnb_ref[...]
    o_ref[...] = x + _dot(jax.nn.silu(zn), w2_ref[...]) + b2_ref[...]


def _conformer(x2, seq, g, w_pw1, b_pw1, dw_w, dw_b, ln_w, ln_b, w_pw2, b_pw2, tm=256):
    rows = x2.shape[0]
    d = D_MODEL
    args = (x2, _row(g), w_pw1.astype(BF16), _row(b_pw1), dw_w.astype(F32), _row(dw_b), _row(ln_w),
            _row(ln_b), w_pw2.astype(BF16), _row(b_pw2))
    return pl.pallas_call(
        functools.partial(_conf_body, tiles_per_seq=seq // tm, rb=64, lanes=128),
        out_shape=jax.ShapeDtypeStruct((rows, d), F32),
        grid=(rows // tm,),
        in_specs=[pl.BlockSpec((tm, d), lambda i: (i, 0))] + [_const_spec(a.shape) for a in args[1:]],
        out_specs=pl.BlockSpec((tm, d), lambda i: (i, 0)),
        scratch_shapes=[pltpu.VMEM((tm + CONF_HALO, d), F32), pltpu.VMEM((tm, d), F32)],
        compiler_params=_params(1),
        name="conformer",
    )(*args)


def kernel(x, norm_mix, norm_ffn, norm_final, ffn_w_up, ffn_conv_w, ffn_conv_b, ffn_w_down, moba_w_qkv, moba_w_o, rwkv_mu, rwkv_w_rkv, rwkv_w0, rwkv_w1, rwkv_w2, rwkv_a0, rwkv_a1, rwkv_a2, rwkv_g1, rwkv_g2, rwkv_k_k, rwkv_k_a, rwkv_r_k, rwkv_gn_w, rwkv_gn_b, rwkv_w_o, ssd_w_in, ssd_conv_w, ssd_conv_b, ssd_dt_bias, ssd_a_log, ssd_d, ssd_norm_w, ssd_w_out, conf_w_pw1, conf_b_pw1, conf_dw_w, conf_dw_b, conf_ln_w, conf_ln_b, conf_w_pw2, conf_b_pw2):
    b, s, d = x.shape
    depth = norm_mix.shape[0]
    rows = b * s
    for i in range(depth):
        kind, j = i % 4, i // 4
        if kind == 0:
            x2 = _moba(x.reshape(b, s, d), norm_mix[i], moba_w_qkv[j], moba_w_o[j]).reshape(rows, d)
        elif kind == 1:
            x2 = _rwkv7(x.reshape(rows, d), s, norm_mix[i], rwkv_mu[j], rwkv_w_rkv[j], rwkv_w0[j], rwkv_w1[j],
                        rwkv_w2[j], rwkv_a0[j], rwkv_a1[j], rwkv_a2[j], rwkv_g1[j], rwkv_g2[j], rwkv_k_k[j],
                        rwkv_k_a[j], rwkv_r_k[j], rwkv_gn_w[j], rwkv_gn_b[j], rwkv_w_o[j])
        elif kind == 2:
            x2 = _mamba2(x.reshape(rows, d), b, s, norm_mix[i], ssd_w_in[j], ssd_conv_w[j], ssd_conv_b[j],
                         ssd_dt_bias[j], ssd_a_log[j], ssd_d[j], ssd_norm_w[j], ssd_w_out[j])
        else:
            x2 = _conformer(x.reshape(rows, d), s, norm_mix[i], conf_w_pw1[j], conf_b_pw1[j], conf_dw_w[j],
                            conf_dw_b[j], conf_ln_w[j], conf_ln_b[j], conf_w_pw2[j], conf_b_pw2[j])
        x = _conv_ffn(x2, s, norm_ffn[i], ffn_w_up[i], ffn_conv_w[i], ffn_conv_b[i], ffn_w_down[i],
                      norm_final, final=(i == depth - 1))
    return x.reshape(b, s, d)
```

```python
import functools

import jax
import jax.numpy as jnp
from jax import lax
from jax.experimental import pallas as pl
from jax.experimental.pallas import tpu as pltpu

F32 = jnp.float32
BF16 = jnp.bfloat16

D_MODEL = 1024
NORM_EPS = 1e-6
NEG_INF = -1e30
LOG2E = 1.4426950408889634

MOBA_HEADS = 16
MOBA_HEAD_DIM = 64
MOBA_BLOCK = 256
MOBA_TOPK = 3
MOBA_ONES_ROWS = 16
MOBA_HEAD_GROUP = 8

RWKV_HEADS = 16
RWKV_HEAD_DIM = 64
RWKV_GN_EPS = 64e-5
RWKV_CHUNK = 64
RWKV_HEAD_GROUP = 16

SSD_D_INNER = 2048
SSD_HEAD_DIM = 64
SSD_HEADS = 32
SSD_GROUPS = 4
SSD_STATE = 128
SSD_CONV = 4
SSD_CHUNK = 128
SSD_CONV_DIM = SSD_D_INNER + 2 * SSD_GROUPS * SSD_STATE
SSD_NORM_EPS = 1e-5

CONF_KERNEL = 31
CONF_LN_EPS = 1e-5
CONF_HALO = 32

D_FF = 2816
FFN_CONV = 3
FFN_CHUNK = 256

SUBLANES = 8
VMEM_LIMIT = 56 * 1024 * 1024


def _rms(x, g, eps=NORM_EPS):
    return x * lax.rsqrt(jnp.mean(x * x, axis=-1, keepdims=True) + eps) * g


def _dot(a, b):
    return jnp.dot(a.astype(BF16), b.astype(BF16), preferred_element_type=F32)


def _dot_nt(a, b):
    return lax.dot_general(a.astype(BF16), b.astype(BF16), (((1,), (1,)), ((), ())),
                           preferred_element_type=F32)


def _dot_tn(a, b):
    return lax.dot_general(a.astype(BF16), b.astype(BF16), (((0,), (0,)), ((), ())),
                           preferred_element_type=F32)


def _split2(a):
    hi = a.astype(BF16)
    lo = (a - hi.astype(F32)).astype(BF16)
    return hi, lo


def _split3(a):
    hi = a.astype(BF16)
    r = a - hi.astype(F32)
    mid = r.astype(BF16)
    lo = (r - mid.astype(F32)).astype(BF16)
    return hi, mid, lo


def _cumsum_rows(tril3, b):
    return jnp.dot(tril3, jnp.concatenate(_split3(b), axis=0), preferred_element_type=F32)


def _dot_hilo(a, b2_bf16):
    return jnp.dot(jnp.concatenate(_split2(a), axis=1), b2_bf16, preferred_element_type=F32)


def _const_spec(shape):
    n = len(shape)
    return pl.BlockSpec(shape, lambda *_: (0,) * n, pipeline_mode=pl.Buffered(1))


def _params(n_axes):
    return pltpu.CompilerParams(dimension_semantics=("arbitrary",) * n_axes,
                                vmem_limit_bytes=VMEM_LIMIT)


def _row(v):
    return v.reshape(1, -1).astype(F32)


def _ffn_body(x_ref, g_ref, wup_ref, cw_ref, cb_ref, wdn_ref, gf_ref, o_ref, halo_ref, act_ref,
              *, tiles_per_seq, final):
    tm = x_ref.shape[0]
    fc = FFN_CHUNK
    hs = SUBLANES

    @pl.when(pl.program_id(0) % tiles_per_seq == 0)
    def _():
        halo_ref[...] = jnp.zeros_like(halo_ref)

    x = x_ref[...]
    h = _rms(x, g_ref[...]).astype(BF16)

    r8 = lax.broadcasted_iota(jnp.int32, (hs, fc), 0)

    def conv(u, off):
        prev = halo_ref[:, off:off + fc]
        halo_ref[:, off:off + fc] = u[tm - hs:tm, :]
        w0 = cw_ref[0:1, off:off + fc]
        w1 = cw_ref[1:2, off:off + fc]
        u1 = pltpu.roll(u, 1, 0)
        u2 = pltpu.roll(u, 2, 0)
        y = w0 * u2 + w1 * u1 + cw_ref[2:3, off:off + fc] * u + cb_ref[0:1, off:off + fc]
        fix = (w0 * jnp.where(r8 < 2, pltpu.roll(prev, 2, 0) - u2[0:hs, :], 0.0)
               + w1 * jnp.where(r8 < 1, pltpu.roll(prev, 1, 0) - u1[0:hs, :], 0.0))
        return jnp.concatenate([y[0:hs, :] + fix, y[hs:, :]], axis=0)

    def up(c):
        og, ou = c * fc, D_FF + c * fc
        return (jnp.dot(h, wup_ref[:, og:og + fc], preferred_element_type=F32),
                jnp.dot(h, wup_ref[:, ou:ou + fc], preferred_element_type=F32))

    n_chunks = D_FF // fc
    nxt = up(0)
    for c in range(n_chunks):
        ug, uu = nxt
        if c + 1 < n_chunks:
            nxt = up(c + 1)
        yg = conv(ug, c * fc)
        yu = conv(uu, D_FF + c * fc)
        act_ref[:, c * fc:(c + 1) * fc] = (jax.nn.silu(yg) * yu).astype(BF16)
    y = x + jnp.dot(act_ref[...], wdn_ref[...], preferred_element_type=F32)
    if final:
        y = _rms(y, gf_ref[...])
    o_ref[...] = y


def _conv_ffn(x2, seq, g, w_up, conv_w, conv_b, w_down, g_final, final, tm=512):
    rows = x2.shape[0]
    body = functools.partial(_ffn_body, tiles_per_seq=seq // tm, final=final)
    return pl.pallas_call(
        body,
        out_shape=jax.ShapeDtypeStruct((rows, D_MODEL), F32),
        grid=(rows // tm,),
        in_specs=[
            pl.BlockSpec((tm, D_MODEL), lambda i: (i, 0)),
            _const_spec((1, D_MODEL)),
            _const_spec((D_MODEL, 2 * D_FF)),
            _const_spec((FFN_CONV, 2 * D_FF)),
            _const_spec((1, 2 * D_FF)),
            _const_spec((D_FF, D_MODEL)),
            _const_spec((1, D_MODEL)),
        ],
        out_specs=pl.BlockSpec((tm, D_MODEL), lambda i: (i, 0)),
        scratch_shapes=[pltpu.VMEM((SUBLANES, 2 * D_FF), F32), pltpu.VMEM((tm, D_FF), BF16)],
        compiler_params=_params(1),
        name="conv_ffn",
    )(x2, _row(g), w_up.astype(BF16), conv_w.astype(F32), _row(conv_b), w_down.astype(BF16), _row(g_final))


def _proj_res_body(x_ref, y_ref, w_ref, o_ref):
    o_ref[...] = x_ref[...] + jnp.dot(y_ref[...].astype(BF16), w_ref[...], preferred_element_type=F32)


def _proj_res(x2, y2, w, tm=512):
    rows, k = y2.shape
    return pl.pallas_call(
        _proj_res_body,
        out_shape=jax.ShapeDtypeStruct((rows, D_MODEL), F32),
        grid=(rows // tm,),
        in_specs=[pl.BlockSpec((tm, D_MODEL), lambda i: (i, 0)),
                  pl.BlockSpec((tm, k), lambda i: (i, 0)),
                  _const_spec((k, D_MODEL))],
        out_specs=pl.BlockSpec((tm, D_MODEL), lambda i: (i, 0)),
        compiler_params=_params(1),
        name="proj_res",
    )(x2, y2, w.astype(BF16))


def _moba_qkv_body(x_ref, g_ref, wt_ref, o_ref, *, nc):
    h = _rms(x_ref[0], g_ref[...]).astype(BF16)
    for n in range(3 * D_MODEL // nc):
        o_ref[0, n * nc:(n + 1) * nc, :] = lax.dot_general(
            wt_ref[n * nc:(n + 1) * nc, :], h, (((1,), (1,)), ((), ())),
            preferred_element_type=F32).astype(BF16)


def _moba_attn_body(q_ref, k_ref, v_ref, o_ref, kaug_ref, vblk_ref, vpair_ref, kmean_ref, *, nb):
    bs = MOBA_BLOCK
    dh = MOBA_HEAD_DIM
    hg = MOBA_HEAD_GROUP
    hh = range(hg)
    iq = pl.program_id(2)

    @pl.when(iq == 0)
    def _():
        blk = lax.broadcasted_iota(jnp.int32, (bs, nb), 1)
        for j in range(nb):
            kr = k_ref[0, :, j * bs:(j + 1) * bs].astype(F32).T
            onehot = jnp.where(blk == j, 1.0, 0.0).astype(BF16)
            for i in hh:
                kri = kr[:, i * dh:(i + 1) * dh]
                kaug_ref[i, j // 2, (j % 2) * bs:(j % 2 + 1) * bs, :] = jnp.concatenate(
                    [kri.astype(BF16), onehot], axis=1)
                kmean_ref[i, j:j + 1, :] = jnp.mean(kri, axis=0, keepdims=True)
                vblk_ref[i, j, 0:dh, :] = v_ref[0, i * dh:(i + 1) * dh, j * bs:(j + 1) * bs]
                vblk_ref[i, j, dh:dh + MOBA_ONES_ROWS, :] = jnp.ones((MOBA_ONES_ROWS, bs), BF16)
                if j % 2 == 0:
                    vpair_ref[i, j // 2, 0:dh, :] = v_ref[0, i * dh:(i + 1) * dh, j * bs:(j + 2) * bs]
                    vpair_ref[i, j // 2, dh:dh + MOBA_ONES_ROWS, :] = jnp.ones((MOBA_ONES_ROWS, 2 * bs), BF16)

    jidx = lax.broadcasted_iota(jnp.int32, (nb, bs), 0)
    qs, qaug = [], []
    for i in hh:
        q_raw = q_ref[0, i * dh:(i + 1) * dh, :]
        km_hi, km_lo = _split2(kmean_ref[i])
        gate = (jnp.dot(km_hi, q_raw, preferred_element_type=F32)
                + jnp.dot(km_lo, q_raw, preferred_element_type=F32))
        cnt = jnp.zeros((nb, bs), jnp.int32)
        for jp in range(nb):
            gj = gate[jp:jp + 1, :]
            beats = (gj > gate) | ((gj == gate) & (jp < jidx))
            cnt = cnt + jnp.where(beats & (jp < iq), 1, 0)
        selm = (cnt < MOBA_TOPK) & (jidx < iq)
        qs.append(q_raw * (dh ** -0.5 * LOG2E))
        qaug.append(jnp.concatenate([qs[i], jnp.where(selm, 0.0, NEG_INF).astype(BF16)], axis=0))

    kpos = lax.broadcasted_iota(jnp.int32, (bs, bs), 0)
    qpos = lax.broadcasted_iota(jnp.int32, (bs, bs), 1)
    causal = kpos <= qpos
    own_rows = pl.ds(pl.multiple_of((iq % 2) * bs, bs), bs)
    s = [jnp.where(causal, jnp.dot(kaug_ref[i, iq // 2, own_rows, :][:, 0:dh], qs[i],
                                   preferred_element_type=F32), NEG_INF)
         for i in hh]
    m0 = [jnp.max(s[i], axis=0, keepdims=True) for i in hh]
    p = [jnp.exp2(s[i] - m0[i]) for i in hh]
    acc0 = [jnp.dot(vblk_ref[i, iq], p[i].astype(BF16), preferred_element_type=F32) for i in hh]

    def step(j, carry):
        m, acc = carry
        sj = [jnp.dot(kaug_ref[i, j], qaug[i], preferred_element_type=F32) for i in hh]
        m_new, acc_new = [], []
        for i in hh:
            mi = jnp.maximum(m[i], jnp.max(sj[i], axis=0, keepdims=True))
            pj = jnp.exp2(sj[i] - mi).astype(BF16)
            acc_new.append(jnp.exp2(m[i] - mi) * acc[i]
                           + jnp.dot(vpair_ref[i, j], pj, preferred_element_type=F32))
            m_new.append(mi)
        return m_new, acc_new

    _, acc = lax.fori_loop(0, (iq + 1) // 2, step, (m0, acc0))
    for i in hh:
        o_ref[0, i * dh:(i + 1) * dh, :] = (acc[i][0:dh, :] / acc[i][dh:dh + 1, :]).astype(BF16)


def _moba_out_body(x_ref, o_ref_in, w_ref, o_ref):
    o_ref[0] = x_ref[0] + lax.dot_general(o_ref_in[0], w_ref[...], (((0,), (0,)), ((), ())),
                                          preferred_element_type=F32)


def _moba(x, g, w_qkv, w_o, ts=512):
    b, s, d = x.shape
    nb = s // MOBA_BLOCK
    dh = MOBA_HEAD_DIM
    qkvt = pl.pallas_call(
        functools.partial(_moba_qkv_body, nc=512),
        out_shape=jax.ShapeDtypeStruct((b, 3 * d, s), BF16),
        grid=(b, s // ts),
        in_specs=[pl.BlockSpec((1, ts, d), lambda i, j: (i, j, 0)),
                  _const_spec((1, d)),
                  _const_spec((3 * d, d))],
        out_specs=pl.BlockSpec((1, 3 * d, ts), lambda i, j: (i, 0, j)),
        compiler_params=_params(2),
        name="moba_qkv",
    )(x, _row(g), w_qkv.T.astype(BF16))

    hg = MOBA_HEAD_GROUP
    ng = MOBA_HEADS // hg
    gd = hg * dh
    ot = pl.pallas_call(
        functools.partial(_moba_attn_body, nb=nb),
        out_shape=jax.ShapeDtypeStruct((b, d, s), BF16),
        grid=(b, ng, nb),
        in_specs=[pl.BlockSpec((1, gd, MOBA_BLOCK), lambda i, j, k: (i, j, k)),
                  pl.BlockSpec((1, gd, s), lambda i, j, k: (i, ng + j, 0)),
                  pl.BlockSpec((1, gd, s), lambda i, j, k: (i, 2 * ng + j, 0))],
        out_specs=pl.BlockSpec((1, gd, MOBA_BLOCK), lambda i, j, k: (i, j, k)),
        scratch_shapes=[pltpu.VMEM((hg, nb // 2, 2 * MOBA_BLOCK, dh + nb), BF16),
                        pltpu.VMEM((hg, nb, dh + MOBA_ONES_ROWS, MOBA_BLOCK), BF16),
                        pltpu.VMEM((hg, nb // 2, dh + MOBA_ONES_ROWS, 2 * MOBA_BLOCK), BF16),
                        pltpu.VMEM((hg, nb, dh), F32)],
        compiler_params=_params(3),
        name="moba_attn",
    )(qkvt, qkvt, qkvt)

    return pl.pallas_call(
        _moba_out_body,
        out_shape=jax.ShapeDtypeStruct((b, s, d), F32),
        grid=(b, s // ts),
        in_specs=[pl.BlockSpec((1, ts, d), lambda i, j: (i, j, 0)),
                  pl.BlockSpec((1, d, ts), lambda i, j: (i, 0, j)),
                  _const_spec((d, d))],
        out_specs=pl.BlockSpec((1, ts, d), lambda i, j: (i, j, 0)),
        compiler_params=_params(2),
        name="moba_out",
    )(x, ot, w_o.astype(BF16))


def _rwkv_body(x_ref, g_ref, mu_ref, wr_ref, wk_ref, wv_ref, w0_ref, w1_ref, w2_ref, a0_ref, a1_ref,
               a2_ref, g1_ref, g2_ref, kk_ref, ka_ref, rk_ref, gnw_ref, gnb_ref, wo_ref, p_ref, pt_ref,
               o_ref,
               hbuf_ref, st_ref, r_s, lw_s, k_s, v_s, a_s, b_s, y_s, gate_s, bonus_s,
               ch0, t0, mv0, mr0, gt0, ch1, t1, mv1, mr1, gt1, *, tiles_per_seq):
    bufs = ((ch0, t0, mv0, mr0, gt0), (ch1, t1, mv1, mr1, gt1))
    tm = x_ref.shape[0]
    lc = RWKV_CHUNK
    hd = RWKV_HEAD_DIM
    hs = SUBLANES

    @pl.when(pl.program_id(0) % tiles_per_seq == 0)
    def _():
        hbuf_ref[tm:tm + hs, :] = jnp.zeros((hs, D_MODEL), F32)
        st_ref[...] = jnp.zeros_like(st_ref)

    def seg_sum(v):
        return jnp.dot(v.astype(BF16), p_ref[...], preferred_element_type=F32)

    def seg_bcast(v):
        return _dot_hilo(v, pt_ref[...])

    x = x_ref[...]
    hn = _rms(x, g_ref[...])
    hbuf_ref[0:hs, :] = hbuf_ref[tm:tm + hs, :]
    hbuf_ref[hs:hs + tm, :] = hn
    xx = hbuf_ref[hs - 1:hs - 1 + tm, :] - hn

    def mix(i):
        return hn + xx * mu_ref[i:i + 1, :]

    r = _dot(mix(0), wr_ref[...])
    k = _dot(mix(1), wk_ref[...])
    v = _dot(mix(2), wv_ref[...])
    wpre = w0_ref[...] + _dot(jnp.tanh(_dot(mix(3), w1_ref[...])), w2_ref[...])
    lw = -jnp.exp(-jax.nn.softplus(-wpre) - 0.5)
    a = jax.nn.sigmoid(a0_ref[...] + _dot(_dot(mix(4), a1_ref[...]), a2_ref[...]))
    gate_s[...] = _dot(jax.nn.sigmoid(_dot(mix(5), g1_ref[...])), g2_ref[...])

    kk = k * kk_ref[...]
    inv = 1.0 / jnp.maximum(jnp.sqrt(seg_sum(kk * kk)), 1e-12)
    kk = kk * seg_bcast(inv)
    k = k * (1.0 + (a - 1.0) * ka_ref[...])
    bonus_s[...] = seg_bcast(seg_sum(r * k * rk_ref[...])) * v
    r_s[...] = r
    lw_s[...] = lw
    k_s[...] = k
    v_s[...] = v
    a_s[...] = -kk
    b_s[...] = kk * a

    ti = lax.broadcasted_iota(jnp.int32, (lc, lc), 0)
    si = lax.broadcasted_iota(jnp.int32, (lc, lc), 1)
    low_strict = si < ti
    tril = jnp.where(si <= ti, 1.0, 0.0).astype(BF16)
    tril3 = jnp.concatenate([tril, tril, tril], axis=1)
    ti2 = lax.broadcasted_iota(jnp.int32, (lc, 2 * lc), 0)
    si2 = lax.broadcasted_iota(jnp.int32, (lc, 2 * lc), 1)
    low_incl2 = jnp.where(si2 >= lc, si2 - lc, si2) <= ti2
    keep_u = si2 >= lc
    low_strict2 = si2 < ti2

    hh = range(RWKV_HEADS)
    sls = [slice(h * hd, (h + 1) * hd) for h in hh]
    eye2 = jnp.where(si2 - lc == ti2, 1.0, 0.0)

    def prepare(c):
        ch, t_s, mv_s, mr_s, gt_s = bufs[c % 2]
        rows = slice(c * lc, (c + 1) * lc)
        lw_c = lw_s[rows, :]
        lg = _cumsum_rows(tril3, lw_c)
        lg_last = lg[lc - 1:lc, :]
        g_inv = jnp.exp(-lg)
        g_tail = jnp.exp(lg_last - lg)
        gt_s[...] = jnp.broadcast_to(jnp.exp(lg_last), gt_s.shape)
        ch[0, 0:lc, :] = a_s[rows, :] * jnp.exp(lg - lw_c)
        ch[0, lc:2 * lc, :] = r_s[rows, :] * jnp.exp(lg)
        ch[1, 0:lc, :] = b_s[rows, :] * g_inv
        ch[1, lc:2 * lc, :] = k_s[rows, :] * g_inv
        ch[2, 0:lc, :] = b_s[rows, :] * g_tail
        ch[2, lc:2 * lc, :] = k_s[rows, :] * g_tail
        gm = [_dot_nt(ch[0, :, sls[h]], ch[1, :, sls[h]]) for h in hh]
        for h in hh:
            mr_s[h] = jnp.where(low_incl2, gm[h][lc:2 * lc, :], 0.0).astype(BF16)
            mv_s[h] = _dot(jnp.where(low_strict, gm[h][0:lc, lc:2 * lc], 0.0), v_s[rows, sls[h]])
        w = [jnp.where(keep_u, eye2, jnp.where(low_strict2, gm[h][0:lc, :], 0.0)) for h in hh]
        steps = 1
        while steps < lc:
            r = [_dot(w[h][:, 0:lc], w[h]) for h in hh]
            w = [r[h] + jnp.where(keep_u, w[h], 0.0) for h in hh]
            steps *= 2
        for h in hh:
            t_s[h] = w[h][:, lc:].astype(BF16)

    def advance(c):
        ch, t_s, mv_s, mr_s, gt_s = bufs[c % 2]
        rows = slice(c * lc, (c + 1) * lc)
        sh = [st_ref[h] for h in hh]
        ars = [_dot_nt(ch[0, :, sls[h]], sh[h]) for h in hh]
        u = [jnp.dot(t_s[h], (ars[h][0:lc, :] + mv_s[h]).astype(BF16), preferred_element_type=F32)
             for h in hh]
        uv = [jnp.concatenate([u[h].astype(BF16), v_s[rows, sls[h]].astype(BF16)], axis=0) for h in hh]
        for h in hh:
            y_s[rows, sls[h]] = ars[h][lc:2 * lc, :] + jnp.dot(mr_s[h], uv[h], preferred_element_type=F32)
        for h in hh:
            st_ref[h] = sh[h] * gt_s[0:1, sls[h]] + _dot_tn(uv[h], ch[2, :, sls[h]])

    n_chunks = tm // lc
    prepare(0)
    for c in range(n_chunks):
        if c + 1 < n_chunks:
            prepare(c + 1)
        advance(c)

    y = y_s[...]
    mean = seg_bcast(seg_sum(y)) * (1.0 / hd)
    dy = y - mean
    var = seg_bcast(seg_sum(dy * dy)) * (1.0 / hd)
    yn = dy * lax.rsqrt(var + RWKV_GN_EPS) * gnw_ref[...] + gnb_ref[...]
    o_ref[...] = x + _dot((yn + bonus_s[...]) * gate_s[...], wo_ref[...])


def _rwkv7(x2, seq, g, mu, w_rkv, w0, w1, w2, a0, a1, a2, g1, g2, k_k, k_a, r_k, gn_w, gn_b, w_o, tm=256):
    rows = x2.shape[0]
    d = D_MODEL
    head_of = jnp.arange(d) // RWKV_HEAD_DIM
    p = (head_of[:, None] == jnp.arange(128)[None, :]).astype(BF16)
    args = (x2, _row(g), mu.astype(F32), w_rkv[0].astype(BF16), w_rkv[1].astype(BF16), w_rkv[2].astype(BF16),
            _row(w0), w1.astype(BF16), w2.astype(BF16), _row(a0), a1.astype(BF16), a2.astype(BF16),
            g1.astype(BF16), g2.astype(BF16), _row(k_k), _row(k_a), _row(r_k), _row(gn_w), _row(gn_b),
            w_o.astype(BF16), p, jnp.concatenate([p.T, p.T], axis=0))
    in_specs = [pl.BlockSpec((tm, d), lambda i: (i, 0))] + [_const_spec(a.shape) for a in args[1:]]
    big = pltpu.VMEM((tm, d), F32)
    lc = RWKV_CHUNK
    return pl.pallas_call(
        functools.partial(_rwkv_body, tiles_per_seq=seq // tm),
        out_shape=jax.ShapeDtypeStruct((rows, d), F32),
        grid=(rows // tm,),
        in_specs=in_specs,
        out_specs=pl.BlockSpec((tm, d), lambda i: (i, 0)),
        scratch_shapes=[pltpu.VMEM((tm + 2 * SUBLANES, d), F32),
                        pltpu.VMEM((RWKV_HEADS, RWKV_HEAD_DIM, RWKV_HEAD_DIM), F32),
                        big, big, big, big, big, big, big, big, big,
                        ] + 2 * [pltpu.VMEM((3, 2 * lc, d), F32),
                                 pltpu.VMEM((RWKV_HEADS, lc, lc), BF16),
                                 pltpu.VMEM((RWKV_HEADS, lc, RWKV_HEAD_DIM), F32),
                                 pltpu.VMEM((RWKV_HEADS, lc, 2 * lc), BF16),
                                 pltpu.VMEM((SUBLANES, d), F32)],
        compiler_params=_params(1),
        name="rwkv7",
    )(*args)


def _ssd_in_body(x_ref, g_ref, w_ref, wdt_ref, zx_ref, dt_ref, *, nc):
    hn = _rms(x_ref[...], g_ref[...])
    h = hn.astype(BF16)
    for n in range(w_ref.shape[1] // nc):
        zx_ref[:, n * nc:(n + 1) * nc] = jnp.dot(h, w_ref[:, n * nc:(n + 1) * nc],
                                                 preferred_element_type=F32)
    h_lo = (hn - h.astype(F32)).astype(BF16)
    w_hi, w_lo = _split2(wdt_ref[...])
    dt_ref[...] = (jnp.dot(h, w_hi, preferred_element_type=F32)
                   + jnp.dot(h, w_lo, preferred_element_type=F32)
                   + jnp.dot(h_lo, w_hi, preferred_element_type=F32))


def _ssd_core_body(zx_ref, dt_ref, cw_ref, cb_ref, dtb_ref, alog_ref, dskip_ref, nw_ref, e_ref, o_ref,
                   cbuf_ref, state_ref, y_s):
    lc = SSD_CHUNK
    di = SSD_D_INNER
    hs = SUBLANES
    gw = di // SSD_GROUPS
    ns = SSD_STATE

    @pl.when(pl.program_id(1) == 0)
    def _():
        cbuf_ref[lc:lc + hs, :] = jnp.zeros((hs, SSD_CONV_DIM), F32)
        state_ref[...] = jnp.zeros_like(state_ref)

    z = zx_ref[:, 0:di]
    xbc_raw = zx_ref[:, di:di + SSD_CONV_DIM]
    cbuf_ref[0:hs, :] = cbuf_ref[lc:lc + hs, :]
    cbuf_ref[hs:hs + lc, :] = xbc_raw
    xbc = cb_ref[...] + cw_ref[SSD_CONV - 1:SSD_CONV, :] * xbc_raw
    win = cbuf_ref[0:hs + lc, :]
    for j in range(SSD_CONV - 1):
        back = SSD_CONV - 1 - j
        xbc = xbc + cw_ref[j:j + 1, :] * pltpu.roll(win, back, 0)[hs:hs + lc, :]
    xbc = jax.nn.silu(xbc)
    xs = xbc[:, 0:di]
    bm = xbc[:, di:di + SSD_GROUPS * ns]
    cm = xbc[:, di + SSD_GROUPS * ns:]

    dt = jax.nn.softplus(dt_ref[...] + dtb_ref[...])
    adt = dt * (-jnp.exp(alog_ref[...]))
    li = lax.broadcasted_iota(jnp.int32, (lc, lc), 0)
    si = lax.broadcasted_iota(jnp.int32, (lc, lc), 1)
    causal = si <= li
    tril = jnp.where(causal, 1.0, 0.0).astype(BF16)
    acs = _cumsum_rows(jnp.concatenate([tril, tril, tril], axis=1), adt)
    acs_t = acs.T
    a_last = acs[lc - 1:lc, :]

    def expand(v):
        return _dot_hilo(v, e_ref[...])

    exp_acs_x = expand(jnp.exp(acs))
    xd = xs * expand(dt)
    xdec = xd * expand(jnp.exp(a_last - acs))

    for g in range(SSD_GROUPS):
        bg = bm[:, g * ns:(g + 1) * ns]
        cg = cm[:, g * ns:(g + 1) * ns]
        cb = _dot_nt(cg, bg)
        gsl = slice(g * gw, (g + 1) * gw)
        st_g = state_ref[:, gsl]
        y_off = _dot(cg, st_g) * exp_acs_x[:, gsl]
        for e in range(SSD_HEADS // SSD_GROUPS):
            h = g * (SSD_HEADS // SSD_GROUPS) + e
            col = jnp.broadcast_to(acs[:, h:h + 1], (lc, lc))
            row = jnp.broadcast_to(acs_t[h:h + 1, :], (lc, lc))
            lm = jnp.where(causal, jnp.exp(col - row), 0.0)
            hsl = slice(h * SSD_HEAD_DIM, (h + 1) * SSD_HEAD_DIM)
            y_s[:, hsl] = _dot(cb * lm, xd[:, hsl])
        state_ref[:, gsl] = st_g * exp_acs_x[lc - 1:lc, gsl] + _dot_tn(bg, xdec[:, gsl])
        y_s[:, gsl] = y_s[:, gsl] + y_off

    y = y_s[...] + xs * dskip_ref[...]
    yz = y * jax.nn.silu(z)
    for g in range(SSD_GROUPS):
        gsl = slice(g * gw, (g + 1) * gw)
        yg = yz[:, gsl]
        o_ref[:, gsl] = (yg * lax.rsqrt(jnp.mean(yg * yg, axis=-1, keepdims=True) + SSD_NORM_EPS)
                         * nw_ref[:, gsl])


def _mamba2(x2, batch, seq, g, w_in, conv_w, conv_b, dt_bias, a_log, d_skip, norm_w, w_out, tm=512):
    rows = x2.shape[0]
    d = D_MODEL
    di = SSD_D_INNER
    nzx = di + SSD_CONV_DIM
    zx, dt = pl.pallas_call(
        functools.partial(_ssd_in_body, nc=512),
        out_shape=(jax.ShapeDtypeStruct((rows, nzx), F32), jax.ShapeDtypeStruct((rows, SSD_HEADS), F32)),
        grid=(rows // tm,),
        in_specs=[pl.BlockSpec((tm, d), lambda i: (i, 0)),
                  _const_spec((1, d)),
                  _const_spec((d, nzx)),
                  _const_spec((d, SSD_HEADS))],
        out_specs=(pl.BlockSpec((tm, nzx), lambda i: (i, 0)),
                   pl.BlockSpec((tm, SSD_HEADS), lambda i: (i, 0))),
        compiler_params=_params(1),
        name="ssd_in",
    )(x2, _row(g), w_in[:, :nzx].astype(BF16), w_in[:, nzx:].astype(F32))

    lc = SSD_CHUNK
    nchunk = seq // lc
    expand = (jnp.arange(SSD_HEADS)[:, None] == (jnp.arange(di) // SSD_HEAD_DIM)[None, :]).astype(BF16)
    d_x = jnp.repeat(d_skip.astype(F32), SSD_HEAD_DIM).reshape(1, di)
    args = (zx, dt, conv_w.astype(F32), _row(conv_b), _row(dt_bias), _row(a_log), d_x, _row(norm_w),
            jnp.concatenate([expand, expand], axis=0))
    yz = pl.pallas_call(
        _ssd_core_body,
        out_shape=jax.ShapeDtypeStruct((rows, di), F32),
        grid=(batch, nchunk),
        in_specs=[pl.BlockSpec((lc, nzx), lambda b, c: (b * nchunk + c, 0)),
                  pl.BlockSpec((lc, SSD_HEADS), lambda b, c: (b * nchunk + c, 0))]
                 + [_const_spec(a.shape) for a in args[2:]],
        out_specs=pl.BlockSpec((lc, di), lambda b, c: (b * nchunk + c, 0)),
        scratch_shapes=[pltpu.VMEM((lc + 2 * SUBLANES, SSD_CONV_DIM), F32),
                        pltpu.VMEM((SSD_STATE, di), F32),
                        pltpu.VMEM((lc, di), F32)],
        compiler_params=_params(2),
        name="ssd_core",
    )(*args)
    return _proj_res(x2, yz, w_out)


def _conf_body(x_ref, g_ref, w1_ref, b1_ref, dw_ref, dwb_ref, lnw_ref, lnb_ref, w2_ref, b2_ref, o_ref,
               ubuf_ref, c_s, *, tiles_per_seq, rb, lanes):
    tm = x_ref.shape[0]
    d = D_MODEL
    hl = CONF_HALO

    @pl.when(pl.program_id(0) % tiles_per_seq == 0)
    def _():
        ubuf_ref[tm:tm + hl, :] = jnp.zeros((hl, d), F32)

    x = x_ref[...]
    h = _rms(x, g_ref[...]).astype(BF16)
    p = jnp.dot(h, w1_ref[...], preferred_element_type=F32) + b1_ref[...]
    u = p[:, 0:d] * jax.nn.sigmoid(p[:, d:2 * d])
    ubuf_ref[0:hl, :] = ubuf_ref[tm:tm + hl, :]
    ubuf_ref[hl:hl + tm, :] = u
    base = hl - (CONF_KERNEL - 1)
    for r0 in range(0, tm, rb):
        for c0 in range(0, d, lanes):
            acc = jnp.broadcast_to(dwb_ref[:, c0:c0 + lanes], (rb, lanes))
            win = ubuf_ref[r0:r0 + rb + hl, c0:c0 + lanes]
            for res in range(SUBLANES):
                offs = [o for o in range(base, base + CONF_KERNEL) if o % SUBLANES == res]
                slab = win if res == 0 else pltpu.roll(win, rb + hl - res, 0)
                for o in offs:
                    a0 = o - res
                    acc = acc + dw_ref[o - base:o - base + 1, c0:c0 + lanes] * slab[a0:a0 + rb, :]
            c_s[r0:r0 + rb, c0:c0 + lanes] = acc
    y = c_s[...]
    mu = jnp.mean(y, axis=-1, keepdims=True)
    dy = y - mu
    var = jnp.mean(dy * dy, axis=-1, keepdims=True)
    zn = dy * lax.rsqrt(var + CONF_LN_EPS) * lnw_ref[...] + lnb_ref[...]
    o_ref[...] = x + _dot(jax.nn.silu(zn), w2_ref[...]) + b2_ref[...]


def _conformer(x2, seq, g, w_pw1, b_pw1, dw_w, dw_b, ln_w, ln_b, w_pw2, b_pw2, tm=256):
    rows = x2.shape[0]
    d = D_MODEL
    args = (x2, _row(g), w_pw1.astype(BF16), _row(b_pw1), dw_w.astype(F32), _row(dw_b), _row(ln_w),
            _row(ln_b), w_pw2.astype(BF16), _row(b_pw2))
    return pl.pallas_call(
        functools.partial(_conf_body, tiles_per_seq=seq // tm, rb=64, lanes=128),
        out_shape=jax.ShapeDtypeStruct((rows, d), F32),
        grid=(rows // tm,),
        in_specs=[pl.BlockSpec((tm, d), lambda i: (i, 0))] + [_const_spec(a.shape) for a in args[1:]],
        out_specs=pl.BlockSpec((tm, d), lambda i: (i, 0)),
        scratch_shapes=[pltpu.VMEM((tm + CONF_HALO, d), F32), pltpu.VMEM((tm, d), F32)],
        compiler_params=_params(1),
        name="conformer",
    )(*args)


def kernel(x, norm_mix, norm_ffn, norm_final, ffn_w_up, ffn_conv_w, ffn_conv_b, ffn_w_down, moba_w_qkv, moba_w_o, rwkv_mu, rwkv_w_rkv, rwkv_w0, rwkv_w1, rwkv_w2, rwkv_a0, rwkv_a1, rwkv_a2, rwkv_g1, rwkv_g2, rwkv_k_k, rwkv_k_a, rwkv_r_k, rwkv_gn_w, rwkv_gn_b, rwkv_w_o, ssd_w_in, ssd_conv_w, ssd_conv_b, ssd_dt_bias, ssd_a_log, ssd_d, ssd_norm_w, ssd_w_out, conf_w_pw1, conf_b_pw1, conf_dw_w, conf_dw_b, conf_ln_w, conf_ln_b, conf_w_pw2, conf_b_pw2):
    b, s, d = x.shape
    depth = norm_mix.shape[0]
    rows = b * s
    for i in range(depth):
        kind, j = i % 4, i // 4
        if kind == 0:
            x2 = _moba(x.reshape(b, s, d), norm_mix[i], moba_w_qkv[j], moba_w_o[j]).reshape(rows, d)
        elif kind == 1:
            x2 = _rwkv7(x.reshape(rows, d), s, norm_mix[i], rwkv_mu[j], rwkv_w_rkv[j], rwkv_w0[j], rwkv_w1[j],
                        rwkv_w2[j], rwkv_a0[j], rwkv_a1[j], rwkv_a2[j], rwkv_g1[j], rwkv_g2[j], rwkv_k_k[j],
                        rwkv_k_a[j], rwkv_r_k[j], rwkv_gn_w[j], rwkv_gn_b[j], rwkv_w_o[j])
        elif kind == 2:
            x2 = _mamba2(x.reshape(rows, d), b, s, norm_mix[i], ssd_w_in[j], ssd_conv_w[j], ssd_conv_b[j],
                         ssd_dt_bias[j], ssd_a_log[j], ssd_d[j], ssd_norm_w[j], ssd_w_out[j])
        else:
            x2 = _conformer(x.reshape(rows, d), s, norm_mix[i], conf_w_pw1[j], conf_b_pw1[j], conf_dw_w[j],
                            conf_dw_b[j], conf_ln_w[j], conf_ln_b[j], conf_w_pw2[j], conf_b_pw2[j])
        x = _conv_ffn(x2, s, norm_ffn[i], ffn_w_up[i], ffn_conv_w[i], ffn_conv_b[i], ffn_w_down[i],
                      norm_final, final=(i == depth - 1))
    return x.reshape(b, s, d)
```

```python
import functools

import jax
import jax.numpy as jnp
from jax import lax
from jax.experimental import pallas as pl
from jax.experimental.pallas import tpu as pltpu

F32 = jnp.float32
BF16 = jnp.bfloat16

D_MODEL = 1024
NORM_EPS = 1e-6
NEG_INF = -1e30
LOG2E = 1.4426950408889634

MOBA_HEADS = 16
MOBA_HEAD_DIM = 64
MOBA_BLOCK = 256
MOBA_TOPK = 3
MOBA_ONES_ROWS = 16
MOBA_HEAD_GROUP = 8

RWKV_HEADS = 16
RWKV_HEAD_DIM = 64
RWKV_GN_EPS = 64e-5
RWKV_CHUNK = 64

SSD_D_INNER = 2048
SSD_HEAD_DIM = 64
SSD_HEADS = 32
SSD_GROUPS = 4
SSD_STATE = 128
SSD_CONV = 4
SSD_CHUNK = 128
SSD_CONV_DIM = SSD_D_INNER + 2 * SSD_GROUPS * SSD_STATE
SSD_NORM_EPS = 1e-5

CONF_KERNEL = 31
CONF_LN_EPS = 1e-5
CONF_HALO = 32

D_FF = 2816
FFN_CONV = 3
FFN_CHUNK = 256

SUBLANES = 8
VMEM_LIMIT = 56 * 1024 * 1024


def _rms(x, g, eps=NORM_EPS):
    return x * lax.rsqrt(jnp.mean(x * x, axis=-1, keepdims=True) + eps) * g


def _dot(a, b):
    return jnp.dot(a.astype(BF16), b.astype(BF16), preferred_element_type=F32)


def _dot_nt(a, b):
    return lax.dot_general(a.astype(BF16), b.astype(BF16), (((1,), (1,)), ((), ())),
                           preferred_element_type=F32)


def _dot_tn(a, b):
    return lax.dot_general(a.astype(BF16), b.astype(BF16), (((0,), (0,)), ((), ())),
                           preferred_element_type=F32)


def _split2(a):
    hi = a.astype(BF16)
    lo = (a - hi.astype(F32)).astype(BF16)
    return hi, lo


def _split3(a):
    hi = a.astype(BF16)
    r = a - hi.astype(F32)
    mid = r.astype(BF16)
    lo = (r - mid.astype(F32)).astype(BF16)
    return hi, mid, lo


def _cumsum_rows(tril3, b):
    return jnp.dot(tril3, jnp.concatenate(_split3(b), axis=0), preferred_element_type=F32)


def _dot_hilo(a, b2_bf16):
    return jnp.dot(jnp.concatenate(_split2(a), axis=1), b2_bf16, preferred_element_type=F32)


def _const_spec(shape):
    n = len(shape)
    return pl.BlockSpec(shape, lambda *_: (0,) * n, pipeline_mode=pl.Buffered(1))


def _params(n_axes):
    return pltpu.CompilerParams(dimension_semantics=("arbitrary",) * n_axes,
                                vmem_limit_bytes=VMEM_LIMIT)


def _row(v):
    return v.reshape(1, -1).astype(F32)


def _ffn_body(x_ref, g_ref, wup_ref, cw_ref, cb_ref, wdn_ref, gf_ref, o_ref, halo_ref, act_ref,
              *, tiles_per_seq, final):
    tm = x_ref.shape[0]
    fc = FFN_CHUNK
    hs = SUBLANES

    @pl.when(pl.program_id(0) % tiles_per_seq == 0)
    def _():
        halo_ref[...] = jnp.zeros_like(halo_ref)

    x = x_ref[...]
    h = _rms(x, g_ref[...]).astype(BF16)

    r8 = lax.broadcasted_iota(jnp.int32, (hs, fc), 0)

    def conv(u, off):
        prev = halo_ref[:, off:off + fc]
        halo_ref[:, off:off + fc] = u[tm - hs:tm, :]
        w0 = cw_ref[0:1, off:off + fc]
        w1 = cw_ref[1:2, off:off + fc]
        u1 = pltpu.roll(u, 1, 0)
        u2 = pltpu.roll(u, 2, 0)
        y = w0 * u2 + w1 * u1 + cw_ref[2:3, off:off + fc] * u + cb_ref[0:1, off:off + fc]
        fix = (w0 * jnp.where(r8 < 2, pltpu.roll(prev, 2, 0) - u2[0:hs, :], 0.0)
               + w1 * jnp.where(r8 < 1, pltpu.roll(prev, 1, 0) - u1[0:hs, :], 0.0))
        return jnp.concatenate([y[0:hs, :] + fix, y[hs:, :]], axis=0)

    def up(c):
        og, ou = c * fc, D_FF + c * fc
        return (jnp.dot(h, wup_ref[:, og:og + fc], preferred_element_type=F32),
                jnp.dot(h, wup_ref[:, ou:ou + fc], preferred_element_type=F32))

    n_chunks = D_FF // fc
    nxt = up(0)
    for c in range(n_chunks):
        ug, uu = nxt
        if c + 1 < n_chunks:
            nxt = up(c + 1)
        yg = conv(ug, c * fc)
        yu = conv(uu, D_FF + c * fc)
        act_ref[:, c * fc:(c + 1) * fc] = (jax.nn.silu(yg) * yu).astype(BF16)
    y = x + jnp.dot(act_ref[...], wdn_ref[...], preferred_element_type=F32)
    if final:
        y = _rms(y, gf_ref[...])
    o_ref[...] = y


def _conv_ffn(x2, seq, g, w_up, conv_w, conv_b, w_down, g_final, final, tm=512):
    rows = x2.shape[0]
    body = functools.partial(_ffn_body, tiles_per_seq=seq // tm, final=final)
    return pl.pallas_call(
        body,
        out_shape=jax.ShapeDtypeStruct((rows, D_MODEL), F32),
        grid=(rows // tm,),
        in_specs=[
            pl.BlockSpec((tm, D_MODEL), lambda i: (i, 0)),
            _const_spec((1, D_MODEL)),
            _const_spec((D_MODEL, 2 * D_FF)),
            _const_spec((FFN_CONV, 2 * D_FF)),
            _const_spec((1, 2 * D_FF)),
            _const_spec((D_FF, D_MODEL)),
            _const_spec((1, D_MODEL)),
        ],
        out_specs=pl.BlockSpec((tm, D_MODEL), lambda i: (i, 0)),
        scratch_shapes=[pltpu.VMEM((SUBLANES, 2 * D_FF), F32), pltpu.VMEM((tm, D_FF), BF16)],
        compiler_params=_params(1),
        name="conv_ffn",
    )(x2, _row(g), w_up.astype(BF16), conv_w.astype(F32), _row(conv_b), w_down.astype(BF16), _row(g_final))


def _proj_res_body(x_ref, y_ref, w_ref, o_ref):
    o_ref[...] = x_ref[...] + jnp.dot(y_ref[...].astype(BF16), w_ref[...], preferred_element_type=F32)


def _proj_res(x2, y2, w, tm=512):
    rows, k = y2.shape
    return pl.pallas_call(
        _proj_res_body,
        out_shape=jax.ShapeDtypeStruct((rows, D_MODEL), F32),
        grid=(rows // tm,),
        in_specs=[pl.BlockSpec((tm, D_MODEL), lambda i: (i, 0)),
                  pl.BlockSpec((tm, k), lambda i: (i, 0)),
                  _const_spec((k, D_MODEL))],
        out_specs=pl.BlockSpec((tm, D_MODEL), lambda i: (i, 0)),
        compiler_params=_params(1),
        name="proj_res",
    )(x2, y2, w.astype(BF16))


def _moba_qkv_body(x_ref, g_ref, wt_ref, o_ref, *, nc):
    h = _rms(x_ref[0], g_ref[...]).astype(BF16)
    for n in range(3 * D_MODEL // nc):
        o_ref[0, n * nc:(n + 1) * nc, :] = lax.dot_general(
            wt_ref[n * nc:(n + 1) * nc, :], h, (((1,), (1,)), ((), ())),
            preferred_element_type=F32).astype(BF16)


def _moba_attn_body(q_ref, k_ref, v_ref, o_ref, kaug_ref, vblk_ref, vpair_ref, kmean_ref, *, nb):
    bs = MOBA_BLOCK
    dh = MOBA_HEAD_DIM
    hg = MOBA_HEAD_GROUP
    hh = range(hg)
    iq = pl.program_id(2)

    @pl.when(iq == 0)
    def _():
        blk = lax.broadcasted_iota(jnp.int32, (bs, nb), 1)
        for j in range(nb):
            kr = k_ref[0, :, j * bs:(j + 1) * bs].astype(F32).T
            onehot = jnp.where(blk == j, 1.0, 0.0).astype(BF16)
            for i in hh:
                kri = kr[:, i * dh:(i + 1) * dh]
                kaug_ref[i, j // 2, (j % 2) * bs:(j % 2 + 1) * bs, :] = jnp.concatenate(
                    [kri.astype(BF16), onehot], axis=1)
                kmean_ref[i, j:j + 1, :] = jnp.mean(kri, axis=0, keepdims=True)
                vblk_ref[i, j, 0:dh, :] = v_ref[0, i * dh:(i + 1) * dh, j * bs:(j + 1) * bs]
                vblk_ref[i, j, dh:dh + MOBA_ONES_ROWS, :] = jnp.ones((MOBA_ONES_ROWS, bs), BF16)
                if j % 2 == 0:
                    vpair_ref[i, j // 2, 0:dh, :] = v_ref[0, i * dh:(i + 1) * dh, j * bs:(j + 2) * bs]
                    vpair_ref[i, j // 2, dh:dh + MOBA_ONES_ROWS, :] = jnp.ones((MOBA_ONES_ROWS, 2 * bs), BF16)

    jidx = lax.broadcasted_iota(jnp.int32, (nb, bs), 0)
    qs, qaug = [], []
    for i in hh:
        q_raw = q_ref[0, i * dh:(i + 1) * dh, :]
        km_hi, km_lo = _split2(kmean_ref[i])
        gate = (jnp.dot(km_hi, q_raw, preferred_element_type=F32)
                + jnp.dot(km_lo, q_raw, preferred_element_type=F32))
        cnt = jnp.zeros((nb, bs), jnp.int32)
        for jp in range(nb):
            gj = gate[jp:jp + 1, :]
            beats = (gj > gate) | ((gj == gate) & (jp < jidx))
            cnt = cnt + jnp.where(beats & (jp < iq), 1, 0)
        selm = (cnt < MOBA_TOPK) & (jidx < iq)
        qs.append(q_raw * (dh ** -0.5 * LOG2E))
        qaug.append(jnp.concatenate([qs[i], jnp.where(selm, 0.0, NEG_INF).astype(BF16)], axis=0))

    kpos = lax.broadcasted_iota(jnp.int32, (bs, bs), 0)
    qpos = lax.broadcasted_iota(jnp.int32, (bs, bs), 1)
    causal = kpos <= qpos
    own_rows = pl.ds(pl.multiple_of((iq % 2) * bs, bs), bs)
    s = [jnp.where(causal, jnp.dot(kaug_ref[i, iq // 2, own_rows, :][:, 0:dh], qs[i],
                                   preferred_element_type=F32), NEG_INF)
         for i in hh]
    m0 = [jnp.max(s[i], axis=0, keepdims=True) for i in hh]
    p = [jnp.exp2(s[i] - m0[i]) for i in hh]
    acc0 = [jnp.dot(vblk_ref[i, iq], p[i].astype(BF16), preferred_element_type=F32) for i in hh]

    def step(j, carry):
        m, acc = carry
        sj = [jnp.dot(kaug_ref[i, j], qaug[i], preferred_element_type=F32) for i in hh]
        m_new, acc_new = [], []
        for i in hh:
            mi = jnp.maximum(m[i], jnp.max(sj[i], axis=0, keepdims=True))
            pj = jnp.exp2(sj[i] - mi).astype(BF16)
            acc_new.append(jnp.exp2(m[i] - mi) * acc[i]
                           + jnp.dot(vpair_ref[i, j], pj, preferred_element_type=F32))
            m_new.append(mi)
        return m_new, acc_new

    _, acc = lax.fori_loop(0, (iq + 1) // 2, step, (m0, acc0))
    for i in hh:
        o_ref[0, i * dh:(i + 1) * dh, :] = (acc[i][0:dh, :] / acc[i][dh:dh + 1, :]).astype(BF16)


def _moba_out_body(x_ref, o_ref_in, w_ref, o_ref):
    o_ref[0] = x_ref[0] + lax.dot_general(o_ref_in[0], w_ref[...], (((0,), (0,)), ((), ())),
                                          preferred_element_type=F32)


def _moba(x, g, w_qkv, w_o, ts=512):
    b, s, d = x.shape
    nb = s // MOBA_BLOCK
    dh = MOBA_HEAD_DIM
    qkvt = pl.pallas_call(
        functools.partial(_moba_qkv_body, nc=512),
        out_shape=jax.ShapeDtypeStruct((b, 3 * d, s), BF16),
        grid=(b, s // ts),
        in_specs=[pl.BlockSpec((1, ts, d), lambda i, j: (i, j, 0)),
                  _const_spec((1, d)),
                  _const_spec((3 * d, d))],
        out_specs=pl.BlockSpec((1, 3 * d, ts), lambda i, j: (i, 0, j)),
        compiler_params=_params(2),
        name="moba_qkv",
    )(x, _row(g), w_qkv.T.astype(BF16))

    hg = MOBA_HEAD_GROUP
    ng = MOBA_HEADS // hg
    gd = hg * dh
    ot = pl.pallas_call(
        functools.partial(_moba_attn_body, nb=nb),
        out_shape=jax.ShapeDtypeStruct((b, d, s), BF16),
        grid=(b, ng, nb),
        in_specs=[pl.BlockSpec((1, gd, MOBA_BLOCK), lambda i, j, k: (i, j, k)),
                  pl.BlockSpec((1, gd, s), lambda i, j, k: (i, ng + j, 0)),
                  pl.BlockSpec((1, gd, s), lambda i, j, k: (i, 2 * ng + j, 0))],
        out_specs=pl.BlockSpec((1, gd, MOBA_BLOCK), lambda i, j, k: (i, j, k)),
        scratch_shapes=[pltpu.VMEM((hg, nb // 2, 2 * MOBA_BLOCK, dh + nb), BF16),
                        pltpu.VMEM((hg, nb, dh + MOBA_ONES_ROWS, MOBA_BLOCK), BF16),
                        pltpu.VMEM((hg, nb // 2, dh + MOBA_ONES_ROWS, 2 * MOBA_BLOCK), BF16),
                        pltpu.VMEM((hg, nb, dh), F32)],
        compiler_params=_params(3),
        name="moba_attn",
    )(qkvt, qkvt, qkvt)

    return pl.pallas_call(
        _moba_out_body,
        out_shape=jax.ShapeDtypeStruct((b, s, d), F32),
        grid=(b, s // ts),
        in_specs=[pl.BlockSpec((1, ts, d), lambda i, j: (i, j, 0)),
                  pl.BlockSpec((1, d, ts), lambda i, j: (i, 0, j)),
                  _const_spec((d, d))],
        out_specs=pl.BlockSpec((1, ts, d), lambda i, j: (i, j, 0)),
        compiler_params=_params(2),
        name="moba_out",
    )(x, ot, w_o.astype(BF16))


def _rwkv_body(x_ref, g_ref, mu_ref, wr_ref, wk_ref, wv_ref, w0_ref, w1_ref, w2_ref, a0_ref, a1_ref,
               a2_ref, g1_ref, g2_ref, kk_ref, ka_ref, rk_ref, gnw_ref, gnb_ref, wo_ref, p_ref, pt_ref,
               o_ref,
               hbuf_ref, st_ref, r_s, lw_s, k_s, v_s, a_s, b_s, y_s, gate_s, bonus_s,
               ch0, t0, mv0, mr0, gt0, ch1, t1, mv1, mr1, gt1, *, tiles_per_seq):
    bufs = ((ch0, t0, mv0, mr0, gt0), (ch1, t1, mv1, mr1, gt1))
    tm = x_ref.shape[0]
    lc = RWKV_CHUNK
    hd = RWKV_HEAD_DIM
    hs = SUBLANES

    @pl.when(pl.program_id(0) % tiles_per_seq == 0)
    def _():
        hbuf_ref[tm:tm + hs, :] = jnp.zeros((hs, D_MODEL), F32)
        st_ref[...] = jnp.zeros_like(st_ref)

    def seg_sum(v):
        return jnp.dot(v.astype(BF16), p_ref[...], preferred_element_type=F32)

    def seg_bcast(v):
        return _dot_hilo(v, pt_ref[...])

    x = x_ref[...]
    hn = _rms(x, g_ref[...])
    hbuf_ref[0:hs, :] = hbuf_ref[tm:tm + hs, :]
    hbuf_ref[hs:hs + tm, :] = hn
    xx = hbuf_ref[hs - 1:hs - 1 + tm, :] - hn

    def mix(i):
        return hn + xx * mu_ref[i:i + 1, :]

    r = _dot(mix(0), wr_ref[...])
    k = _dot(mix(1), wk_ref[...])
    v = _dot(mix(2), wv_ref[...])
    wpre = w0_ref[...] + _dot(jnp.tanh(_dot(mix(3), w1_ref[...])), w2_ref[...])
    lw = -jnp.exp(-jax.nn.softplus(-wpre) - 0.5)
    a = jax.nn.sigmoid(a0_ref[...] + _dot(_dot(mix(4), a1_ref[...]), a2_ref[...]))
    gate_s[...] = _dot(jax.nn.sigmoid(_dot(mix(5), g1_ref[...])), g2_ref[...])

    kk = k * kk_ref[...]
    inv = 1.0 / jnp.maximum(jnp.sqrt(seg_sum(kk * kk)), 1e-12)
    kk = kk * seg_bcast(inv)
    k = k * (1.0 + (a - 1.0) * ka_ref[...])
    bonus_s[...] = seg_bcast(seg_sum(r * k * rk_ref[...])) * v
    r_s[...] = r
    lw_s[...] = lw
    k_s[...] = k
    v_s[...] = v
    a_s[...] = -kk
    b_s[...] = kk * a

    ti = lax.broadcasted_iota(jnp.int32, (lc, lc), 0)
    si = lax.broadcasted_iota(jnp.int32, (lc, lc), 1)
    tril = jnp.where(si <= ti, 1.0, 0.0).astype(BF16)
    tril3 = jnp.concatenate([tril, tril, tril], axis=1)

    pw = 2 * hd
    pp = range(RWKV_HEADS // 2)
    psl = [slice(p * pw, (p + 1) * pw) for p in pp]
    row1 = lax.broadcasted_iota(jnp.int32, (lc, pw), 0)
    lane1 = lax.broadcasted_iota(jnp.int32, (lc, pw), 1)
    is_a = lane1 < hd
    half = jnp.where(is_a, lane1, lane1 - hd)
    strict_pair = half < row1
    incl_pair = half <= row1
    eye_pair = jnp.where(half == row1, 1.0, 0.0)
    row2 = lax.broadcasted_iota(jnp.int32, (2 * lc, pw), 0)
    lane2 = lax.broadcasted_iota(jnp.int32, (2 * lc, pw), 1)
    is_a2 = lane2 < hd
    same_head = (row2 < hd) == is_a2
    lane4 = lax.broadcasted_iota(jnp.int32, (lc, 2 * pw), 1)
    w_is_a = (lane4 < hd) | ((lane4 >= pw) & (lane4 < pw + hd))
    w_is_t = lane4 >= pw

    def diag2(x):
        return jnp.concatenate([jnp.where(is_a, x, 0.0), jnp.where(is_a, 0.0, x)], axis=0)

    def prepare(c):
        ch, t_s, mv_s, mr_s, gt_s = bufs[c % 2]
        rows = slice(c * lc, (c + 1) * lc)
        lw_c = lw_s[rows, :]
        lg = _cumsum_rows(tril3, lw_c)
        lg_last = lg[lc - 1:lc, :]
        g_inv = jnp.exp(-lg)
        g_tail = jnp.exp(lg_last - lg)
        gt_s[...] = jnp.broadcast_to(jnp.exp(lg_last), gt_s.shape)
        ch[0, 0:lc, :] = a_s[rows, :] * jnp.exp(lg - lw_c)
        ch[0, lc:2 * lc, :] = r_s[rows, :] * jnp.exp(lg)
        ch[1, 0:lc, :] = b_s[rows, :] * g_inv
        ch[1, lc:2 * lc, :] = k_s[rows, :] * g_inv
        ch[2, 0:lc, :] = b_s[rows, :] * g_tail
        ch[2, lc:2 * lc, :] = k_s[rows, :] * g_tail
        ar = [ch[0, :, psl[p]] for p in pp]
        bk = [ch[1, :, psl[p]].astype(BF16) for p in pp]
        gma = [_dot_nt(jnp.where(is_a2, ar[p], 0.0), bk[p]) for p in pp]
        gmb = [_dot_nt(jnp.where(is_a2, 0.0, ar[p]),
                       jnp.concatenate([bk[p][lc:2 * lc, :], bk[p][0:lc, :]], axis=0)) for p in pp]
        w = []
        for p in pp:
            ta, tb = gma[p][0:lc, :], gmb[p][0:lc, :]
            n_pair = jnp.where(strict_pair, jnp.where(is_a, ta, tb), 0.0)
            mak_sw = jnp.where(strict_pair, jnp.where(is_a, tb, ta), 0.0)
            vp = v_s[rows, psl[p]]
            v_sw = jnp.concatenate([jnp.where(is_a, 0.0, vp), jnp.where(is_a, vp, 0.0)], axis=0)
            mv_s[p] = _dot(mak_sw, v_sw)
            mr_s[p] = jnp.concatenate([jnp.where(incl_pair, gma[p][lc:2 * lc, :], 0.0),
                                       jnp.where(incl_pair, gmb[p][lc:2 * lc, :], 0.0)], axis=1).astype(BF16)
            w.append(jnp.concatenate([n_pair, eye_pair], axis=1))
        steps = 1
        while steps < lc:
            r = [_dot(w[p][:, 0:pw],
                      jnp.concatenate([jnp.where(w_is_a, w[p], 0.0), jnp.where(w_is_a, 0.0, w[p])], axis=0))
                 for p in pp]
            w = [r[p] + jnp.where(w_is_t, w[p], 0.0) for p in pp]
            steps *= 2
        for p in pp:
            t_s[p] = w[p][:, pw:].astype(BF16)

    def advance(c):
        ch, t_s, mv_s, mr_s, gt_s = bufs[c % 2]
        rows = slice(c * lc, (c + 1) * lc)
        sbd = [st_ref[p] for p in pp]
        ars = [_dot_nt(ch[0, :, psl[p]], sbd[p]) for p in pp]
        u = [jnp.dot(t_s[p], diag2(ars[p][0:lc, :] + mv_s[p]).astype(BF16), preferred_element_type=F32)
             for p in pp]
        vp = [v_s[rows, psl[p]] for p in pp]
        for p in pp:
            uv_bd = jnp.concatenate([jnp.where(is_a, u[p], 0.0), jnp.where(is_a, vp[p], 0.0),
                                     jnp.where(is_a, 0.0, vp[p]), jnp.where(is_a, 0.0, u[p])], axis=0)
            y_s[rows, psl[p]] = ars[p][lc:2 * lc, :] + jnp.dot(mr_s[p], uv_bd.astype(BF16),
                                                               preferred_element_type=F32)
        for p in pp:
            upd = _dot_tn(jnp.concatenate([u[p], vp[p]], axis=0), ch[2, :, psl[p]])
            st_ref[p] = sbd[p] * gt_s[0:1, psl[p]] + jnp.where(same_head, upd, 0.0)

    n_chunks = tm // lc
    prepare(0)
    for c in range(n_chunks):
        if c + 1 < n_chunks:
            prepare(c + 1)
        advance(c)

    y = y_s[...]
    mean = seg_bcast(seg_sum(y)) * (1.0 / hd)
    dy = y - mean
    var = seg_bcast(seg_sum(dy * dy)) * (1.0 / hd)
    yn = dy * lax.rsqrt(var + RWKV_GN_EPS) * gnw_ref[...] + gnb_ref[...]
    o_ref[...] = x + _dot((yn + bonus_s[...]) * gate_s[...], wo_ref[...])


def _rwkv7(x2, seq, g, mu, w_rkv, w0, w1, w2, a0, a1, a2, g1, g2, k_k, k_a, r_k, gn_w, gn_b, w_o, tm=256):
    rows = x2.shape[0]
    d = D_MODEL
    head_of = jnp.arange(d) // RWKV_HEAD_DIM
    p = (head_of[:, None] == jnp.arange(128)[None, :]).astype(BF16)
    args = (x2, _row(g), mu.astype(F32), w_rkv[0].astype(BF16), w_rkv[1].astype(BF16), w_rkv[2].astype(BF16),
            _row(w0), w1.astype(BF16), w2.astype(BF16), _row(a0), a1.astype(BF16), a2.astype(BF16),
            g1.astype(BF16), g2.astype(BF16), _row(k_k), _row(k_a), _row(r_k), _row(gn_w), _row(gn_b),
            w_o.astype(BF16), p, jnp.concatenate([p.T, p.T], axis=0))
    in_specs = [pl.BlockSpec((tm, d), lambda i: (i, 0))] + [_const_spec(a.shape) for a in args[1:]]
    big = pltpu.VMEM((tm, d), F32)
    lc = RWKV_CHUNK
    npair, pw = RWKV_HEADS // 2, 2 * RWKV_HEAD_DIM
    return pl.pallas_call(
        functools.partial(_rwkv_body, tiles_per_seq=seq // tm),
        out_shape=jax.ShapeDtypeStruct((rows, d), F32),
        grid=(rows // tm,),
        in_specs=in_specs,
        out_specs=pl.BlockSpec((tm, d), lambda i: (i, 0)),
        scratch_shapes=[pltpu.VMEM((tm + 2 * SUBLANES, d), F32),
                        pltpu.VMEM((npair, pw, pw), F32),
                        big, big, big, big, big, big, big, big, big,
                        ] + 2 * [pltpu.VMEM((3, 2 * lc, d), F32),
                                 pltpu.VMEM((npair, lc, pw), BF16),
                                 pltpu.VMEM((npair, lc, pw), F32),
                                 pltpu.VMEM((npair, lc, 2 * pw), BF16),
                                 pltpu.VMEM((SUBLANES, d), F32)],
        compiler_params=_params(1),
        name="rwkv7",
    )(*args)


def _ssd_in_body(x_ref, g_ref, w_ref, wdt_ref, zx_ref, dt_ref, *, nc):
    hn = _rms(x_ref[...], g_ref[...])
    h = hn.astype(BF16)
    for n in range(w_ref.shape[1] // nc):
        zx_ref[:, n * nc:(n + 1) * nc] = jnp.dot(h, w_ref[:, n * nc:(n + 1) * nc],
                                                 preferred_element_type=F32)
    h_lo = (hn - h.astype(F32)).astype(BF16)
    w_hi, w_lo = _split2(wdt_ref[...])
    dt_ref[...] = (jnp.dot(h, w_hi, preferred_element_type=F32)
                   + jnp.dot(h, w_lo, preferred_element_type=F32)
                   + jnp.dot(h_lo, w_hi, preferred_element_type=F32))


def _ssd_core_body(zx_ref, dt_ref, cw_ref, cb_ref, dtb_ref, alog_ref, dskip_ref, nw_ref, e_ref, o_ref,
                   cbuf_ref, state_ref, y_s):
    lc = SSD_CHUNK
    di = SSD_D_INNER
    hs = SUBLANES
    gw = di // SSD_GROUPS
    ns = SSD_STATE

    @pl.when(pl.program_id(1) == 0)
    def _():
        cbuf_ref[lc:lc + hs, :] = jnp.zeros((hs, SSD_CONV_DIM), F32)
        state_ref[...] = jnp.zeros_like(state_ref)

    z = zx_ref[:, 0:di]
    xbc_raw = zx_ref[:, di:di + SSD_CONV_DIM]
    cbuf_ref[0:hs, :] = cbuf_ref[lc:lc + hs, :]
    cbuf_ref[hs:hs + lc, :] = xbc_raw
    xbc = cb_ref[...] + cw_ref[SSD_CONV - 1:SSD_CONV, :] * xbc_raw
    win = cbuf_ref[0:hs + lc, :]
    for j in range(SSD_CONV - 1):
        back = SSD_CONV - 1 - j
        xbc = xbc + cw_ref[j:j + 1, :] * pltpu.roll(win, back, 0)[hs:hs + lc, :]
    xbc = jax.nn.silu(xbc)
    xs = xbc[:, 0:di]
    bm = xbc[:, di:di + SSD_GROUPS * ns]
    cm = xbc[:, di + SSD_GROUPS * ns:]

    dt = jax.nn.softplus(dt_ref[...] + dtb_ref[...])
    adt = dt * (-jnp.exp(alog_ref[...]))
    li = lax.broadcasted_iota(jnp.int32, (lc, lc), 0)
    si = lax.broadcasted_iota(jnp.int32, (lc, lc), 1)
    causal = si <= li
    first_head = lax.broadcasted_iota(jnp.int32, (lc, 2 * SSD_HEAD_DIM), 1) < SSD_HEAD_DIM
    tril = jnp.where(causal, 1.0, 0.0).astype(BF16)
    acs = _cumsum_rows(jnp.concatenate([tril, tril, tril], axis=1), adt)
    acs_t = acs.T
    a_last = acs[lc - 1:lc, :]

    def expand(v):
        return _dot_hilo(v, e_ref[...])

    exp_acs_x = expand(jnp.exp(acs))
    xd = xs * expand(dt)
    xdec = xd * expand(jnp.exp(a_last - acs))

    for g in range(SSD_GROUPS):
        bg = bm[:, g * ns:(g + 1) * ns]
        cg = cm[:, g * ns:(g + 1) * ns]
        cb = _dot_nt(cg, bg)
        gsl = slice(g * gw, (g + 1) * gw)
        st_g = state_ref[:, gsl]
        y_off = _dot(cg, st_g) * exp_acs_x[:, gsl]
        def masked_decay(h):
            col = jnp.broadcast_to(acs[:, h:h + 1], (lc, lc))
            row = jnp.broadcast_to(acs_t[h:h + 1, :], (lc, lc))
            return (cb * jnp.where(causal, jnp.exp(col - row), 0.0)).astype(BF16)

        for e in range(0, SSD_HEADS // SSD_GROUPS, 2):
            h = g * (SSD_HEADS // SSD_GROUPS) + e
            psl = slice(h * SSD_HEAD_DIM, (h + 2) * SSD_HEAD_DIM)
            xdp = xd[:, psl]
            xd_bd = jnp.concatenate([jnp.where(first_head, xdp, 0.0), jnp.where(first_head, 0.0, xdp)], axis=0)
            o0 = e * SSD_HEAD_DIM
            y_s[:, psl] = (jnp.dot(jnp.concatenate([masked_decay(h), masked_decay(h + 1)], axis=1),
                                   xd_bd.astype(BF16), preferred_element_type=F32)
                           + y_off[:, o0:o0 + 2 * SSD_HEAD_DIM])
        state_ref[:, gsl] = st_g * exp_acs_x[lc - 1:lc, gsl] + _dot_tn(bg, xdec[:, gsl])

    y = y_s[...] + xs * dskip_ref[...]
    yz = y * jax.nn.silu(z)
    for g in range(SSD_GROUPS):
        gsl = slice(g * gw, (g + 1) * gw)
        yg = yz[:, gsl]
        o_ref[:, gsl] = (yg * lax.rsqrt(jnp.mean(yg * yg, axis=-1, keepdims=True) + SSD_NORM_EPS)
                         * nw_ref[:, gsl])


def _mamba2(x2, batch, seq, g, w_in, conv_w, conv_b, dt_bias, a_log, d_skip, norm_w, w_out, tm=512):
    rows = x2.shape[0]
    d = D_MODEL
    di = SSD_D_INNER
    nzx = di + SSD_CONV_DIM
    zx, dt = pl.pallas_call(
        functools.partial(_ssd_in_body, nc=512),
        out_shape=(jax.ShapeDtypeStruct((rows, nzx), F32), jax.ShapeDtypeStruct((rows, SSD_HEADS), F32)),
        grid=(rows // tm,),
        in_specs=[pl.BlockSpec((tm, d), lambda i: (i, 0)),
                  _const_spec((1, d)),
                  _const_spec((d, nzx)),
                  _const_spec((d, SSD_HEADS))],
        out_specs=(pl.BlockSpec((tm, nzx), lambda i: (i, 0)),
                   pl.BlockSpec((tm, SSD_HEADS), lambda i: (i, 0))),
        compiler_params=_params(1),
        name="ssd_in",
    )(x2, _row(g), w_in[:, :nzx].astype(BF16), w_in[:, nzx:].astype(F32))

    lc = SSD_CHUNK
    nchunk = seq // lc
    expand = (jnp.arange(SSD_HEADS)[:, None] == (jnp.arange(di) // SSD_HEAD_DIM)[None, :]).astype(BF16)
    d_x = jnp.repeat(d_skip.astype(F32), SSD_HEAD_DIM).reshape(1, di)
    args = (zx, dt, conv_w.astype(F32), _row(conv_b), _row(dt_bias), _row(a_log), d_x, _row(norm_w),
            jnp.concatenate([expand, expand], axis=0))
    yz = pl.pallas_call(
        _ssd_core_body,
        out_shape=jax.ShapeDtypeStruct((rows, di), F32),
        grid=(batch, nchunk),
        in_specs=[pl.BlockSpec((lc, nzx), lambda b, c: (b * nchunk + c, 0)),
                  pl.BlockSpec((lc, SSD_HEADS), lambda b, c: (b * nchunk + c, 0))]
                 + [_const_spec(a.shape) for a in args[2:]],
        out_specs=pl.BlockSpec((lc, di), lambda b, c: (b * nchunk + c, 0)),
        scratch_shapes=[pltpu.VMEM((lc + 2 * SUBLANES, SSD_CONV_DIM), F32),
                        pltpu.VMEM((SSD_STATE, di), F32),
                        pltpu.VMEM((lc, di), F32)],
        compiler_params=_params(2),
        name="ssd_core",
    )(*args)
    return _proj_res(x2, yz, w_out)


def _conf_body(x_ref, g_ref, w1_ref, b1_ref, dw_ref, dwb_ref, lnw_ref, lnb_ref, w2_ref, b2_ref, o_ref,
               ubuf_ref, c_s, *, tiles_per_seq, rb, lanes):
    tm = x_ref.shape[0]
    d = D_MODEL
    hl = CONF_HALO

    @pl.when(pl.program_id(0) % tiles_per_seq == 0)
    def _():
        ubuf_ref[tm:tm + hl, :] = jnp.zeros((hl, d), F32)

    x = x_ref[...]
    h = _rms(x, g_ref[...]).astype(BF16)
    p = jnp.dot(h, w1_ref[...], preferred_element_type=F32) + b1_ref[...]
    u = p[:, 0:d] * jax.nn.sigmoid(p[:, d:2 * d])
    ubuf_ref[0:hl, :] = ubuf_ref[tm:tm + hl, :]
    ubuf_ref[hl:hl + tm, :] = u
    base = hl - (CONF_KERNEL - 1)
    for r0 in range(0, tm, rb):
        for c0 in range(0, d, lanes):
            acc = jnp.broadcast_to(dwb_ref[:, c0:c0 + lanes], (rb, lanes))
            win = ubuf_ref[r0:r0 + rb + hl, c0:c0 + lanes]
            for res in range(SUBLANES):
                offs = [o for o in range(base, base + CONF_KERNEL) if o % SUBLANES == res]
                slab = win if res == 0 else pltpu.roll(win, rb + hl - res, 0)
                for o in offs:
                    a0 = o - res
                    acc = acc + dw_ref[o - base:o - base + 1, c0:c0 + lanes] * slab[a0:a0 + rb, :]
            c_s[r0:r0 + rb, c0:c0 + lanes] = acc
    y = c_s[...]
    mu = jnp.mean(y, axis=-1, keepdims=True)
    dy = y - mu
    var = jnp.mean(dy * dy, axis=-1, keepdims=True)
    zn = dy * lax.rsqrt(var + CONF_LN_EPS) * lnw_ref[...] + lnb_ref[...]
    o_ref[...] = x + _dot(jax.nn.silu(zn), w2_ref[...]) + b2_ref[...]


def _conformer(x2, seq, g, w_pw1, b_pw1, dw_w, dw_b, ln_w, ln_b, w_pw2, b_pw2, tm=256):
    rows = x2.shape[0]
    d = D_MODEL
    args = (x2, _row(g), w_pw1.astype(BF16), _row(b_pw1), dw_w.astype(F32), _row(dw_b), _row(ln_w),
            _row(ln_b), w_pw2.astype(BF16), _row(b_pw2))
    return pl.pallas_call(
        functools.partial(_conf_body, tiles_per_seq=seq // tm, rb=128, lanes=128),
        out_shape=jax.ShapeDtypeStruct((rows, d), F32),
        grid=(rows // tm,),
        in_specs=[pl.BlockSpec((tm, d), lambda i: (i, 0))] + [_const_spec(a.shape) for a in args[1:]],
        out_specs=pl.BlockSpec((tm, d), lambda i: (i, 0)),
        scratch_shapes=[pltpu.VMEM((tm + CONF_HALO, d), F32), pltpu.VMEM((tm, d), F32)],
        compiler_params=_params(1),
        name="conformer",
    )(*args)


def kernel(x, norm_mix, norm_ffn, norm_final, ffn_w_up, ffn_conv_w, ffn_conv_b, ffn_w_down, moba_w_qkv, moba_w_o, rwkv_mu, rwkv_w_rkv, rwkv_w0, rwkv_w1, rwkv_w2, rwkv_a0, rwkv_a1, rwkv_a2, rwkv_g1, rwkv_g2, rwkv_k_k, rwkv_k_a, rwkv_r_k, rwkv_gn_w, rwkv_gn_b, rwkv_w_o, ssd_w_in, ssd_conv_w, ssd_conv_b, ssd_dt_bias, ssd_a_log, ssd_d, ssd_norm_w, ssd_w_out, conf_w_pw1, conf_b_pw1, conf_dw_w, conf_dw_b, conf_ln_w, conf_ln_b, conf_w_pw2, conf_b_pw2):
    b, s, d = x.shape
    depth = norm_mix.shape[0]
    rows = b * s
    for i in range(depth):
        kind, j = i % 4, i // 4
        if kind == 0:
            x2 = _moba(x.reshape(b, s, d), norm_mix[i], moba_w_qkv[j], moba_w_o[j]).reshape(rows, d)
        elif kind == 1:
            x2 = _rwkv7(x.reshape(rows, d), s, norm_mix[i], rwkv_mu[j], rwkv_w_rkv[j], rwkv_w0[j], rwkv_w1[j],
                        rwkv_w2[j], rwkv_a0[j], rwkv_a1[j], rwkv_a2[j], rwkv_g1[j], rwkv_g2[j], rwkv_k_k[j],
                        rwkv_k_a[j], rwkv_r_k[j], rwkv_gn_w[j], rwkv_gn_b[j], rwkv_w_o[j])
        elif kind == 2:
            x2 = _mamba2(x.reshape(rows, d), b, s, norm_mix[i], ssd_w_in[j], ssd_conv_w[j], ssd_conv_b[j],
                         ssd_dt_bias[j], ssd_a_log[j], ssd_d[j], ssd_norm_w[j], ssd_w_out[j])
        else:
            x2 = _conformer(x.reshape(rows, d), s, norm_mix[i], conf_w_pw1[j], conf_b_pw1[j], conf_dw_w[j],
                            conf_dw_b[j], conf_ln_w[j], conf_ln_b[j], conf_w_pw2[j], conf_b_pw2[j])
        x = _conv_ffn(x2, s, norm_ffn[i], ffn_w_up[i], ffn_conv_w[i], ffn_conv_b[i], ffn_w_down[i],
                      norm_final, final=(i == depth - 1))
    return x.reshape(b, s, d)
```

```python
import functools

import jax
import jax.numpy as jnp
from jax import lax
from jax.experimental import pallas as pl
from jax.experimental.pallas import tpu as pltpu

F32 = jnp.float32
BF16 = jnp.bfloat16

D_MODEL = 1024
NORM_EPS = 1e-6
NEG_INF = -1e30
LOG2E = 1.4426950408889634

MOBA_HEADS = 16
MOBA_HEAD_DIM = 64
MOBA_BLOCK = 256
MOBA_TOPK = 3
MOBA_ONES_ROWS = 16
MOBA_HEAD_GROUP = 8

RWKV_HEADS = 16
RWKV_HEAD_DIM = 64
RWKV_GN_EPS = 64e-5
RWKV_CHUNK = 64

SSD_D_INNER = 2048
SSD_HEAD_DIM = 64
SSD_HEADS = 32
SSD_GROUPS = 4
SSD_STATE = 128
SSD_CONV = 4
SSD_CHUNK = 128
SSD_CONV_DIM = SSD_D_INNER + 2 * SSD_GROUPS * SSD_STATE
SSD_NORM_EPS = 1e-5

CONF_KERNEL = 31
CONF_LN_EPS = 1e-5
CONF_HALO = 32

D_FF = 2816
FFN_CONV = 3
FFN_CHUNK = 256

SUBLANES = 8
VMEM_LIMIT = 56 * 1024 * 1024


def _rms(x, g, eps=NORM_EPS):
    return x * lax.rsqrt(jnp.mean(x * x, axis=-1, keepdims=True) + eps) * g


def _dot(a, b):
    return jnp.dot(a.astype(BF16), b.astype(BF16), preferred_element_type=F32)


def _dot_nt(a, b):
    return lax.dot_general(a.astype(BF16), b.astype(BF16), (((1,), (1,)), ((), ())),
                           preferred_element_type=F32)


def _dot_tn(a, b):
    return lax.dot_general(a.astype(BF16), b.astype(BF16), (((0,), (0,)), ((), ())),
                           preferred_element_type=F32)


def _split2(a):
    hi = a.astype(BF16)
    lo = (a - hi.astype(F32)).astype(BF16)
    return hi, lo


def _split3(a):
    hi = a.astype(BF16)
    r = a - hi.astype(F32)
    mid = r.astype(BF16)
    lo = (r - mid.astype(F32)).astype(BF16)
    return hi, mid, lo


def _cumsum_rows(tril3, b):
    return jnp.dot(tril3, jnp.concatenate(_split3(b), axis=0), preferred_element_type=F32)


def _dot_hilo(a, b2_bf16):
    return jnp.dot(jnp.concatenate(_split2(a), axis=1), b2_bf16, preferred_element_type=F32)


def _const_spec(shape):
    n = len(shape)
    return pl.BlockSpec(shape, lambda *_: (0,) * n, pipeline_mode=pl.Buffered(1))


def _params(n_axes):
    return pltpu.CompilerParams(dimension_semantics=("arbitrary",) * n_axes,
                                vmem_limit_bytes=VMEM_LIMIT)


def _row(v):
    return v.reshape(1, -1).astype(F32)


def _ffn_body(x_ref, g_ref, wup_ref, cw_ref, cb_ref, wdn_ref, gf_ref, o_ref, halo_ref, act_ref,
              *, tiles_per_seq, final):
    tm = x_ref.shape[0]
    fc = FFN_CHUNK
    hs = SUBLANES

    @pl.when(pl.program_id(0) % tiles_per_seq == 0)
    def _():
        halo_ref[...] = jnp.zeros_like(halo_ref)

    x = x_ref[...]
    h = _rms(x, g_ref[...]).astype(BF16)

    r8 = lax.broadcasted_iota(jnp.int32, (hs, fc), 0)

    def conv(u, off):
        prev = halo_ref[:, off:off + fc]
        halo_ref[:, off:off + fc] = u[tm - hs:tm, :]
        w0 = cw_ref[0:1, off:off + fc]
        w1 = cw_ref[1:2, off:off + fc]
        u1 = pltpu.roll(u, 1, 0)
        u2 = pltpu.roll(u, 2, 0)
        y = w0 * u2 + w1 * u1 + cw_ref[2:3, off:off + fc] * u + cb_ref[0:1, off:off + fc]
        fix = (w0 * jnp.where(r8 < 2, pltpu.roll(prev, 2, 0) - u2[0:hs, :], 0.0)
               + w1 * jnp.where(r8 < 1, pltpu.roll(prev, 1, 0) - u1[0:hs, :], 0.0))
        return jnp.concatenate([y[0:hs, :] + fix, y[hs:, :]], axis=0)

    def up(c):
        og, ou = c * fc, D_FF + c * fc
        return (jnp.dot(h, wup_ref[:, og:og + fc], preferred_element_type=F32),
                jnp.dot(h, wup_ref[:, ou:ou + fc], preferred_element_type=F32))

    n_chunks = D_FF // fc
    nxt = up(0)
    for c in range(n_chunks):
        ug, uu = nxt
        if c + 1 < n_chunks:
            nxt = up(c + 1)
        yg = conv(ug, c * fc)
        yu = conv(uu, D_FF + c * fc)
        act_ref[:, c * fc:(c + 1) * fc] = (jax.nn.silu(yg) * yu).astype(BF16)
    y = x + jnp.dot(act_ref[...], wdn_ref[...], preferred_element_type=F32)
    if final:
        y = _rms(y, gf_ref[...])
    o_ref[...] = y


def _conv_ffn(x2, seq, g, w_up, conv_w, conv_b, w_down, g_final, final, tm=512):
    rows = x2.shape[0]
    body = functools.partial(_ffn_body, tiles_per_seq=seq // tm, final=final)
    return pl.pallas_call(
        body,
        out_shape=jax.ShapeDtypeStruct((rows, D_MODEL), F32),
        grid=(rows // tm,),
        in_specs=[
            pl.BlockSpec((tm, D_MODEL), lambda i: (i, 0)),
            _const_spec((1, D_MODEL)),
            _const_spec((D_MODEL, 2 * D_FF)),
            _const_spec((FFN_CONV, 2 * D_FF)),
            _const_spec((1, 2 * D_FF)),
            _const_spec((D_FF, D_MODEL)),
            _const_spec((1, D_MODEL)),
        ],
        out_specs=pl.BlockSpec((tm, D_MODEL), lambda i: (i, 0)),
        scratch_shapes=[pltpu.VMEM((SUBLANES, 2 * D_FF), F32), pltpu.VMEM((tm, D_FF), BF16)],
        compiler_params=_params(1),
        name="conv_ffn",
    )(x2, _row(g), w_up.astype(BF16), conv_w.astype(F32), _row(conv_b), w_down.astype(BF16), _row(g_final))


def _proj_res_body(x_ref, y_ref, w_ref, o_ref):
    o_ref[...] = x_ref[...] + jnp.dot(y_ref[...].astype(BF16), w_ref[...], preferred_element_type=F32)


def _proj_res(x2, y2, w, tm=512):
    rows, k = y2.shape
    return pl.pallas_call(
        _proj_res_body,
        out_shape=jax.ShapeDtypeStruct((rows, D_MODEL), F32),
        grid=(rows // tm,),
        in_specs=[pl.BlockSpec((tm, D_MODEL), lambda i: (i, 0)),
                  pl.BlockSpec((tm, k), lambda i: (i, 0)),
                  _const_spec((k, D_MODEL))],
        out_specs=pl.BlockSpec((tm, D_MODEL), lambda i: (i, 0)),
        compiler_params=_params(1),
        name="proj_res",
    )(x2, y2, w.astype(BF16))


def _moba_qkv_body(x_ref, g_ref, wt_ref, o_ref, *, nc):
    h = _rms(x_ref[0], g_ref[...]).astype(BF16)
    for n in range(3 * D_MODEL // nc):
        o_ref[0, n * nc:(n + 1) * nc, :] = lax.dot_general(
            wt_ref[n * nc:(n + 1) * nc, :], h, (((1,), (1,)), ((), ())),
            preferred_element_type=F32).astype(BF16)


def _moba_attn_body(q_ref, k_ref, v_ref, o_ref, kaug_ref, vblk_ref, vpair_ref, kmean_ref, *, nb):
    bs = MOBA_BLOCK
    dh = MOBA_HEAD_DIM
    hg = MOBA_HEAD_GROUP
    hh = range(hg)
    iq = pl.program_id(2)

    @pl.when(iq == 0)
    def _():
        blk = lax.broadcasted_iota(jnp.int32, (bs, nb), 1)
        for j in range(nb):
            kr = k_ref[0, :, j * bs:(j + 1) * bs].astype(F32).T
            onehot = jnp.where(blk == j, 1.0, 0.0).astype(BF16)
            for i in hh:
                kri = kr[:, i * dh:(i + 1) * dh]
                kaug_ref[i, j // 2, (j % 2) * bs:(j % 2 + 1) * bs, :] = jnp.concatenate(
                    [kri.astype(BF16), onehot], axis=1)
                kmean_ref[i, j:j + 1, :] = jnp.mean(kri, axis=0, keepdims=True)
                vblk_ref[i, j, 0:dh, :] = v_ref[0, i * dh:(i + 1) * dh, j * bs:(j + 1) * bs]
                vblk_ref[i, j, dh:dh + MOBA_ONES_ROWS, :] = jnp.ones((MOBA_ONES_ROWS, bs), BF16)
                if j % 2 == 0:
                    vpair_ref[i, j // 2, 0:dh, :] = v_ref[0, i * dh:(i + 1) * dh, j * bs:(j + 2) * bs]
                    vpair_ref[i, j // 2, dh:dh + MOBA_ONES_ROWS, :] = jnp.ones((MOBA_ONES_ROWS, 2 * bs), BF16)

    jidx = lax.broadcasted_iota(jnp.int32, (nb, bs), 0)
    qs, qaug = [], []
    for i in hh:
        q_raw = q_ref[0, i * dh:(i + 1) * dh, :]
        km_hi, km_lo = _split2(kmean_ref[i])
        gate = (jnp.dot(km_hi, q_raw, preferred_element_type=F32)
                + jnp.dot(km_lo, q_raw, preferred_element_type=F32))
        cnt = jnp.zeros((nb, bs), jnp.int32)
        for jp in range(nb):
            gj = gate[jp:jp + 1, :]
            beats = (gj > gate) | ((gj == gate) & (jp < jidx))
            cnt = cnt + jnp.where(beats & (jp < iq), 1, 0)
        selm = (cnt < MOBA_TOPK) & (jidx < iq)
        qs.append(q_raw * (dh ** -0.5 * LOG2E))
        qaug.append(jnp.concatenate([qs[i], jnp.where(selm, 0.0, NEG_INF).astype(BF16)], axis=0))

    kpos = lax.broadcasted_iota(jnp.int32, (bs, bs), 0)
    qpos = lax.broadcasted_iota(jnp.int32, (bs, bs), 1)
    causal = kpos <= qpos
    own_rows = pl.ds(pl.multiple_of((iq % 2) * bs, bs), bs)
    s = [jnp.where(causal, jnp.dot(kaug_ref[i, iq // 2, own_rows, :][:, 0:dh], qs[i],
                                   preferred_element_type=F32), NEG_INF)
         for i in hh]
    m0 = [jnp.max(s[i], axis=0, keepdims=True) for i in hh]
    p = [jnp.exp2(s[i] - m0[i]) for i in hh]
    acc0 = [jnp.dot(vblk_ref[i, iq], p[i].astype(BF16), preferred_element_type=F32) for i in hh]

    def step(j, carry):
        m, acc = carry
        sj = [jnp.dot(kaug_ref[i, j], qaug[i], preferred_element_type=F32) for i in hh]
        m_new, acc_new = [], []
        for i in hh:
            mi = jnp.maximum(m[i], jnp.max(sj[i], axis=0, keepdims=True))
            pj = jnp.exp2(sj[i] - mi).astype(BF16)
            acc_new.append(jnp.exp2(m[i] - mi) * acc[i]
                           + jnp.dot(vpair_ref[i, j], pj, preferred_element_type=F32))
            m_new.append(mi)
        return m_new, acc_new

    _, acc = lax.fori_loop(0, (iq + 1) // 2, step, (m0, acc0))
    for i in hh:
        o_ref[0, i * dh:(i + 1) * dh, :] = (acc[i][0:dh, :] / acc[i][dh:dh + 1, :]).astype(BF16)


def _moba_out_body(x_ref, o_ref_in, w_ref, o_ref):
    o_ref[0] = x_ref[0] + lax.dot_general(o_ref_in[0], w_ref[...], (((0,), (0,)), ((), ())),
                                          preferred_element_type=F32)


def _moba(x, g, w_qkv, w_o, ts=512):
    b, s, d = x.shape
    nb = s // MOBA_BLOCK
    dh = MOBA_HEAD_DIM
    qkvt = pl.pallas_call(
        functools.partial(_moba_qkv_body, nc=512),
        out_shape=jax.ShapeDtypeStruct((b, 3 * d, s), BF16),
        grid=(b, s // ts),
        in_specs=[pl.BlockSpec((1, ts, d), lambda i, j: (i, j, 0)),
                  _const_spec((1, d)),
                  _const_spec((3 * d, d))],
        out_specs=pl.BlockSpec((1, 3 * d, ts), lambda i, j: (i, 0, j)),
        compiler_params=_params(2),
        name="moba_qkv",
    )(x, _row(g), w_qkv.T.astype(BF16))

    hg = MOBA_HEAD_GROUP
    ng = MOBA_HEADS // hg
    gd = hg * dh
    ot = pl.pallas_call(
        functools.partial(_moba_attn_body, nb=nb),
        out_shape=jax.ShapeDtypeStruct((b, d, s), BF16),
        grid=(b, ng, nb),
        in_specs=[pl.BlockSpec((1, gd, MOBA_BLOCK), lambda i, j, k: (i, j, k)),
                  pl.BlockSpec((1, gd, s), lambda i, j, k: (i, ng + j, 0)),
                  pl.BlockSpec((1, gd, s), lambda i, j, k: (i, 2 * ng + j, 0))],
        out_specs=pl.BlockSpec((1, gd, MOBA_BLOCK), lambda i, j, k: (i, j, k)),
        scratch_shapes=[pltpu.VMEM((hg, nb // 2, 2 * MOBA_BLOCK, dh + nb), BF16),
                        pltpu.VMEM((hg, nb, dh + MOBA_ONES_ROWS, MOBA_BLOCK), BF16),
                        pltpu.VMEM((hg, nb // 2, dh + MOBA_ONES_ROWS, 2 * MOBA_BLOCK), BF16),
                        pltpu.VMEM((hg, nb, dh), F32)],
        compiler_params=_params(3),
        name="moba_attn",
    )(qkvt, qkvt, qkvt)

    return pl.pallas_call(
        _moba_out_body,
        out_shape=jax.ShapeDtypeStruct((b, s, d), F32),
        grid=(b, s // ts),
        in_specs=[pl.BlockSpec((1, ts, d), lambda i, j: (i, j, 0)),
                  pl.BlockSpec((1, d, ts), lambda i, j: (i, 0, j)),
                  _const_spec((d, d))],
        out_specs=pl.BlockSpec((1, ts, d), lambda i, j: (i, j, 0)),
        compiler_params=_params(2),
        name="moba_out",
    )(x, ot, w_o.astype(BF16))


def _rwkv_body(x_ref, g_ref, mu_ref, wr_ref, wk_ref, wv_ref, w0_ref, w1_ref, w2_ref, a0_ref, a1_ref,
               a2_ref, g1_ref, g2_ref, kk_ref, ka_ref, rk_ref, gnw_ref, gnb_ref, wo_ref, p_ref, pt_ref,
               o_ref,
               hbuf_ref, st_ref, r_s, lw_s, k_s, v_s, a_s, b_s, y_s, gate_s, bonus_s,
               ch0, t0, mv0, mr0, gt0, ch1, t1, mv1, mr1, gt1, *, tiles_per_seq):
    bufs = ((ch0, t0, mv0, mr0, gt0), (ch1, t1, mv1, mr1, gt1))
    tm = x_ref.shape[0]
    lc = RWKV_CHUNK
    hd = RWKV_HEAD_DIM
    hs = SUBLANES

    @pl.when(pl.program_id(0) % tiles_per_seq == 0)
    def _():
        hbuf_ref[tm:tm + hs, :] = jnp.zeros((hs, D_MODEL), F32)
        st_ref[...] = jnp.zeros_like(st_ref)

    def seg_sum(v):
        return jnp.dot(v.astype(BF16), p_ref[...], preferred_element_type=F32)

    def seg_bcast(v):
        return _dot_hilo(v, pt_ref[...])

    x = x_ref[...]
    hn = _rms(x, g_ref[...])
    hbuf_ref[0:hs, :] = hbuf_ref[tm:tm + hs, :]
    hbuf_ref[hs:hs + tm, :] = hn
    xx = hbuf_ref[hs - 1:hs - 1 + tm, :] - hn

    def mix(i):
        return hn + xx * mu_ref[i:i + 1, :]

    r = _dot(mix(0), wr_ref[...])
    k = _dot(mix(1), wk_ref[...])
    v = _dot(mix(2), wv_ref[...])
    wpre = w0_ref[...] + _dot(jnp.tanh(_dot(mix(3), w1_ref[...])), w2_ref[...])
    lw = -jnp.exp(-jax.nn.softplus(-wpre) - 0.5)
    a = jax.nn.sigmoid(a0_ref[...] + _dot(_dot(mix(4), a1_ref[...]), a2_ref[...]))
    gate_s[...] = _dot(jax.nn.sigmoid(_dot(mix(5), g1_ref[...])), g2_ref[...])

    kk = k * kk_ref[...]
    inv = 1.0 / jnp.maximum(jnp.sqrt(seg_sum(kk * kk)), 1e-12)
    kk = kk * seg_bcast(inv)
    k = k * (1.0 + (a - 1.0) * ka_ref[...])
    bonus_s[...] = seg_bcast(seg_sum(r * k * rk_ref[...])) * v
    r_s[...] = r
    lw_s[...] = lw
    k_s[...] = k
    v_s[...] = v
    a_s[...] = -kk
    b_s[...] = kk * a

    ti = lax.broadcasted_iota(jnp.int32, (lc, lc), 0)
    si = lax.broadcasted_iota(jnp.int32, (lc, lc), 1)
    tril = jnp.where(si <= ti, 1.0, 0.0).astype(BF16)
    tril3 = jnp.concatenate([tril, tril, tril], axis=1)
    low_strict = si < ti
    ti2 = lax.broadcasted_iota(jnp.int32, (lc, 2 * lc), 0)
    si2 = lax.broadcasted_iota(jnp.int32, (lc, 2 * lc), 1)
    low_incl2 = jnp.where(si2 >= lc, si2 - lc, si2) <= ti2
    keep_u = si2 >= lc
    low_strict2 = si2 < ti2

    hh = range(RWKV_HEADS)
    sls = [slice(h * hd, (h + 1) * hd) for h in hh]
    eye2 = jnp.where(si2 - lc == ti2, 1.0, 0.0)

    def prepare(c):
        ch, t_s, mv_s, mr_s, gt_s = bufs[c % 2]
        rows = slice(c * lc, (c + 1) * lc)
        lw_c = lw_s[rows, :]
        lg = _cumsum_rows(tril3, lw_c)
        lg_last = lg[lc - 1:lc, :]
        g_inv = jnp.exp(-lg)
        g_tail = jnp.exp(lg_last - lg)
        gt_s[...] = jnp.broadcast_to(jnp.exp(lg_last), gt_s.shape)
        ch[0, 0:lc, :] = a_s[rows, :] * jnp.exp(lg - lw_c)
        ch[0, lc:2 * lc, :] = r_s[rows, :] * jnp.exp(lg)
        ch[1, 0:lc, :] = b_s[rows, :] * g_inv
        ch[1, lc:2 * lc, :] = k_s[rows, :] * g_inv
        ch[2, 0:lc, :] = b_s[rows, :] * g_tail
        ch[2, lc:2 * lc, :] = k_s[rows, :] * g_tail
        gm = [_dot_nt(ch[0, :, sls[h]], ch[1, :, sls[h]]) for h in hh]
        for h in hh:
            mr_s[h] = jnp.where(low_incl2, gm[h][lc:2 * lc, :], 0.0).astype(BF16)
            mv_s[h] = _dot(jnp.where(low_strict, gm[h][0:lc, lc:2 * lc], 0.0), v_s[rows, sls[h]])
        w = [jnp.where(keep_u, eye2, jnp.where(low_strict2, gm[h][0:lc, :], 0.0)) for h in hh]
        steps = 1
        while steps < lc:
            r = [_dot(w[h][:, 0:lc], w[h]) for h in hh]
            w = [r[h] + jnp.where(keep_u, w[h], 0.0) for h in hh]
            steps *= 2
        for h in hh:
            t_s[h] = w[h][:, lc:].astype(BF16)

    def advance(c):
        ch, t_s, mv_s, mr_s, gt_s = bufs[c % 2]
        rows = slice(c * lc, (c + 1) * lc)
        sh = [st_ref[h] for h in hh]
        ars = [_dot_nt(ch[0, :, sls[h]], sh[h]) for h in hh]
        u = [jnp.dot(t_s[h], (ars[h][0:lc, :] + mv_s[h]).astype(BF16), preferred_element_type=F32)
             for h in hh]
        uv = [jnp.concatenate([u[h].astype(BF16), v_s[rows, sls[h]].astype(BF16)], axis=0) for h in hh]
        for h in hh:
            y_s[rows, sls[h]] = ars[h][lc:2 * lc, :] + jnp.dot(mr_s[h], uv[h], preferred_element_type=F32)
        for h in hh:
            st_ref[h] = sh[h] * gt_s[0:1, sls[h]] + _dot_tn(uv[h], ch[2, :, sls[h]])

    n_chunks = tm // lc
    prepare(0)
    for c in range(n_chunks):
        if c + 1 < n_chunks:
            prepare(c + 1)
        advance(c)

    y = y_s[...]
    mean = seg_bcast(seg_sum(y)) * (1.0 / hd)
    dy = y - mean
    var = seg_bcast(seg_sum(dy * dy)) * (1.0 / hd)
    yn = dy * lax.rsqrt(var + RWKV_GN_EPS) * gnw_ref[...] + gnb_ref[...]
    o_ref[...] = x + _dot((yn + bonus_s[...]) * gate_s[...], wo_ref[...])


def _rwkv7(x2, seq, g, mu, w_rkv, w0, w1, w2, a0, a1, a2, g1, g2, k_k, k_a, r_k, gn_w, gn_b, w_o, tm=256):
    rows = x2.shape[0]
    d = D_MODEL
    head_of = jnp.arange(d) // RWKV_HEAD_DIM
    p = (head_of[:, None] == jnp.arange(128)[None, :]).astype(BF16)
    args = (x2, _row(g), mu.astype(F32), w_rkv[0].astype(BF16), w_rkv[1].astype(BF16), w_rkv[2].astype(BF16),
            _row(w0), w1.astype(BF16), w2.astype(BF16), _row(a0), a1.astype(BF16), a2.astype(BF16),
            g1.astype(BF16), g2.astype(BF16), _row(k_k), _row(k_a), _row(r_k), _row(gn_w), _row(gn_b),
            w_o.astype(BF16), p, jnp.concatenate([p.T, p.T], axis=0))
    in_specs = [pl.BlockSpec((tm, d), lambda i: (i, 0))] + [_const_spec(a.shape) for a in args[1:]]
    big = pltpu.VMEM((tm, d), F32)
    lc = RWKV_CHUNK
    return pl.pallas_call(
        functools.partial(_rwkv_body, tiles_per_seq=seq // tm),
        out_shape=jax.ShapeDtypeStruct((rows, d), F32),
        grid=(rows // tm,),
        in_specs=in_specs,
        out_specs=pl.BlockSpec((tm, d), lambda i: (i, 0)),
        scratch_shapes=[pltpu.VMEM((tm + 2 * SUBLANES, d), F32),
                        pltpu.VMEM((RWKV_HEADS, RWKV_HEAD_DIM, RWKV_HEAD_DIM), F32),
                        big, big, big, big, big, big, big, big, big,
                        ] + 2 * [pltpu.VMEM((3, 2 * lc, d), F32),
                                 pltpu.VMEM((RWKV_HEADS, lc, lc), BF16),
                                 pltpu.VMEM((RWKV_HEADS, lc, RWKV_HEAD_DIM), F32),
                                 pltpu.VMEM((RWKV_HEADS, lc, 2 * lc), BF16),
                                 pltpu.VMEM((SUBLANES, d), F32)],
        compiler_params=_params(1),
        name="rwkv7",
    )(*args)


def _ssd_in_body(x_ref, g_ref, w_ref, wdt_ref, zx_ref, dt_ref, *, nc):
    hn = _rms(x_ref[...], g_ref[...])
    h = hn.astype(BF16)
    for n in range(w_ref.shape[1] // nc):
        zx_ref[:, n * nc:(n + 1) * nc] = jnp.dot(h, w_ref[:, n * nc:(n + 1) * nc],
                                                 preferred_element_type=F32)
    h_lo = (hn - h.astype(F32)).astype(BF16)
    w_hi, w_lo = _split2(wdt_ref[...])
    dt_ref[...] = (jnp.dot(h, w_hi, preferred_element_type=F32)
                   + jnp.dot(h, w_lo, preferred_element_type=F32)
                   + jnp.dot(h_lo, w_hi, preferred_element_type=F32))


def _ssd_core_body(zx_ref, dt_ref, cw_ref, cb_ref, dtb_ref, alog_ref, dskip_ref, nw_ref, e_ref, o_ref,
                   cbuf_ref, state_ref, y_s):
    lc = SSD_CHUNK
    di = SSD_D_INNER
    hs = SUBLANES
    gw = di // SSD_GROUPS
    ns = SSD_STATE

    @pl.when(pl.program_id(1) == 0)
    def _():
        cbuf_ref[lc:lc + hs, :] = jnp.zeros((hs, SSD_CONV_DIM), F32)
        state_ref[...] = jnp.zeros_like(state_ref)

    z = zx_ref[:, 0:di]
    xbc_raw = zx_ref[:, di:di + SSD_CONV_DIM]
    cbuf_ref[0:hs, :] = cbuf_ref[lc:lc + hs, :]
    cbuf_ref[hs:hs + lc, :] = xbc_raw
    xbc = cb_ref[...] + cw_ref[SSD_CONV - 1:SSD_CONV, :] * xbc_raw
    win = cbuf_ref[0:hs + lc, :]
    for j in range(SSD_CONV - 1):
        back = SSD_CONV - 1 - j
        xbc = xbc + cw_ref[j:j + 1, :] * pltpu.roll(win, back, 0)[hs:hs + lc, :]
    xbc = jax.nn.silu(xbc)
    xs = xbc[:, 0:di]
    bm = xbc[:, di:di + SSD_GROUPS * ns]
    cm = xbc[:, di + SSD_GROUPS * ns:]

    dt = jax.nn.softplus(dt_ref[...] + dtb_ref[...])
    adt = dt * (-jnp.exp(alog_ref[...]))
    li = lax.broadcasted_iota(jnp.int32, (lc, lc), 0)
    si = lax.broadcasted_iota(jnp.int32, (lc, lc), 1)
    causal = si <= li
    first_head = lax.broadcasted_iota(jnp.int32, (lc, 2 * SSD_HEAD_DIM), 1) < SSD_HEAD_DIM
    tril = jnp.where(causal, 1.0, 0.0).astype(BF16)
    acs = _cumsum_rows(jnp.concatenate([tril, tril, tril], axis=1), adt)
    acs_t = acs.T
    a_last = acs[lc - 1:lc, :]

    def expand(v):
        return _dot_hilo(v, e_ref[...])

    exp_acs_x = expand(jnp.exp(acs))
    xd = xs * expand(dt)
    xdec = xd * expand(jnp.exp(a_last - acs))

    for g in range(SSD_GROUPS):
        bg = bm[:, g * ns:(g + 1) * ns]
        cg = cm[:, g * ns:(g + 1) * ns]
        cb = _dot_nt(cg, bg)
        gsl = slice(g * gw, (g + 1) * gw)
        st_g = state_ref[:, gsl]
        y_off = _dot(cg, st_g) * exp_acs_x[:, gsl]
        def masked_decay(h):
            col = jnp.broadcast_to(acs[:, h:h + 1], (lc, lc))
            row = jnp.broadcast_to(acs_t[h:h + 1, :], (lc, lc))
            return (cb * jnp.where(causal, jnp.exp(col - row), 0.0)).astype(BF16)

        for e in range(0, SSD_HEADS // SSD_GROUPS, 2):
            h = g * (SSD_HEADS // SSD_GROUPS) + e
            psl = slice(h * SSD_HEAD_DIM, (h + 2) * SSD_HEAD_DIM)
            xdp = xd[:, psl]
            xd_bd = jnp.concatenate([jnp.where(first_head, xdp, 0.0), jnp.where(first_head, 0.0, xdp)], axis=0)
            o0 = e * SSD_HEAD_DIM
            y_s[:, psl] = (jnp.dot(jnp.concatenate([masked_decay(h), masked_decay(h + 1)], axis=1),
                                   xd_bd.astype(BF16), preferred_element_type=F32)
                           + y_off[:, o0:o0 + 2 * SSD_HEAD_DIM])
        state_ref[:, gsl] = st_g * exp_acs_x[lc - 1:lc, gsl] + _dot_tn(bg, xdec[:, gsl])

    y = y_s[...] + xs * dskip_ref[...]
    yz = y * jax.nn.silu(z)
    for g in range(SSD_GROUPS):
        gsl = slice(g * gw, (g + 1) * gw)
        yg = yz[:, gsl]
        o_ref[:, gsl] = (yg * lax.rsqrt(jnp.mean(yg * yg, axis=-1, keepdims=True) + SSD_NORM_EPS)
                         * nw_ref[:, gsl])


def _mamba2(x2, batch, seq, g, w_in, conv_w, conv_b, dt_bias, a_log, d_skip, norm_w, w_out, tm=512):
    rows = x2.shape[0]
    d = D_MODEL
    di = SSD_D_INNER
    nzx = di + SSD_CONV_DIM
    zx, dt = pl.pallas_call(
        functools.partial(_ssd_in_body, nc=512),
        out_shape=(jax.ShapeDtypeStruct((rows, nzx), F32), jax.ShapeDtypeStruct((rows, SSD_HEADS), F32)),
        grid=(rows // tm,),
        in_specs=[pl.BlockSpec((tm, d), lambda i: (i, 0)),
                  _const_spec((1, d)),
                  _const_spec((d, nzx)),
                  _const_spec((d, SSD_HEADS))],
        out_specs=(pl.BlockSpec((tm, nzx), lambda i: (i, 0)),
                   pl.BlockSpec((tm, SSD_HEADS), lambda i: (i, 0))),
        compiler_params=_params(1),
        name="ssd_in",
    )(x2, _row(g), w_in[:, :nzx].astype(BF16), w_in[:, nzx:].astype(F32))

    lc = SSD_CHUNK
    nchunk = seq // lc
    expand = (jnp.arange(SSD_HEADS)[:, None] == (jnp.arange(di) // SSD_HEAD_DIM)[None, :]).astype(BF16)
    d_x = jnp.repeat(d_skip.astype(F32), SSD_HEAD_DIM).reshape(1, di)
    args = (zx, dt, conv_w.astype(F32), _row(conv_b), _row(dt_bias), _row(a_log), d_x, _row(norm_w),
            jnp.concatenate([expand, expand], axis=0))
    yz = pl.pallas_call(
        _ssd_core_body,
        out_shape=jax.ShapeDtypeStruct((rows, di), F32),
        grid=(batch, nchunk),
        in_specs=[pl.BlockSpec((lc, nzx), lambda b, c: (b * nchunk + c, 0)),
                  pl.BlockSpec((lc, SSD_HEADS), lambda b, c: (b * nchunk + c, 0))]
                 + [_const_spec(a.shape) for a in args[2:]],
        out_specs=pl.BlockSpec((lc, di), lambda b, c: (b * nchunk + c, 0)),
        scratch_shapes=[pltpu.VMEM((lc + 2 * SUBLANES, SSD_CONV_DIM), F32),
                        pltpu.VMEM((SSD_STATE, di), F32),
                        pltpu.VMEM((lc, di), F32)],
        compiler_params=_params(2),
        name="ssd_core",
    )(*args)
    return _proj_res(x2, yz, w_out)


def _conf_body(x_ref, g_ref, w1_ref, b1_ref, dw_ref, dwb_ref, lnw_ref, lnb_ref, w2_ref, b2_ref, o_ref,
               ubuf_ref, c_s, *, tiles_per_seq, rb, lanes):
    tm = x_ref.shape[0]
    d = D_MODEL
    hl = CONF_HALO

    @pl.when(pl.program_id(0) % tiles_per_seq == 0)
    def _():
        ubuf_ref[tm:tm + hl, :] = jnp.zeros((hl, d), F32)

    x = x_ref[...]
    h = _rms(x, g_ref[...]).astype(BF16)
    p = jnp.dot(h, w1_ref[...], preferred_element_type=F32) + b1_ref[...]
    u = p[:, 0:d] * jax.nn.sigmoid(p[:, d:2 * d])
    ubuf_ref[0:hl, :] = ubuf_ref[tm:tm + hl, :]
    ubuf_ref[hl:hl + tm, :] = u
    base = hl - (CONF_KERNEL - 1)
    for r0 in range(0, tm, rb):
        for c0 in range(0, d, lanes):
            acc = jnp.broadcast_to(dwb_ref[:, c0:c0 + lanes], (rb, lanes))
            win = ubuf_ref[r0:r0 + rb + hl, c0:c0 + lanes]
            for res in range(SUBLANES):
                offs = [o for o in range(base, base + CONF_KERNEL) if o % SUBLANES == res]
                slab = win if res == 0 else pltpu.roll(win, rb + hl - res, 0)
                for o in offs:
                    a0 = o - res
                    acc = acc + dw_ref[o - base:o - base + 1, c0:c0 + lanes] * slab[a0:a0 + rb, :]
            c_s[r0:r0 + rb, c0:c0 + lanes] = acc
    y = c_s[...]
    mu = jnp.mean(y, axis=-1, keepdims=True)
    dy = y - mu
    var = jnp.mean(dy * dy, axis=-1, keepdims=True)
    zn = dy * lax.rsqrt(var + CONF_LN_EPS) * lnw_ref[...] + lnb_ref[...]
    o_ref[...] = x + _dot(jax.nn.silu(zn), w2_ref[...]) + b2_ref[...]


def _conformer(x2, seq, g, w_pw1, b_pw1, dw_w, dw_b, ln_w, ln_b, w_pw2, b_pw2, tm=256):
    rows = x2.shape[0]
    d = D_MODEL
    args = (x2, _row(g), w_pw1.astype(BF16), _row(b_pw1), dw_w.astype(F32), _row(dw_b), _row(ln_w),
            _row(ln_b), w_pw2.astype(BF16), _row(b_pw2))
    return pl.pallas_call(
        functools.partial(_conf_body, tiles_per_seq=seq // tm, rb=128, lanes=128),
        out_shape=jax.ShapeDtypeStruct((rows, d), F32),
        grid=(rows // tm,),
        in_specs=[pl.BlockSpec((tm, d), lambda i: (i, 0))] + [_const_spec(a.shape) for a in args[1:]],
        out_specs=pl.BlockSpec((tm, d), lambda i: (i, 0)),
        scratch_shapes=[pltpu.VMEM((tm + CONF_HALO, d), F32), pltpu.VMEM((tm, d), F32)],
        compiler_params=_params(1),
        name="conformer",
    )(*args)


def kernel(x, norm_mix, norm_ffn, norm_final, ffn_w_up, ffn_conv_w, ffn_conv_b, ffn_w_down, moba_w_qkv, moba_w_o, rwkv_mu, rwkv_w_rkv, rwkv_w0, rwkv_w1, rwkv_w2, rwkv_a0, rwkv_a1, rwkv_a2, rwkv_g1, rwkv_g2, rwkv_k_k, rwkv_k_a, rwkv_r_k, rwkv_gn_w, rwkv_gn_b, rwkv_w_o, ssd_w_in, ssd_conv_w, ssd_conv_b, ssd_dt_bias, ssd_a_log, ssd_d, ssd_norm_w, ssd_w_out, conf_w_pw1, conf_b_pw1, conf_dw_w, conf_dw_b, conf_ln_w, conf_ln_b, conf_w_pw2, conf_b_pw2):
    b, s, d = x.shape
    depth = norm_mix.shape[0]
    rows = b * s
    for i in range(depth):
        kind, j = i % 4, i // 4
        if kind == 0:
            x2 = _moba(x.reshape(b, s, d), norm_mix[i], moba_w_qkv[j], moba_w_o[j]).reshape(rows, d)
        elif kind == 1:
            x2 = _rwkv7(x.reshape(rows, d), s, norm_mix[i], rwkv_mu[j], rwkv_w_rkv[j], rwkv_w0[j], rwkv_w1[j],
                        rwkv_w2[j], rwkv_a0[j], rwkv_a1[j], rwkv_a2[j], rwkv_g1[j], rwkv_g2[j], rwkv_k_k[j],
                        rwkv_k_a[j], rwkv_r_k[j], rwkv_gn_w[j], rwkv_gn_b[j], rwkv_w_o[j])
        elif kind == 2:
            x2 = _mamba2(x.reshape(rows, d), b, s, norm_mix[i], ssd_w_in[j], ssd_conv_w[j], ssd_conv_b[j],
                         ssd_dt_bias[j], ssd_a_log[j], ssd_d[j], ssd_norm_w[j], ssd_w_out[j])
        else:
            x2 = _conformer(x.reshape(rows, d), s, norm_mix[i], conf_w_pw1[j], conf_b_pw1[j], conf_dw_w[j],
                            conf_dw_b[j], conf_ln_w[j], conf_ln_b[j], conf_w_pw2[j], conf_b_pw2[j])
        x = _conv_ffn(x2, s, norm_ffn[i], ffn_w_up[i], ffn_conv_w[i], ffn_conv_b[i], ffn_w_down[i],
                      norm_final, final=(i == depth - 1))
    return x.reshape(b, s, d)
```

```python
import functools

import jax
import jax.numpy as jnp
from jax import lax
from jax.experimental import pallas as pl
from jax.experimental.pallas import tpu as pltpu

F32 = jnp.float32
BF16 = jnp.bfloat16

D_MODEL = 1024
NORM_EPS = 1e-6
NEG_INF = -1e30
LOG2E = 1.4426950408889634

MOBA_HEADS = 16
MOBA_HEAD_DIM = 64
MOBA_BLOCK = 256
MOBA_TOPK = 3
MOBA_ONES_ROWS = 16
MOBA_HEAD_GROUP = 8

RWKV_HEADS = 16
RWKV_HEAD_DIM = 64
RWKV_GN_EPS = 64e-5
RWKV_CHUNK = 64

SSD_D_INNER = 2048
SSD_HEAD_DIM = 64
SSD_HEADS = 32
SSD_GROUPS = 4
SSD_STATE = 128
SSD_CONV = 4
SSD_CHUNK = 128
SSD_CONV_DIM = SSD_D_INNER + 2 * SSD_GROUPS * SSD_STATE
SSD_NORM_EPS = 1e-5

CONF_KERNEL = 31
CONF_LN_EPS = 1e-5
CONF_HALO = 32

D_FF = 2816
FFN_CONV = 3
FFN_CHUNK = 256

SUBLANES = 8
VMEM_LIMIT = 56 * 1024 * 1024


def _rms(x, g, eps=NORM_EPS):
    return x * lax.rsqrt(jnp.mean(x * x, axis=-1, keepdims=True) + eps) * g


def _dot(a, b):
    return jnp.dot(a.astype(BF16), b.astype(BF16), preferred_element_type=F32)


def _dot_nt(a, b):
    return lax.dot_general(a.astype(BF16), b.astype(BF16), (((1,), (1,)), ((), ())),
                           preferred_element_type=F32)


def _dot_tn(a, b):
    return lax.dot_general(a.astype(BF16), b.astype(BF16), (((0,), (0,)), ((), ())),
                           preferred_element_type=F32)


def _split2(a):
    hi = a.astype(BF16)
    lo = (a - hi.astype(F32)).astype(BF16)
    return hi, lo


def _split3(a):
    hi = a.astype(BF16)
    r = a - hi.astype(F32)
    mid = r.astype(BF16)
    lo = (r - mid.astype(F32)).astype(BF16)
    return hi, mid, lo


def _cumsum_rows(tril3, b):
    return jnp.dot(tril3, jnp.concatenate(_split3(b), axis=0), preferred_element_type=F32)


def _dot_hilo(a, b2_bf16):
    return jnp.dot(jnp.concatenate(_split2(a), axis=1), b2_bf16, preferred_element_type=F32)


def _const_spec(shape):
    n = len(shape)
    return pl.BlockSpec(shape, lambda *_: (0,) * n, pipeline_mode=pl.Buffered(1))


def _params(n_axes):
    return pltpu.CompilerParams(dimension_semantics=("arbitrary",) * n_axes,
                                vmem_limit_bytes=VMEM_LIMIT)


def _row(v):
    return v.reshape(1, -1).astype(F32)


def _ffn_body(x_ref, g_ref, wup_ref, cw_ref, cb_ref, wdn_ref, gf_ref, o_ref, halo_ref, act_ref,
              *, tiles_per_seq, final):
    tm = x_ref.shape[0]
    fc = FFN_CHUNK
    hs = SUBLANES

    @pl.when(pl.program_id(0) % tiles_per_seq == 0)
    def _():
        halo_ref[...] = jnp.zeros_like(halo_ref)

    x = x_ref[...]
    h = _rms(x, g_ref[...]).astype(BF16)

    r8 = lax.broadcasted_iota(jnp.int32, (hs, fc), 0)

    def conv(u, off):
        prev = halo_ref[:, off:off + fc]
        halo_ref[:, off:off + fc] = u[tm - hs:tm, :]
        w0 = cw_ref[0:1, off:off + fc]
        w1 = cw_ref[1:2, off:off + fc]
        u1 = pltpu.roll(u, 1, 0)
        u2 = pltpu.roll(u, 2, 0)
        y = w0 * u2 + w1 * u1 + cw_ref[2:3, off:off + fc] * u + cb_ref[0:1, off:off + fc]
        fix = (w0 * jnp.where(r8 < 2, pltpu.roll(prev, 2, 0) - u2[0:hs, :], 0.0)
               + w1 * jnp.where(r8 < 1, pltpu.roll(prev, 1, 0) - u1[0:hs, :], 0.0))
        return jnp.concatenate([y[0:hs, :] + fix, y[hs:, :]], axis=0)

    def up(c):
        og, ou = c * fc, D_FF + c * fc
        return (jnp.dot(h, wup_ref[:, og:og + fc], preferred_element_type=F32),
                jnp.dot(h, wup_ref[:, ou:ou + fc], preferred_element_type=F32))

    n_chunks = D_FF // fc
    nxt = up(0)
    for c in range(n_chunks):
        ug, uu = nxt
        if c + 1 < n_chunks:
            nxt = up(c + 1)
        yg = conv(ug, c * fc)
        yu = conv(uu, D_FF + c * fc)
        act_ref[:, c * fc:(c + 1) * fc] = (jax.nn.silu(yg) * yu).astype(BF16)
    y = x + jnp.dot(act_ref[...], wdn_ref[...], preferred_element_type=F32)
    if final:
        y = _rms(y, gf_ref[...])
    o_ref[...] = y


def _conv_ffn(x2, seq, g, w_up, conv_w, conv_b, w_down, g_final, final, tm=512):
    rows = x2.shape[0]
    assert seq % tm == 0 and D_FF % FFN_CHUNK == 0
    body = functools.partial(_ffn_body, tiles_per_seq=seq // tm, final=final)
    return pl.pallas_call(
        body,
        out_shape=jax.ShapeDtypeStruct((rows, D_MODEL), F32),
        grid=(rows // tm,),
        in_specs=[
            pl.BlockSpec((tm, D_MODEL), lambda i: (i, 0)),
            _const_spec((1, D_MODEL)),
            _const_spec((D_MODEL, 2 * D_FF)),
            _const_spec((FFN_CONV, 2 * D_FF)),
            _const_spec((1, 2 * D_FF)),
            _const_spec((D_FF, D_MODEL)),
            _const_spec((1, D_MODEL)),
        ],
        out_specs=pl.BlockSpec((tm, D_MODEL), lambda i: (i, 0)),
        scratch_shapes=[pltpu.VMEM((SUBLANES, 2 * D_FF), F32), pltpu.VMEM((tm, D_FF), BF16)],
        compiler_params=_params(1),
        name="conv_ffn",
    )(x2, _row(g), w_up.astype(BF16), conv_w.astype(F32), _row(conv_b), w_down.astype(BF16), _row(g_final))


def _proj_res_body(x_ref, y_ref, w_ref, o_ref):
    o_ref[...] = x_ref[...] + jnp.dot(y_ref[...].astype(BF16), w_ref[...], preferred_element_type=F32)


def _proj_res(x2, y2, w, tm=512):
    rows, k = y2.shape
    assert rows % tm == 0
    return pl.pallas_call(
        _proj_res_body,
        out_shape=jax.ShapeDtypeStruct((rows, D_MODEL), F32),
        grid=(rows // tm,),
        in_specs=[pl.BlockSpec((tm, D_MODEL), lambda i: (i, 0)),
                  pl.BlockSpec((tm, k), lambda i: (i, 0)),
                  _const_spec((k, D_MODEL))],
        out_specs=pl.BlockSpec((tm, D_MODEL), lambda i: (i, 0)),
        compiler_params=_params(1),
        name="proj_res",
    )(x2, y2, w.astype(BF16))


def _moba_qkv_body(x_ref, g_ref, wt_ref, o_ref, *, nc):
    h = _rms(x_ref[0], g_ref[...]).astype(BF16)
    for n in range(3 * D_MODEL // nc):
        o_ref[0, n * nc:(n + 1) * nc, :] = lax.dot_general(
            wt_ref[n * nc:(n + 1) * nc, :], h, (((1,), (1,)), ((), ())),
            preferred_element_type=F32).astype(BF16)


def _moba_attn_body(q_ref, k_ref, v_ref, o_ref, kaug_ref, vblk_ref, vpair_ref, kmean_ref, *, nb):
    bs = MOBA_BLOCK
    dh = MOBA_HEAD_DIM
    hg = MOBA_HEAD_GROUP
    hh = range(hg)
    iq = pl.program_id(2)

    @pl.when(iq == 0)
    def _():
        blk = lax.broadcasted_iota(jnp.int32, (bs, nb), 1)
        for j in range(nb):
            kr = k_ref[0, :, j * bs:(j + 1) * bs].astype(F32).T
            onehot = jnp.where(blk == j, 1.0, 0.0).astype(BF16)
            for i in hh:
                kri = kr[:, i * dh:(i + 1) * dh]
                kaug_ref[i, j // 2, (j % 2) * bs:(j % 2 + 1) * bs, :] = jnp.concatenate(
                    [kri.astype(BF16), onehot], axis=1)
                kmean_ref[i, j:j + 1, :] = jnp.mean(kri, axis=0, keepdims=True)
                vblk_ref[i, j, 0:dh, :] = v_ref[0, i * dh:(i + 1) * dh, j * bs:(j + 1) * bs]
                vblk_ref[i, j, dh:dh + MOBA_ONES_ROWS, :] = jnp.ones((MOBA_ONES_ROWS, bs), BF16)
                if j % 2 == 0:
                    vpair_ref[i, j // 2, 0:dh, :] = v_ref[0, i * dh:(i + 1) * dh, j * bs:(j + 2) * bs]
                    vpair_ref[i, j // 2, dh:dh + MOBA_ONES_ROWS, :] = jnp.ones((MOBA_ONES_ROWS, 2 * bs), BF16)

    jidx = lax.broadcasted_iota(jnp.int32, (nb, bs), 0)
    qs, qaug = [], []
    for i in hh:
        q_raw = q_ref[0, i * dh:(i + 1) * dh, :]
        km_hi, km_lo = _split2(kmean_ref[i])
        gate = (jnp.dot(km_hi, q_raw, preferred_element_type=F32)
                + jnp.dot(km_lo, q_raw, preferred_element_type=F32))
        cnt = jnp.zeros((nb, bs), jnp.int32)
        for jp in range(nb):
            gj = gate[jp:jp + 1, :]
            beats = (gj > gate) | ((gj == gate) & (jp < jidx))
            cnt = cnt + jnp.where(beats & (jp < iq), 1, 0)
        selm = (cnt < MOBA_TOPK) & (jidx < iq)
        qs.append(q_raw * (dh ** -0.5 * LOG2E))
        qaug.append(jnp.concatenate([qs[i], jnp.where(selm, 0.0, NEG_INF).astype(BF16)], axis=0))

    kpos = lax.broadcasted_iota(jnp.int32, (bs, bs), 0)
    qpos = lax.broadcasted_iota(jnp.int32, (bs, bs), 1)
    causal = kpos <= qpos
    own_rows = pl.ds(pl.multiple_of((iq % 2) * bs, bs), bs)
    s = [jnp.where(causal, jnp.dot(kaug_ref[i, iq // 2, own_rows, :][:, 0:dh], qs[i],
                                   preferred_element_type=F32), NEG_INF)
         for i in hh]
    m0 = [jnp.max(s[i], axis=0, keepdims=True) for i in hh]
    p = [jnp.exp2(s[i] - m0[i]) for i in hh]
    acc0 = [jnp.dot(vblk_ref[i, iq], p[i].astype(BF16), preferred_element_type=F32) for i in hh]

    def step(j, carry):
        m, acc = carry
        sj = [jnp.dot(kaug_ref[i, j], qaug[i], preferred_element_type=F32) for i in hh]
        m_new, acc_new = [], []
        for i in hh:
            mi = jnp.maximum(m[i], jnp.max(sj[i], axis=0, keepdims=True))
            pj = jnp.exp2(sj[i] - mi).astype(BF16)
            acc_new.append(jnp.exp2(m[i] - mi) * acc[i]
                           + jnp.dot(vpair_ref[i, j], pj, preferred_element_type=F32))
            m_new.append(mi)
        return m_new, acc_new

    _, acc = lax.fori_loop(0, (iq + 1) // 2, step, (m0, acc0))
    for i in hh:
        o_ref[0, i * dh:(i + 1) * dh, :] = (acc[i][0:dh, :] / acc[i][dh:dh + 1, :]).astype(BF16)


def _moba_out_body(x_ref, o_ref_in, w_ref, o_ref):
    o_ref[0] = x_ref[0] + lax.dot_general(o_ref_in[0], w_ref[...], (((0,), (0,)), ((), ())),
                                          preferred_element_type=F32)


def _moba(x, g, w_qkv, w_o, ts=512):
    b, s, d = x.shape
    nb = s // MOBA_BLOCK
    dh = MOBA_HEAD_DIM
    assert s % ts == 0 and s % MOBA_BLOCK == 0 and nb % 2 == 0
    qkvt = pl.pallas_call(
        functools.partial(_moba_qkv_body, nc=512),
        out_shape=jax.ShapeDtypeStruct((b, 3 * d, s), BF16),
        grid=(b, s // ts),
        in_specs=[pl.BlockSpec((1, ts, d), lambda i, j: (i, j, 0)),
                  _const_spec((1, d)),
                  _const_spec((3 * d, d))],
        out_specs=pl.BlockSpec((1, 3 * d, ts), lambda i, j: (i, 0, j)),
        compiler_params=_params(2),
        name="moba_qkv",
    )(x, _row(g), w_qkv.T.astype(BF16))

    hg = MOBA_HEAD_GROUP
    ng = MOBA_HEADS // hg
    gd = hg * dh
    ot = pl.pallas_call(
        functools.partial(_moba_attn_body, nb=nb),
        out_shape=jax.ShapeDtypeStruct((b, d, s), BF16),
        grid=(b, ng, nb),
        in_specs=[pl.BlockSpec((1, gd, MOBA_BLOCK), lambda i, j, k: (i, j, k)),
                  pl.BlockSpec((1, gd, s), lambda i, j, k: (i, ng + j, 0)),
                  pl.BlockSpec((1, gd, s), lambda i, j, k: (i, 2 * ng + j, 0))],
        out_specs=pl.BlockSpec((1, gd, MOBA_BLOCK), lambda i, j, k: (i, j, k)),
        scratch_shapes=[pltpu.VMEM((hg, nb // 2, 2 * MOBA_BLOCK, dh + nb), BF16),
                        pltpu.VMEM((hg, nb, dh + MOBA_ONES_ROWS, MOBA_BLOCK), BF16),
                        pltpu.VMEM((hg, nb // 2, dh + MOBA_ONES_ROWS, 2 * MOBA_BLOCK), BF16),
                        pltpu.VMEM((hg, nb, dh), F32)],
        compiler_params=_params(3),
        name="moba_attn",
    )(qkvt, qkvt, qkvt)

    return pl.pallas_call(
        _moba_out_body,
        out_shape=jax.ShapeDtypeStruct((b, s, d), F32),
        grid=(b, s // ts),
        in_specs=[pl.BlockSpec((1, ts, d), lambda i, j: (i, j, 0)),
                  pl.BlockSpec((1, d, ts), lambda i, j: (i, 0, j)),
                  _const_spec((d, d))],
        out_specs=pl.BlockSpec((1, ts, d), lambda i, j: (i, j, 0)),
        compiler_params=_params(2),
        name="moba_out",
    )(x, ot, w_o.astype(BF16))


def _rwkv_body(x_ref, g_ref, mu_ref, wr_ref, wk_ref, wv_ref, w0_ref, w1_ref, w2_ref, a0_ref, a1_ref,
               a2_ref, g1_ref, g2_ref, kk_ref, ka_ref, rk_ref, gnw_ref, gnb_ref, wo_ref, p_ref, pt_ref,
               o_ref,
               hbuf_ref, st_ref, r_s, lw_s, k_s, v_s, a_s, b_s, y_s, gate_s, bonus_s,
               ch0, t0, mv0, mr0, gt0, ch1, t1, mv1, mr1, gt1, *, tiles_per_seq):
    bufs = ((ch0, t0, mv0, mr0, gt0), (ch1, t1, mv1, mr1, gt1))
    tm = x_ref.shape[0]
    lc = RWKV_CHUNK
    hd = RWKV_HEAD_DIM
    hs = SUBLANES

    @pl.when(pl.program_id(0) % tiles_per_seq == 0)
    def _():
        hbuf_ref[tm:tm + hs, :] = jnp.zeros((hs, D_MODEL), F32)
        st_ref[...] = jnp.zeros_like(st_ref)

    def seg_sum(v):
        return jnp.dot(v.astype(BF16), p_ref[...], preferred_element_type=F32)

    def seg_bcast(v):
        return _dot_hilo(v, pt_ref[...])

    x = x_ref[...]
    hn = _rms(x, g_ref[...])
    hbuf_ref[0:hs, :] = hbuf_ref[tm:tm + hs, :]
    hbuf_ref[hs:hs + tm, :] = hn
    xx = hbuf_ref[hs - 1:hs - 1 + tm, :] - hn

    def mix(i):
        return hn + xx * mu_ref[i:i + 1, :]

    r = _dot(mix(0), wr_ref[...])
    k = _dot(mix(1), wk_ref[...])
    v = _dot(mix(2), wv_ref[...])
    wpre = w0_ref[...] + _dot(jnp.tanh(_dot(mix(3), w1_ref[...])), w2_ref[...])
    lw = -jnp.exp(-jax.nn.softplus(-wpre) - 0.5)
    a = jax.nn.sigmoid(a0_ref[...] + _dot(_dot(mix(4), a1_ref[...]), a2_ref[...]))
    gate_s[...] = _dot(jax.nn.sigmoid(_dot(mix(5), g1_ref[...])), g2_ref[...])

    kk = k * kk_ref[...]
    inv = 1.0 / jnp.maximum(jnp.sqrt(seg_sum(kk * kk)), 1e-12)
    kk = kk * seg_bcast(inv)
    k = k * (1.0 + (a - 1.0) * ka_ref[...])
    bonus_s[...] = seg_bcast(seg_sum(r * k * rk_ref[...])) * v
    r_s[...] = r
    lw_s[...] = lw
    k_s[...] = k
    v_s[...] = v
    a_s[...] = -kk
    b_s[...] = kk * a

    ti = lax.broadcasted_iota(jnp.int32, (lc, lc), 0)
    si = lax.broadcasted_iota(jnp.int32, (lc, lc), 1)
    tril = jnp.where(si <= ti, 1.0, 0.0).astype(BF16)
    tril3 = jnp.concatenate([tril, tril, tril], axis=1)
    low_strict = si < ti
    ti2 = lax.broadcasted_iota(jnp.int32, (lc, 2 * lc), 0)
    si2 = lax.broadcasted_iota(jnp.int32, (lc, 2 * lc), 1)
    low_incl2 = jnp.where(si2 >= lc, si2 - lc, si2) <= ti2
    keep_u = si2 >= lc
    low_strict2 = si2 < ti2

    hh = range(RWKV_HEADS)
    sls = [slice(h * hd, (h + 1) * hd) for h in hh]
    eye2 = jnp.where(si2 - lc == ti2, 1.0, 0.0)

    def prepare(c):
        ch, t_s, mv_s, mr_s, gt_s = bufs[c % 2]
        rows = slice(c * lc, (c + 1) * lc)
        lw_c = lw_s[rows, :]
        lg = _cumsum_rows(tril3, lw_c)
        lg_last = lg[lc - 1:lc, :]
        g_inv = jnp.exp(-lg)
        g_tail = jnp.exp(lg_last - lg)
        gt_s[...] = jnp.broadcast_to(jnp.exp(lg_last), gt_s.shape)
        ch[0, 0:lc, :] = a_s[rows, :] * jnp.exp(lg - lw_c)
        ch[0, lc:2 * lc, :] = r_s[rows, :] * jnp.exp(lg)
        ch[1, 0:lc, :] = b_s[rows, :] * g_inv
        ch[1, lc:2 * lc, :] = k_s[rows, :] * g_inv
        ch[2, 0:lc, :] = b_s[rows, :] * g_tail
        ch[2, lc:2 * lc, :] = k_s[rows, :] * g_tail
        gm = [_dot_nt(ch[0, :, sls[h]], ch[1, :, sls[h]]) for h in hh]
        for h in hh:
            mr_s[h] = jnp.where(low_incl2, gm[h][lc:2 * lc, :], 0.0).astype(BF16)
            mv_s[h] = _dot(jnp.where(low_strict, gm[h][0:lc, lc:2 * lc], 0.0), v_s[rows, sls[h]])
        w = [jnp.where(keep_u, eye2, jnp.where(low_strict2, gm[h][0:lc, :], 0.0)) for h in hh]
        steps = 1
        while steps < lc:
            r = [_dot(w[h][:, 0:lc], w[h]) for h in hh]
            w = [r[h] + jnp.where(keep_u, w[h], 0.0) for h in hh]
            steps *= 2
        for h in hh:
            t_s[h] = w[h][:, lc:].astype(BF16)

    def advance(c):
        ch, t_s, mv_s, mr_s, gt_s = bufs[c % 2]
        rows = slice(c * lc, (c + 1) * lc)
        sh = [st_ref[h] for h in hh]
        ars = [_dot_nt(ch[0, :, sls[h]], sh[h]) for h in hh]
        u = [jnp.dot(t_s[h], (ars[h][0:lc, :] + mv_s[h]).astype(BF16), preferred_element_type=F32)
             for h in hh]
        uv = [jnp.concatenate([u[h].astype(BF16), v_s[rows, sls[h]].astype(BF16)], axis=0) for h in hh]
        for h in hh:
            y_s[rows, sls[h]] = ars[h][lc:2 * lc, :] + jnp.dot(mr_s[h], uv[h], preferred_element_type=F32)
        for h in hh:
            st_ref[h] = sh[h] * gt_s[0:1, sls[h]] + _dot_tn(uv[h], ch[2, :, sls[h]])

    n_chunks = tm // lc
    prepare(0)
    for c in range(n_chunks):
        if c + 1 < n_chunks:
            prepare(c + 1)
        advance(c)

    y = y_s[...]
    mean = seg_bcast(seg_sum(y)) * (1.0 / hd)
    dy = y - mean
    var = seg_bcast(seg_sum(dy * dy)) * (1.0 / hd)
    yn = dy * lax.rsqrt(var + RWKV_GN_EPS) * gnw_ref[...] + gnb_ref[...]
    o_ref[...] = x + _dot((yn + bonus_s[...]) * gate_s[...], wo_ref[...])


def _rwkv7(x2, seq, g, mu, w_rkv, w0, w1, w2, a0, a1, a2, g1, g2, k_k, k_a, r_k, gn_w, gn_b, w_o, tm=256):
    rows = x2.shape[0]
    d = D_MODEL
    assert seq % tm == 0 and tm % RWKV_CHUNK == 0 and RWKV_CHUNK == RWKV_HEAD_DIM
    head_of = jnp.arange(d) // RWKV_HEAD_DIM
    p = (head_of[:, None] == jnp.arange(128)[None, :]).astype(BF16)
    args = (x2, _row(g), mu.astype(F32), w_rkv[0].astype(BF16), w_rkv[1].astype(BF16), w_rkv[2].astype(BF16),
            _row(w0), w1.astype(BF16), w2.astype(BF16), _row(a0), a1.astype(BF16), a2.astype(BF16),
            g1.astype(BF16), g2.astype(BF16), _row(k_k), _row(k_a), _row(r_k), _row(gn_w), _row(gn_b),
            w_o.astype(BF16), p, jnp.concatenate([p.T, p.T], axis=0))
    in_specs = [pl.BlockSpec((tm, d), lambda i: (i, 0))] + [_const_spec(a.shape) for a in args[1:]]
    big = pltpu.VMEM((tm, d), F32)
    lc = RWKV_CHUNK
    return pl.pallas_call(
        functools.partial(_rwkv_body, tiles_per_seq=seq // tm),
        out_shape=jax.ShapeDtypeStruct((rows, d), F32),
        grid=(rows // tm,),
        in_specs=in_specs,
        out_specs=pl.BlockSpec((tm, d), lambda i: (i, 0)),
        scratch_shapes=[pltpu.VMEM((tm + 2 * SUBLANES, d), F32),
                        pltpu.VMEM((RWKV_HEADS, RWKV_HEAD_DIM, RWKV_HEAD_DIM), F32),
                        big, big, big, big, big, big, big, big, big,
                        ] + 2 * [pltpu.VMEM((3, 2 * lc, d), F32),
                                 pltpu.VMEM((RWKV_HEADS, lc, lc), BF16),
                                 pltpu.VMEM((RWKV_HEADS, lc, RWKV_HEAD_DIM), F32),
                                 pltpu.VMEM((RWKV_HEADS, lc, 2 * lc), BF16),
                                 pltpu.VMEM((SUBLANES, d), F32)],
        compiler_params=_params(1),
        name="rwkv7",
    )(*args)


def _ssd_in_body(x_ref, g_ref, w_ref, wdt_ref, zx_ref, dt_ref, *, nc):
    hn = _rms(x_ref[...], g_ref[...])
    h = hn.astype(BF16)
    for n in range(w_ref.shape[1] // nc):
        zx_ref[:, n * nc:(n + 1) * nc] = jnp.dot(h, w_ref[:, n * nc:(n + 1) * nc],
                                                 preferred_element_type=F32)
    h_lo = (hn - h.astype(F32)).astype(BF16)
    w_hi, w_lo = _split2(wdt_ref[...])
    dt_ref[...] = (jnp.dot(h, w_hi, preferred_element_type=F32)
                   + jnp.dot(h, w_lo, preferred_element_type=F32)
                   + jnp.dot(h_lo, w_hi, preferred_element_type=F32))


def _ssd_core_body(zx_ref, dt_ref, cw_ref, cb_ref, dtb_ref, alog_ref, dskip_ref, nw_ref, e_ref, o_ref,
                   cbuf_ref, state_ref, y_s):
    lc = SSD_CHUNK
    di = SSD_D_INNER
    hs = SUBLANES
    gw = di // SSD_GROUPS
    ns = SSD_STATE

    @pl.when(pl.program_id(1) == 0)
    def _():
        cbuf_ref[lc:lc + hs, :] = jnp.zeros((hs, SSD_CONV_DIM), F32)
        state_ref[...] = jnp.zeros_like(state_ref)

    z = zx_ref[:, 0:di]
    xbc_raw = zx_ref[:, di:di + SSD_CONV_DIM]
    cbuf_ref[0:hs, :] = cbuf_ref[lc:lc + hs, :]
    cbuf_ref[hs:hs + lc, :] = xbc_raw
    xbc = cb_ref[...] + cw_ref[SSD_CONV - 1:SSD_CONV, :] * xbc_raw
    win = cbuf_ref[0:hs + lc, :]
    for j in range(SSD_CONV - 1):
        back = SSD_CONV - 1 - j
        xbc = xbc + cw_ref[j:j + 1, :] * pltpu.roll(win, back, 0)[hs:hs + lc, :]
    xbc = jax.nn.silu(xbc)
    xs = xbc[:, 0:di]
    bm = xbc[:, di:di + SSD_GROUPS * ns]
    cm = xbc[:, di + SSD_GROUPS * ns:]

    dt = jax.nn.softplus(dt_ref[...] + dtb_ref[...])
    adt = dt * (-jnp.exp(alog_ref[...]))
    li = lax.broadcasted_iota(jnp.int32, (lc, lc), 0)
    si = lax.broadcasted_iota(jnp.int32, (lc, lc), 1)
    causal = si <= li
    first_head = lax.broadcasted_iota(jnp.int32, (lc, 2 * SSD_HEAD_DIM), 1) < SSD_HEAD_DIM
    tril = jnp.where(causal, 1.0, 0.0).astype(BF16)
    acs = _cumsum_rows(jnp.concatenate([tril, tril, tril], axis=1), adt)
    acs_t = acs.T
    a_last = acs[lc - 1:lc, :]

    def expand(v):
        return _dot_hilo(v, e_ref[...])

    exp_acs_x = expand(jnp.exp(acs))
    xd = xs * expand(dt)
    xdec = xd * expand(jnp.exp(a_last - acs))

    for g in range(SSD_GROUPS):
        bg = bm[:, g * ns:(g + 1) * ns]
        cg = cm[:, g * ns:(g + 1) * ns]
        cb = _dot_nt(cg, bg)
        gsl = slice(g * gw, (g + 1) * gw)
        st_g = state_ref[:, gsl]
        y_off = _dot(cg, st_g) * exp_acs_x[:, gsl]
        def masked_decay(h):
            col = jnp.broadcast_to(acs[:, h:h + 1], (lc, lc))
            row = jnp.broadcast_to(acs_t[h:h + 1, :], (lc, lc))
            return (cb * jnp.where(causal, jnp.exp(col - row), 0.0)).astype(BF16)

        for e in range(0, SSD_HEADS // SSD_GROUPS, 2):
            h = g * (SSD_HEADS // SSD_GROUPS) + e
            psl = slice(h * SSD_HEAD_DIM, (h + 2) * SSD_HEAD_DIM)
            xdp = xd[:, psl]
            xd_bd = jnp.concatenate([jnp.where(first_head, xdp, 0.0), jnp.where(first_head, 0.0, xdp)], axis=0)
            o0 = e * SSD_HEAD_DIM
            y_s[:, psl] = (jnp.dot(jnp.concatenate([masked_decay(h), masked_decay(h + 1)], axis=1),
                                   xd_bd.astype(BF16), preferred_element_type=F32)
                           + y_off[:, o0:o0 + 2 * SSD_HEAD_DIM])
        state_ref[:, gsl] = st_g * exp_acs_x[lc - 1:lc, gsl] + _dot_tn(bg, xdec[:, gsl])

    y = y_s[...] + xs * dskip_ref[...]
    yz = y * jax.nn.silu(z)
    for g in range(SSD_GROUPS):
        gsl = slice(g * gw, (g + 1) * gw)
        yg = yz[:, gsl]
        o_ref[:, gsl] = (yg * lax.rsqrt(jnp.mean(yg * yg, axis=-1, keepdims=True) + SSD_NORM_EPS)
                         * nw_ref[:, gsl])


def _mamba2(x2, batch, seq, g, w_in, conv_w, conv_b, dt_bias, a_log, d_skip, norm_w, w_out, tm=512):
    rows = x2.shape[0]
    d = D_MODEL
    di = SSD_D_INNER
    assert rows % tm == 0 and seq % SSD_CHUNK == 0 and SSD_CHUNK == 2 * SSD_HEAD_DIM
    nzx = di + SSD_CONV_DIM
    zx, dt = pl.pallas_call(
        functools.partial(_ssd_in_body, nc=512),
        out_shape=(jax.ShapeDtypeStruct((rows, nzx), F32), jax.ShapeDtypeStruct((rows, SSD_HEADS), F32)),
        grid=(rows // tm,),
        in_specs=[pl.BlockSpec((tm, d), lambda i: (i, 0)),
                  _const_spec((1, d)),
                  _const_spec((d, nzx)),
                  _const_spec((d, SSD_HEADS))],
        out_specs=(pl.BlockSpec((tm, nzx), lambda i: (i, 0)),
                   pl.BlockSpec((tm, SSD_HEADS), lambda i: (i, 0))),
        compiler_params=_params(1),
        name="ssd_in",
    )(x2, _row(g), w_in[:, :nzx].astype(BF16), w_in[:, nzx:].astype(F32))

    lc = SSD_CHUNK
    nchunk = seq // lc
    expand = (jnp.arange(SSD_HEADS)[:, None] == (jnp.arange(di) // SSD_HEAD_DIM)[None, :]).astype(BF16)
    d_x = jnp.repeat(d_skip.astype(F32), SSD_HEAD_DIM).reshape(1, di)
    args = (zx, dt, conv_w.astype(F32), _row(conv_b), _row(dt_bias), _row(a_log), d_x, _row(norm_w),
            jnp.concatenate([expand, expand], axis=0))
    yz = pl.pallas_call(
        _ssd_core_body,
        out_shape=jax.ShapeDtypeStruct((rows, di), F32),
        grid=(batch, nchunk),
        in_specs=[pl.BlockSpec((lc, nzx), lambda b, c: (b * nchunk + c, 0)),
                  pl.BlockSpec((lc, SSD_HEADS), lambda b, c: (b * nchunk + c, 0))]
                 + [_const_spec(a.shape) for a in args[2:]],
        out_specs=pl.BlockSpec((lc, di), lambda b, c: (b * nchunk + c, 0)),
        scratch_shapes=[pltpu.VMEM((lc + 2 * SUBLANES, SSD_CONV_DIM), F32),
                        pltpu.VMEM((SSD_STATE, di), F32),
                        pltpu.VMEM((lc, di), F32)],
        compiler_params=_params(2),
        name="ssd_core",
    )(*args)
    return _proj_res(x2, yz, w_out)


def _conf_body(x_ref, g_ref, w1_ref, b1_ref, dw_ref, dwb_ref, lnw_ref, lnb_ref, w2_ref, b2_ref, o_ref,
               ubuf_ref, c_s, *, tiles_per_seq, rb, lanes):
    tm = x_ref.shape[0]
    d = D_MODEL
    hl = CONF_HALO

    @pl.when(pl.program_id(0) % tiles_per_seq == 0)
    def _():
        ubuf_ref[tm:tm + hl, :] = jnp.zeros((hl, d), F32)

    x = x_ref[...]
    h = _rms(x, g_ref[...]).astype(BF16)
    p = jnp.dot(h, w1_ref[...], preferred_element_type=F32) + b1_ref[...]
    u = p[:, 0:d] * jax.nn.sigmoid(p[:, d:2 * d])
    ubuf_ref[0:hl, :] = ubuf_ref[tm:tm + hl, :]
    ubuf_ref[hl:hl + tm, :] = u
    base = hl - (CONF_KERNEL - 1)
    for r0 in range(0, tm, rb):
        for c0 in range(0, d, lanes):
            acc = jnp.broadcast_to(dwb_ref[:, c0:c0 + lanes], (rb, lanes))
            win = ubuf_ref[r0:r0 + rb + hl, c0:c0 + lanes]
            for res in range(SUBLANES):
                offs = [o for o in range(base, base + CONF_KERNEL) if o % SUBLANES == res]
                slab = win if res == 0 else pltpu.roll(win, rb + hl - res, 0)
                for o in offs:
                    a0 = o - res
                    acc = acc + dw_ref[o - base:o - base + 1, c0:c0 + lanes] * slab[a0:a0 + rb, :]
            c_s[r0:r0 + rb, c0:c0 + lanes] = acc
    y = c_s[...]
    mu = jnp.mean(y, axis=-1, keepdims=True)
    dy = y - mu
    var = jnp.mean(dy * dy, axis=-1, keepdims=True)
    zn = dy * lax.rsqrt(var + CONF_LN_EPS) * lnw_ref[...] + lnb_ref[...]
    o_ref[...] = x + _dot(jax.nn.silu(zn), w2_ref[...]) + b2_ref[...]


def _conformer(x2, seq, g, w_pw1, b_pw1, dw_w, dw_b, ln_w, ln_b, w_pw2, b_pw2, tm=512):
    rows = x2.shape[0]
    d = D_MODEL
    assert seq % tm == 0 and CONF_HALO >= CONF_KERNEL - 1 and CONF_HALO % SUBLANES == 0
    args = (x2, _row(g), w_pw1.astype(BF16), _row(b_pw1), dw_w.astype(F32), _row(dw_b), _row(ln_w),
            _row(ln_b), w_pw2.astype(BF16), _row(b_pw2))
    return pl.pallas_call(
        functools.partial(_conf_body, tiles_per_seq=seq // tm, rb=128, lanes=128),
        out_shape=jax.ShapeDtypeStruct((rows, d), F32),
        grid=(rows // tm,),
        in_specs=[pl.BlockSpec((tm, d), lambda i: (i, 0))] + [_const_spec(a.shape) for a in args[1:]],
        out_specs=pl.BlockSpec((tm, d), lambda i: (i, 0)),
        scratch_shapes=[pltpu.VMEM((tm + CONF_HALO, d), F32), pltpu.VMEM((tm, d), F32)],
        compiler_params=_params(1),
        name="conformer",
    )(*args)


def kernel(x, norm_mix, norm_ffn, norm_final, ffn_w_up, ffn_conv_w, ffn_conv_b, ffn_w_down, moba_w_qkv, moba_w_o, rwkv_mu, rwkv_w_rkv, rwkv_w0, rwkv_w1, rwkv_w2, rwkv_a0, rwkv_a1, rwkv_a2, rwkv_g1, rwkv_g2, rwkv_k_k, rwkv_k_a, rwkv_r_k, rwkv_gn_w, rwkv_gn_b, rwkv_w_o, ssd_w_in, ssd_conv_w, ssd_conv_b, ssd_dt_bias, ssd_a_log, ssd_d, ssd_norm_w, ssd_w_out, conf_w_pw1, conf_b_pw1, conf_dw_w, conf_dw_b, conf_ln_w, conf_ln_b, conf_w_pw2, conf_b_pw2):
    b, s, d = x.shape
    depth = norm_mix.shape[0]
    rows = b * s
    for i in range(depth):
        kind, j = i % 4, i // 4
        if kind == 0:
            x2 = _moba(x.reshape(b, s, d), norm_mix[i], moba_w_qkv[j], moba_w_o[j]).reshape(rows, d)
        elif kind == 1:
            x2 = _rwkv7(x.reshape(rows, d), s, norm_mix[i], rwkv_mu[j], rwkv_w_rkv[j], rwkv_w0[j], rwkv_w1[j],
                        rwkv_w2[j], rwkv_a0[j], rwkv_a1[j], rwkv_a2[j], rwkv_g1[j], rwkv_g2[j], rwkv_k_k[j],
                        rwkv_k_a[j], rwkv_r_k[j], rwkv_gn_w[j], rwkv_gn_b[j], rwkv_w_o[j])
        elif kind == 2:
            x2 = _mamba2(x.reshape(rows, d), b, s, norm_mix[i], ssd_w_in[j], ssd_conv_w[j], ssd_conv_b[j],
                         ssd_dt_bias[j], ssd_a_log[j], ssd_d[j], ssd_norm_w[j], ssd_w_out[j])
        else:
            x2 = _conformer(x.reshape(rows, d), s, norm_mix[i], conf_w_pw1[j], conf_b_pw1[j], conf_dw_w[j],
                            conf_dw_b[j], conf_ln_w[j], conf_ln_b[j], conf_w_pw2[j], conf_b_pw2[j])
        x = _conv_ffn(x2, s, norm_ffn[i], ffn_w_up[i], ffn_conv_w[i], ffn_conv_b[i], ffn_w_down[i],
                      norm_final, final=(i == depth - 1))
    return x.reshape(b, s, d)
```

```python
import functools

import jax
import jax.numpy as jnp
from jax import lax
from jax.experimental import pallas as pl
from jax.experimental.pallas import tpu as pltpu

F32 = jnp.float32
BF16 = jnp.bfloat16

D_MODEL = 1024
NORM_EPS = 1e-6
NEG_INF = -1e30
LOG2E = 1.4426950408889634

MOBA_HEADS = 16
MOBA_HEAD_DIM = 64
MOBA_BLOCK = 256
MOBA_TOPK = 3
MOBA_ONES_ROWS = 16
MOBA_HEAD_GROUP = 8

RWKV_HEADS = 16
RWKV_HEAD_DIM = 64
RWKV_GN_EPS = 64e-5
RWKV_CHUNK = 64

SSD_D_INNER = 2048
SSD_HEAD_DIM = 64
SSD_HEADS = 32
SSD_GROUPS = 4
SSD_STATE = 128
SSD_CONV = 4
SSD_CHUNK = 128
SSD_CONV_DIM = SSD_D_INNER + 2 * SSD_GROUPS * SSD_STATE
SSD_NORM_EPS = 1e-5

CONF_KERNEL = 31
CONF_LN_EPS = 1e-5
CONF_HALO = 32

D_FF = 2816
FFN_CONV = 3
FFN_CHUNK = 256

SUBLANES = 8
VMEM_LIMIT = 56 * 1024 * 1024


def _rms(x, g, eps=NORM_EPS):
    return x * lax.rsqrt(jnp.mean(x * x, axis=-1, keepdims=True) + eps) * g


def _dot(a, b):
    return jnp.dot(a.astype(BF16), b.astype(BF16), preferred_element_type=F32)


def _dot_nt(a, b):
    return lax.dot_general(a.astype(BF16), b.astype(BF16), (((1,), (1,)), ((), ())),
                           preferred_element_type=F32)


def _dot_tn(a, b):
    return lax.dot_general(a.astype(BF16), b.astype(BF16), (((0,), (0,)), ((), ())),
                           preferred_element_type=F32)


def _split2(a):
    hi = a.astype(BF16)
    lo = (a - hi.astype(F32)).astype(BF16)
    return hi, lo


def _split3(a):
    hi = a.astype(BF16)
    r = a - hi.astype(F32)
    mid = r.astype(BF16)
    lo = (r - mid.astype(F32)).astype(BF16)
    return hi, mid, lo


def _cumsum_rows(tril3, b):
    return jnp.dot(tril3, jnp.concatenate(_split3(b), axis=0), preferred_element_type=F32)


def _dot_hilo(a, b2_bf16):
    return jnp.dot(jnp.concatenate(_split2(a), axis=1), b2_bf16, preferred_element_type=F32)


def _const_spec(shape):
    n = len(shape)
    return pl.BlockSpec(shape, lambda *_: (0,) * n, pipeline_mode=pl.Buffered(1))


def _params(n_axes):
    return pltpu.CompilerParams(dimension_semantics=("arbitrary",) * n_axes,
                                vmem_limit_bytes=VMEM_LIMIT)


def _row(v):
    return v.reshape(1, -1).astype(F32)


def _ffn_body(x_ref, g_ref, wup_ref, cw_ref, cb_ref, wdn_ref, gf_ref, o_ref, halo_ref, act_ref,
              *, tiles_per_seq, final):
    tm = x_ref.shape[0]
    fc = FFN_CHUNK
    hs = SUBLANES

    @pl.when(pl.program_id(0) % tiles_per_seq == 0)
    def _():
        halo_ref[...] = jnp.zeros_like(halo_ref)

    x = x_ref[...]
    h = _rms(x, g_ref[...]).astype(BF16)

    r8 = lax.broadcasted_iota(jnp.int32, (hs, fc), 0)

    def conv(u, off):
        prev = halo_ref[:, off:off + fc]
        halo_ref[:, off:off + fc] = u[tm - hs:tm, :]
        w0 = cw_ref[0:1, off:off + fc]
        w1 = cw_ref[1:2, off:off + fc]
        u1 = pltpu.roll(u, 1, 0)
        u2 = pltpu.roll(u, 2, 0)
        y = w0 * u2 + w1 * u1 + cw_ref[2:3, off:off + fc] * u + cb_ref[0:1, off:off + fc]
        fix = (w0 * jnp.where(r8 < 2, pltpu.roll(prev, 2, 0) - u2[0:hs, :], 0.0)
               + w1 * jnp.where(r8 < 1, pltpu.roll(prev, 1, 0) - u1[0:hs, :], 0.0))
        return jnp.concatenate([y[0:hs, :] + fix, y[hs:, :]], axis=0)

    def up(c):
        og, ou = c * fc, D_FF + c * fc
        return (jnp.dot(h, wup_ref[:, og:og + fc], preferred_element_type=F32),
                jnp.dot(h, wup_ref[:, ou:ou + fc], preferred_element_type=F32))

    n_chunks = D_FF // fc
    nxt = up(0)
    for c in range(n_chunks):
        ug, uu = nxt
        if c + 1 < n_chunks:
            nxt = up(c + 1)
        yg = conv(ug, c * fc)
        yu = conv(uu, D_FF + c * fc)
        act_ref[:, c * fc:(c + 1) * fc] = (jax.nn.silu(yg) * yu).astype(BF16)
    y = x + jnp.dot(act_ref[...], wdn_ref[...], preferred_element_type=F32)
    if final:
        y = _rms(y, gf_ref[...])
    o_ref[...] = y


def _layer_spec(shape, layer):
    n = len(shape)
    return pl.BlockSpec((None,) + tuple(shape), lambda *_: (layer,) + (0,) * n, pipeline_mode=pl.Buffered(1))


def _conv_ffn(x2, seq, g, w_up_all, conv_w, conv_b, w_down_all, layer, g_final, final, tm=512):
    rows = x2.shape[0]
    assert seq % tm == 0 and D_FF % FFN_CHUNK == 0
    body = functools.partial(_ffn_body, tiles_per_seq=seq // tm, final=final)
    return pl.pallas_call(
        body,
        out_shape=jax.ShapeDtypeStruct((rows, D_MODEL), F32),
        grid=(rows // tm,),
        in_specs=[
            pl.BlockSpec((tm, D_MODEL), lambda i: (i, 0)),
            _const_spec((1, D_MODEL)),
            _layer_spec((D_MODEL, 2 * D_FF), layer),
            _const_spec((FFN_CONV, 2 * D_FF)),
            _const_spec((1, 2 * D_FF)),
            _layer_spec((D_FF, D_MODEL), layer),
            _const_spec((1, D_MODEL)),
        ],
        out_specs=pl.BlockSpec((tm, D_MODEL), lambda i: (i, 0)),
        scratch_shapes=[pltpu.VMEM((SUBLANES, 2 * D_FF), F32), pltpu.VMEM((tm, D_FF), BF16)],
        compiler_params=_params(1),
        name="conv_ffn",
    )(x2, _row(g), w_up_all, conv_w.astype(F32), _row(conv_b), w_down_all, _row(g_final))


def _proj_res_body(x_ref, y_ref, w_ref, o_ref):
    o_ref[...] = x_ref[...] + jnp.dot(y_ref[...].astype(BF16), w_ref[...], preferred_element_type=F32)


def _proj_res(x2, y2, w, tm=1024):
    rows, k = y2.shape
    assert rows % tm == 0
    return pl.pallas_call(
        _proj_res_body,
        out_shape=jax.ShapeDtypeStruct((rows, D_MODEL), F32),
        grid=(rows // tm,),
        in_specs=[pl.BlockSpec((tm, D_MODEL), lambda i: (i, 0)),
                  pl.BlockSpec((tm, k), lambda i: (i, 0)),
                  _const_spec((k, D_MODEL))],
        out_specs=pl.BlockSpec((tm, D_MODEL), lambda i: (i, 0)),
        compiler_params=_params(1),
        name="proj_res",
    )(x2, y2, w.astype(BF16))


def _moba_qkv_body(x_ref, g_ref, wt_ref, o_ref, *, nc):
    h = _rms(x_ref[0], g_ref[...]).astype(BF16)
    for n in range(3 * D_MODEL // nc):
        o_ref[0, n * nc:(n + 1) * nc, :] = lax.dot_general(
            wt_ref[n * nc:(n + 1) * nc, :], h, (((1,), (1,)), ((), ())),
            preferred_element_type=F32).astype(BF16)


def _moba_attn_body(q_ref, k_ref, v_ref, o_ref, kaug_ref, vblk_ref, vpair_ref, kmean_ref, *, nb):
    bs = MOBA_BLOCK
    dh = MOBA_HEAD_DIM
    hg = MOBA_HEAD_GROUP
    hh = range(hg)
    iq = pl.program_id(2)

    @pl.when(iq == 0)
    def _():
        blk = lax.broadcasted_iota(jnp.int32, (bs, nb), 1)
        for j in range(nb):
            kr = k_ref[0, :, j * bs:(j + 1) * bs].astype(F32).T
            onehot = jnp.where(blk == j, 1.0, 0.0).astype(BF16)
            for i in hh:
                kri = kr[:, i * dh:(i + 1) * dh]
                kaug_ref[i, j // 2, (j % 2) * bs:(j % 2 + 1) * bs, :] = jnp.concatenate(
                    [kri.astype(BF16), onehot], axis=1)
                kmean_ref[i, j:j + 1, :] = jnp.mean(kri, axis=0, keepdims=True)
                vblk_ref[i, j, 0:dh, :] = v_ref[0, i * dh:(i + 1) * dh, j * bs:(j + 1) * bs]
                vblk_ref[i, j, dh:dh + MOBA_ONES_ROWS, :] = jnp.ones((MOBA_ONES_ROWS, bs), BF16)
                if j % 2 == 0:
                    vpair_ref[i, j // 2, 0:dh, :] = v_ref[0, i * dh:(i + 1) * dh, j * bs:(j + 2) * bs]
                    vpair_ref[i, j // 2, dh:dh + MOBA_ONES_ROWS, :] = jnp.ones((MOBA_ONES_ROWS, 2 * bs), BF16)

    jidx = lax.broadcasted_iota(jnp.int32, (nb, bs), 0)
    qs, qaug = [], []
    for i in hh:
        q_raw = q_ref[0, i * dh:(i + 1) * dh, :]
        km_hi, km_lo = _split2(kmean_ref[i])
        gate = (jnp.dot(km_hi, q_raw, preferred_element_type=F32)
                + jnp.dot(km_lo, q_raw, preferred_element_type=F32))
        cnt = jnp.zeros((nb, bs), jnp.int32)
        for jp in range(nb):
            gj = gate[jp:jp + 1, :]
            beats = (gj > gate) | ((gj == gate) & (jp < jidx))
            cnt = cnt + jnp.where(beats & (jp < iq), 1, 0)
        selm = (cnt < MOBA_TOPK) & (jidx < iq)
        qs.append(q_raw * (dh ** -0.5 * LOG2E))
        qaug.append(jnp.concatenate([qs[i], jnp.where(selm, 0.0, NEG_INF).astype(BF16)], axis=0))

    kpos = lax.broadcasted_iota(jnp.int32, (bs, bs), 0)
    qpos = lax.broadcasted_iota(jnp.int32, (bs, bs), 1)
    causal = kpos <= qpos
    own_rows = pl.ds(pl.multiple_of((iq % 2) * bs, bs), bs)
    s = [jnp.where(causal, jnp.dot(kaug_ref[i, iq // 2, own_rows, :][:, 0:dh], qs[i],
                                   preferred_element_type=F32), NEG_INF)
         for i in hh]
    m0 = [jnp.max(s[i], axis=0, keepdims=True) for i in hh]
    p = [jnp.exp2(s[i] - m0[i]) for i in hh]
    acc0 = [jnp.dot(vblk_ref[i, iq], p[i].astype(BF16), preferred_element_type=F32) for i in hh]

    def step(j, carry):
        m, acc = carry
        sj = [jnp.dot(kaug_ref[i, j], qaug[i], preferred_element_type=F32) for i in hh]
        m_new, acc_new = [], []
        for i in hh:
            mi = jnp.maximum(m[i], jnp.max(sj[i], axis=0, keepdims=True))
            pj = jnp.exp2(sj[i] - mi).astype(BF16)
            acc_new.append(jnp.exp2(m[i] - mi) * acc[i]
                           + jnp.dot(vpair_ref[i, j], pj, preferred_element_type=F32))
            m_new.append(mi)
        return m_new, acc_new

    _, acc = lax.fori_loop(0, (iq + 1) // 2, step, (m0, acc0))
    for i in hh:
        o_ref[0, i * dh:(i + 1) * dh, :] = (acc[i][0:dh, :] / acc[i][dh:dh + 1, :]).astype(BF16)


def _moba_out_body(x_ref, o_ref_in, w_ref, o_ref):
    o_ref[0] = x_ref[0] + lax.dot_general(o_ref_in[0], w_ref[...], (((0,), (0,)), ((), ())),
                                          preferred_element_type=F32)


def _moba(x, g, w_qkv, w_o, ts=1024):
    b, s, d = x.shape
    nb = s // MOBA_BLOCK
    dh = MOBA_HEAD_DIM
    assert s % ts == 0 and s % MOBA_BLOCK == 0 and nb % 2 == 0
    qkvt = pl.pallas_call(
        functools.partial(_moba_qkv_body, nc=512),
        out_shape=jax.ShapeDtypeStruct((b, 3 * d, s), BF16),
        grid=(b, s // ts),
        in_specs=[pl.BlockSpec((1, ts, d), lambda i, j: (i, j, 0)),
                  _const_spec((1, d)),
                  _const_spec((3 * d, d))],
        out_specs=pl.BlockSpec((1, 3 * d, ts), lambda i, j: (i, 0, j)),
        compiler_params=_params(2),
        name="moba_qkv",
    )(x, _row(g), w_qkv.T.astype(BF16))

    hg = MOBA_HEAD_GROUP
    ng = MOBA_HEADS // hg
    gd = hg * dh
    ot = pl.pallas_call(
        functools.partial(_moba_attn_body, nb=nb),
        out_shape=jax.ShapeDtypeStruct((b, d, s), BF16),
        grid=(b, ng, nb),
        in_specs=[pl.BlockSpec((1, gd, MOBA_BLOCK), lambda i, j, k: (i, j, k)),
                  pl.BlockSpec((1, gd, s), lambda i, j, k: (i, ng + j, 0)),
                  pl.BlockSpec((1, gd, s), lambda i, j, k: (i, 2 * ng + j, 0))],
        out_specs=pl.BlockSpec((1, gd, MOBA_BLOCK), lambda i, j, k: (i, j, k)),
        scratch_shapes=[pltpu.VMEM((hg, nb // 2, 2 * MOBA_BLOCK, dh + nb), BF16),
                        pltpu.VMEM((hg, nb, dh + MOBA_ONES_ROWS, MOBA_BLOCK), BF16),
                        pltpu.VMEM((hg, nb // 2, dh + MOBA_ONES_ROWS, 2 * MOBA_BLOCK), BF16),
                        pltpu.VMEM((hg, nb, dh), F32)],
        compiler_params=_params(3),
        name="moba_attn",
    )(qkvt, qkvt, qkvt)

    return pl.pallas_call(
        _moba_out_body,
        out_shape=jax.ShapeDtypeStruct((b, s, d), F32),
        grid=(b, s // ts),
        in_specs=[pl.BlockSpec((1, ts, d), lambda i, j: (i, j, 0)),
                  pl.BlockSpec((1, d, ts), lambda i, j: (i, 0, j)),
                  _const_spec((d, d))],
        out_specs=pl.BlockSpec((1, ts, d), lambda i, j: (i, j, 0)),
        compiler_params=_params(2),
        name="moba_out",
    )(x, ot, w_o.astype(BF16))


def _rwkv_body(x_ref, g_ref, mu_ref, wr_ref, wk_ref, wv_ref, w0_ref, w1_ref, w2_ref, a0_ref, a1_ref,
               a2_ref, g1_ref, g2_ref, kk_ref, ka_ref, rk_ref, gnw_ref, gnb_ref, wo_ref, p_ref, pt_ref,
               o_ref,
               hbuf_ref, st_ref, r_s, lw_s, k_s, v_s, a_s, b_s, y_s, gate_s, bonus_s,
               ch0, t0, mv0, mr0, gt0, ch1, t1, mv1, mr1, gt1, *, tiles_per_seq):
    bufs = ((ch0, t0, mv0, mr0, gt0), (ch1, t1, mv1, mr1, gt1))
    tm = x_ref.shape[0]
    lc = RWKV_CHUNK
    hd = RWKV_HEAD_DIM
    hs = SUBLANES

    @pl.when(pl.program_id(0) % tiles_per_seq == 0)
    def _():
        hbuf_ref[tm:tm + hs, :] = jnp.zeros((hs, D_MODEL), F32)
        st_ref[...] = jnp.zeros_like(st_ref)

    def seg_sum(v):
        return jnp.dot(v.astype(BF16), p_ref[...], preferred_element_type=F32)

    def seg_bcast(v):
        return _dot_hilo(v, pt_ref[...])

    x = x_ref[...]
    hn = _rms(x, g_ref[...])
    hbuf_ref[0:hs, :] = hbuf_ref[tm:tm + hs, :]
    hbuf_ref[hs:hs + tm, :] = hn
    xx = hbuf_ref[hs - 1:hs - 1 + tm, :] - hn

    def mix(i):
        return hn + xx * mu_ref[i:i + 1, :]

    r = _dot(mix(0), wr_ref[...])
    k = _dot(mix(1), wk_ref[...])
    v = _dot(mix(2), wv_ref[...])
    wpre = w0_ref[...] + _dot(jnp.tanh(_dot(mix(3), w1_ref[...])), w2_ref[...])
    lw = -jnp.exp(-jax.nn.softplus(-wpre) - 0.5)
    a = jax.nn.sigmoid(a0_ref[...] + _dot(_dot(mix(4), a1_ref[...]), a2_ref[...]))
    gate_s[...] = _dot(jax.nn.sigmoid(_dot(mix(5), g1_ref[...])), g2_ref[...])

    kk = k * kk_ref[...]
    inv = 1.0 / jnp.maximum(jnp.sqrt(seg_sum(kk * kk)), 1e-12)
    kk = kk * seg_bcast(inv)
    k = k * (1.0 + (a - 1.0) * ka_ref[...])
    bonus_s[...] = seg_bcast(seg_sum(r * k * rk_ref[...])) * v
    r_s[...] = r
    lw_s[...] = lw
    k_s[...] = k
    v_s[...] = v
    a_s[...] = -kk
    b_s[...] = kk * a

    ti = lax.broadcasted_iota(jnp.int32, (lc, lc), 0)
    si = lax.broadcasted_iota(jnp.int32, (lc, lc), 1)
    tril = jnp.where(si <= ti, 1.0, 0.0).astype(BF16)
    tril3 = jnp.concatenate([tril, tril, tril], axis=1)
    low_strict = si < ti
    ti2 = lax.broadcasted_iota(jnp.int32, (lc, 2 * lc), 0)
    si2 = lax.broadcasted_iota(jnp.int32, (lc, 2 * lc), 1)
    low_incl2 = jnp.where(si2 >= lc, si2 - lc, si2) <= ti2
    keep_u = si2 >= lc
    low_strict2 = si2 < ti2

    hh = range(RWKV_HEADS)
    sls = [slice(h * hd, (h + 1) * hd) for h in hh]
    eye2 = jnp.where(si2 - lc == ti2, 1.0, 0.0)

    def prepare(c):
        ch, t_s, mv_s, mr_s, gt_s = bufs[c % 2]
        rows = slice(c * lc, (c + 1) * lc)
        lw_c = lw_s[rows, :]
        lg = _cumsum_rows(tril3, lw_c)
        lg_last = lg[lc - 1:lc, :]
        g_inv = jnp.exp(-lg)
        g_tail = jnp.exp(lg_last - lg)
        gt_s[...] = jnp.broadcast_to(jnp.exp(lg_last), gt_s.shape)
        ch[0, 0:lc, :] = a_s[rows, :] * jnp.exp(lg - lw_c)
        ch[0, lc:2 * lc, :] = r_s[rows, :] * jnp.exp(lg)
        ch[1, 0:lc, :] = b_s[rows, :] * g_inv
        ch[1, lc:2 * lc, :] = k_s[rows, :] * g_inv
        ch[2, 0:lc, :] = b_s[rows, :] * g_tail
        ch[2, lc:2 * lc, :] = k_s[rows, :] * g_tail
        gm = [_dot_nt(ch[0, :, sls[h]], ch[1, :, sls[h]]) for h in hh]
        for h in hh:
            mr_s[h] = jnp.where(low_incl2, gm[h][lc:2 * lc, :], 0.0).astype(BF16)
            mv_s[h] = _dot(jnp.where(low_strict, gm[h][0:lc, lc:2 * lc], 0.0), v_s[rows, sls[h]])
        w = [jnp.where(keep_u, eye2, jnp.where(low_strict2, gm[h][0:lc, :], 0.0)) for h in hh]
        steps = 1
        while steps < lc:
            r = [_dot(w[h][:, 0:lc], w[h]) for h in hh]
            w = [r[h] + jnp.where(keep_u, w[h], 0.0) for h in hh]
            steps *= 2
        for h in hh:
            t_s[h] = w[h][:, lc:].astype(BF16)

    def advance(c):
        ch, t_s, mv_s, mr_s, gt_s = bufs[c % 2]
        rows = slice(c * lc, (c + 1) * lc)
        sh = [st_ref[h] for h in hh]
        ars = [_dot_nt(ch[0, :, sls[h]], sh[h]) for h in hh]
        u = [jnp.dot(t_s[h], (ars[h][0:lc, :] + mv_s[h]).astype(BF16), preferred_element_type=F32)
             for h in hh]
        uv = [jnp.concatenate([u[h].astype(BF16), v_s[rows, sls[h]].astype(BF16)], axis=0) for h in hh]
        for h in hh:
            y_s[rows, sls[h]] = ars[h][lc:2 * lc, :] + jnp.dot(mr_s[h], uv[h], preferred_element_type=F32)
        for h in hh:
            st_ref[h] = sh[h] * gt_s[0:1, sls[h]] + _dot_tn(uv[h], ch[2, :, sls[h]])

    n_chunks = tm // lc
    prepare(0)
    for c in range(n_chunks):
        if c + 1 < n_chunks:
            prepare(c + 1)
        advance(c)

    y = y_s[...]
    mean = seg_bcast(seg_sum(y)) * (1.0 / hd)
    dy = y - mean
    var = seg_bcast(seg_sum(dy * dy)) * (1.0 / hd)
    yn = dy * lax.rsqrt(var + RWKV_GN_EPS) * gnw_ref[...] + gnb_ref[...]
    o_ref[...] = x + _dot((yn + bonus_s[...]) * gate_s[...], wo_ref[...])


def _rwkv7(x2, seq, g, mu, w_rkv, w0, w1, w2, a0, a1, a2, g1, g2, k_k, k_a, r_k, gn_w, gn_b, w_o, tm=256):
    rows = x2.shape[0]
    d = D_MODEL
    assert seq % tm == 0 and tm % RWKV_CHUNK == 0 and RWKV_CHUNK == RWKV_HEAD_DIM
    head_of = jnp.arange(d) // RWKV_HEAD_DIM
    p = (head_of[:, None] == jnp.arange(128)[None, :]).astype(BF16)
    args = (x2, _row(g), mu.astype(F32), w_rkv[0].astype(BF16), w_rkv[1].astype(BF16), w_rkv[2].astype(BF16),
            _row(w0), w1.astype(BF16), w2.astype(BF16), _row(a0), a1.astype(BF16), a2.astype(BF16),
            g1.astype(BF16), g2.astype(BF16), _row(k_k), _row(k_a), _row(r_k), _row(gn_w), _row(gn_b),
            w_o.astype(BF16), p, jnp.concatenate([p.T, p.T], axis=0))
    in_specs = [pl.BlockSpec((tm, d), lambda i: (i, 0))] + [_const_spec(a.shape) for a in args[1:]]
    big = pltpu.VMEM((tm, d), F32)
    lc = RWKV_CHUNK
    return pl.pallas_call(
        functools.partial(_rwkv_body, tiles_per_seq=seq // tm),
        out_shape=jax.ShapeDtypeStruct((rows, d), F32),
        grid=(rows // tm,),
        in_specs=in_specs,
        out_specs=pl.BlockSpec((tm, d), lambda i: (i, 0)),
        scratch_shapes=[pltpu.VMEM((tm + 2 * SUBLANES, d), F32),
                        pltpu.VMEM((RWKV_HEADS, RWKV_HEAD_DIM, RWKV_HEAD_DIM), F32),
                        big, big, big, big, big, big, big, big, big,
                        ] + 2 * [pltpu.VMEM((3, 2 * lc, d), F32),
                                 pltpu.VMEM((RWKV_HEADS, lc, lc), BF16),
                                 pltpu.VMEM((RWKV_HEADS, lc, RWKV_HEAD_DIM), F32),
                                 pltpu.VMEM((RWKV_HEADS, lc, 2 * lc), BF16),
                                 pltpu.VMEM((SUBLANES, d), F32)],
        compiler_params=_params(1),
        name="rwkv7",
    )(*args)


def _ssd_in_body(x_ref, g_ref, w_ref, wdt_ref, zx_ref, dt_ref, *, nc):
    hn = _rms(x_ref[...], g_ref[...])
    h = hn.astype(BF16)
    for n in range(w_ref.shape[1] // nc):
        zx_ref[:, n * nc:(n + 1) * nc] = jnp.dot(h, w_ref[:, n * nc:(n + 1) * nc],
                                                 preferred_element_type=F32)
    h_lo = (hn - h.astype(F32)).astype(BF16)
    w_hi, w_lo = _split2(wdt_ref[...])
    dt_ref[...] = (jnp.dot(h, w_hi, preferred_element_type=F32)
                   + jnp.dot(h, w_lo, preferred_element_type=F32)
                   + jnp.dot(h_lo, w_hi, preferred_element_type=F32))


def _ssd_core_body(zx_ref, dt_ref, cw_ref, cb_ref, dtb_ref, alog_ref, dskip_ref, nw_ref, e_ref, o_ref,
                   cbuf_ref, state_ref, y_s):
    lc = SSD_CHUNK
    di = SSD_D_INNER
    hs = SUBLANES
    gw = di // SSD_GROUPS
    ns = SSD_STATE

    @pl.when(pl.program_id(1) == 0)
    def _():
        cbuf_ref[lc:lc + hs, :] = jnp.zeros((hs, SSD_CONV_DIM), F32)
        state_ref[...] = jnp.zeros_like(state_ref)

    z = zx_ref[:, 0:di]
    xbc_raw = zx_ref[:, di:di + SSD_CONV_DIM]
    cbuf_ref[0:hs, :] = cbuf_ref[lc:lc + hs, :]
    cbuf_ref[hs:hs + lc, :] = xbc_raw
    xbc = cb_ref[...] + cw_ref[SSD_CONV - 1:SSD_CONV, :] * xbc_raw
    win = cbuf_ref[0:hs + lc, :]
    for j in range(SSD_CONV - 1):
        back = SSD_CONV - 1 - j
        xbc = xbc + cw_ref[j:j + 1, :] * pltpu.roll(win, back, 0)[hs:hs + lc, :]
    xbc = jax.nn.silu(xbc)
    xs = xbc[:, 0:di]
    bm = xbc[:, di:di + SSD_GROUPS * ns]
    cm = xbc[:, di + SSD_GROUPS * ns:]

    dt = jax.nn.softplus(dt_ref[...] + dtb_ref[...])
    adt = dt * (-jnp.exp(alog_ref[...]))
    li = lax.broadcasted_iota(jnp.int32, (lc, lc), 0)
    si = lax.broadcasted_iota(jnp.int32, (lc, lc), 1)
    causal = si <= li
    first_head = lax.broadcasted_iota(jnp.int32, (lc, 2 * SSD_HEAD_DIM), 1) < SSD_HEAD_DIM
    tril = jnp.where(causal, 1.0, 0.0).astype(BF16)
    acs = _cumsum_rows(jnp.concatenate([tril, tril, tril], axis=1), adt)
    acs_t = acs.T
    a_last = acs[lc - 1:lc, :]

    def expand(v):
        return _dot_hilo(v, e_ref[...])

    exp_acs_x = expand(jnp.exp(acs))
    xd = xs * expand(dt)
    xdec = xd * expand(jnp.exp(a_last - acs))

    for g in range(SSD_GROUPS):
        bg = bm[:, g * ns:(g + 1) * ns]
        cg = cm[:, g * ns:(g + 1) * ns]
        cb = _dot_nt(cg, bg)
        gsl = slice(g * gw, (g + 1) * gw)
        st_g = state_ref[:, gsl]
        y_off = _dot(cg, st_g) * exp_acs_x[:, gsl]
        def masked_decay(h):
            col = jnp.broadcast_to(acs[:, h:h + 1], (lc, lc))
            row = jnp.broadcast_to(acs_t[h:h + 1, :], (lc, lc))
            return (cb * jnp.where(causal, jnp.exp(col - row), 0.0)).astype(BF16)

        for e in range(0, SSD_HEADS // SSD_GROUPS, 2):
            h = g * (SSD_HEADS // SSD_GROUPS) + e
            psl = slice(h * SSD_HEAD_DIM, (h + 2) * SSD_HEAD_DIM)
            xdp = xd[:, psl]
            xd_bd = jnp.concatenate([jnp.where(first_head, xdp, 0.0), jnp.where(first_head, 0.0, xdp)], axis=0)
            o0 = e * SSD_HEAD_DIM
            y_s[:, psl] = (jnp.dot(jnp.concatenate([masked_decay(h), masked_decay(h + 1)], axis=1),
                                   xd_bd.astype(BF16), preferred_element_type=F32)
                           + y_off[:, o0:o0 + 2 * SSD_HEAD_DIM])
        state_ref[:, gsl] = st_g * exp_acs_x[lc - 1:lc, gsl] + _dot_tn(bg, xdec[:, gsl])

    y = y_s[...] + xs * dskip_ref[...]
    yz = y * jax.nn.silu(z)
    for g in range(SSD_GROUPS):
        gsl = slice(g * gw, (g + 1) * gw)
        yg = yz[:, gsl]
        o_ref[:, gsl] = (yg * lax.rsqrt(jnp.mean(yg * yg, axis=-1, keepdims=True) + SSD_NORM_EPS)
                         * nw_ref[:, gsl])


def _mamba2(x2, batch, seq, g, w_in, conv_w, conv_b, dt_bias, a_log, d_skip, norm_w, w_out, tm=512):
    rows = x2.shape[0]
    d = D_MODEL
    di = SSD_D_INNER
    assert rows % tm == 0 and seq % SSD_CHUNK == 0 and SSD_CHUNK == 2 * SSD_HEAD_DIM
    nzx = di + SSD_CONV_DIM
    zx, dt = pl.pallas_call(
        functools.partial(_ssd_in_body, nc=512),
        out_shape=(jax.ShapeDtypeStruct((rows, nzx), F32), jax.ShapeDtypeStruct((rows, SSD_HEADS), F32)),
        grid=(rows // tm,),
        in_specs=[pl.BlockSpec((tm, d), lambda i: (i, 0)),
                  _const_spec((1, d)),
                  _const_spec((d, nzx)),
                  _const_spec((d, SSD_HEADS))],
        out_specs=(pl.BlockSpec((tm, nzx), lambda i: (i, 0)),
                   pl.BlockSpec((tm, SSD_HEADS), lambda i: (i, 0))),
        compiler_params=_params(1),
        name="ssd_in",
    )(x2, _row(g), w_in[:, :nzx].astype(BF16), w_in[:, nzx:].astype(F32))

    lc = SSD_CHUNK
    nchunk = seq // lc
    expand = (jnp.arange(SSD_HEADS)[:, None] == (jnp.arange(di) // SSD_HEAD_DIM)[None, :]).astype(BF16)
    d_x = jnp.repeat(d_skip.astype(F32), SSD_HEAD_DIM).reshape(1, di)
    args = (zx, dt, conv_w.astype(F32), _row(conv_b), _row(dt_bias), _row(a_log), d_x, _row(norm_w),
            jnp.concatenate([expand, expand], axis=0))
    yz = pl.pallas_call(
        _ssd_core_body,
        out_shape=jax.ShapeDtypeStruct((rows, di), F32),
        grid=(batch, nchunk),
        in_specs=[pl.BlockSpec((lc, nzx), lambda b, c: (b * nchunk + c, 0)),
                  pl.BlockSpec((lc, SSD_HEADS), lambda b, c: (b * nchunk + c, 0))]
                 + [_const_spec(a.shape) for a in args[2:]],
        out_specs=pl.BlockSpec((lc, di), lambda b, c: (b * nchunk + c, 0)),
        scratch_shapes=[pltpu.VMEM((lc + 2 * SUBLANES, SSD_CONV_DIM), F32),
                        pltpu.VMEM((SSD_STATE, di), F32),
                        pltpu.VMEM((lc, di), F32)],
        compiler_params=_params(2),
        name="ssd_core",
    )(*args)
    return _proj_res(x2, yz, w_out)


def _conf_body(x_ref, g_ref, w1_ref, b1_ref, dw_ref, dwb_ref, lnw_ref, lnb_ref, w2_ref, b2_ref, o_ref,
               ubuf_ref, c_s, *, tiles_per_seq, rb, lanes):
    tm = x_ref.shape[0]
    d = D_MODEL
    hl = CONF_HALO

    @pl.when(pl.program_id(0) % tiles_per_seq == 0)
    def _():
        ubuf_ref[tm:tm + hl, :] = jnp.zeros((hl, d), F32)

    x = x_ref[...]
    h = _rms(x, g_ref[...]).astype(BF16)
    p = jnp.dot(h, w1_ref[...], preferred_element_type=F32) + b1_ref[...]
    u = p[:, 0:d] * jax.nn.sigmoid(p[:, d:2 * d])
    ubuf_ref[0:hl, :] = ubuf_ref[tm:tm + hl, :]
    ubuf_ref[hl:hl + tm, :] = u
    base = hl - (CONF_KERNEL - 1)
    for r0 in range(0, tm, rb):
        for c0 in range(0, d, lanes):
            acc = jnp.broadcast_to(dwb_ref[:, c0:c0 + lanes], (rb, lanes))
            win = ubuf_ref[r0:r0 + rb + hl, c0:c0 + lanes]
            for res in range(SUBLANES):
                offs = [o for o in range(base, base + CONF_KERNEL) if o % SUBLANES == res]
                slab = win if res == 0 else pltpu.roll(win, rb + hl - res, 0)
                for o in offs:
                    a0 = o - res
                    acc = acc + dw_ref[o - base:o - base + 1, c0:c0 + lanes] * slab[a0:a0 + rb, :]
            c_s[r0:r0 + rb, c0:c0 + lanes] = acc
    y = c_s[...]
    mu = jnp.mean(y, axis=-1, keepdims=True)
    dy = y - mu
    var = jnp.mean(dy * dy, axis=-1, keepdims=True)
    zn = dy * lax.rsqrt(var + CONF_LN_EPS) * lnw_ref[...] + lnb_ref[...]
    o_ref[...] = x + _dot(jax.nn.silu(zn), w2_ref[...]) + b2_ref[...]


def _conformer(x2, seq, g, w_pw1, b_pw1, dw_w, dw_b, ln_w, ln_b, w_pw2, b_pw2, tm=512):
    rows = x2.shape[0]
    d = D_MODEL
    assert seq % tm == 0 and CONF_HALO >= CONF_KERNEL - 1 and CONF_HALO % SUBLANES == 0
    args = (x2, _row(g), w_pw1.astype(BF16), _row(b_pw1), dw_w.astype(F32), _row(dw_b), _row(ln_w),
            _row(ln_b), w_pw2.astype(BF16), _row(b_pw2))
    return pl.pallas_call(
        functools.partial(_conf_body, tiles_per_seq=seq // tm, rb=128, lanes=128),
        out_shape=jax.ShapeDtypeStruct((rows, d), F32),
        grid=(rows // tm,),
        in_specs=[pl.BlockSpec((tm, d), lambda i: (i, 0))] + [_const_spec(a.shape) for a in args[1:]],
        out_specs=pl.BlockSpec((tm, d), lambda i: (i, 0)),
        scratch_shapes=[pltpu.VMEM((tm + CONF_HALO, d), F32), pltpu.VMEM((tm, d), F32)],
        compiler_params=_params(1),
        name="conformer",
    )(*args)


def kernel(x, norm_mix, norm_ffn, norm_final, ffn_w_up, ffn_conv_w, ffn_conv_b, ffn_w_down, moba_w_qkv, moba_w_o, rwkv_mu, rwkv_w_rkv, rwkv_w0, rwkv_w1, rwkv_w2, rwkv_a0, rwkv_a1, rwkv_a2, rwkv_g1, rwkv_g2, rwkv_k_k, rwkv_k_a, rwkv_r_k, rwkv_gn_w, rwkv_gn_b, rwkv_w_o, ssd_w_in, ssd_conv_w, ssd_conv_b, ssd_dt_bias, ssd_a_log, ssd_d, ssd_norm_w, ssd_w_out, conf_w_pw1, conf_b_pw1, conf_dw_w, conf_dw_b, conf_ln_w, conf_ln_b, conf_w_pw2, conf_b_pw2):
    b, s, d = x.shape
    depth = norm_mix.shape[0]
    rows = b * s
    w_up_all = ffn_w_up.astype(BF16)
    w_down_all = ffn_w_down.astype(BF16)
    for i in range(depth):
        kind, j = i % 4, i // 4
        if kind == 0:
            x2 = _moba(x.reshape(b, s, d), norm_mix[i], moba_w_qkv[j], moba_w_o[j]).reshape(rows, d)
        elif kind == 1:
            x2 = _rwkv7(x.reshape(rows, d), s, norm_mix[i], rwkv_mu[j], rwkv_w_rkv[j], rwkv_w0[j], rwkv_w1[j],
                        rwkv_w2[j], rwkv_a0[j], rwkv_a1[j], rwkv_a2[j], rwkv_g1[j], rwkv_g2[j], rwkv_k_k[j],
                        rwkv_k_a[j], rwkv_r_k[j], rwkv_gn_w[j], rwkv_gn_b[j], rwkv_w_o[j])
        elif kind == 2:
            x2 = _mamba2(x.reshape(rows, d), b, s, norm_mix[i], ssd_w_in[j], ssd_conv_w[j], ssd_conv_b[j],
                         ssd_dt_bias[j], ssd_a_log[j], ssd_d[j], ssd_norm_w[j], ssd_w_out[j])
        else:
            x2 = _conformer(x.reshape(rows, d), s, norm_mix[i], conf_w_pw1[j], conf_b_pw1[j], conf_dw_w[j],
                            conf_dw_b[j], conf_ln_w[j], conf_ln_b[j], conf_w_pw2[j], conf_b_pw2[j])
        x = _conv_ffn(x2, s, norm_ffn[i], w_up_all, ffn_conv_w[i], ffn_conv_b[i], w_down_all, i,
                      norm_final, final=(i == depth - 1))
    return x.reshape(b, s, d)
```

```python
import functools

import jax
import jax.numpy as jnp
from jax import lax
from jax.experimental import pallas as pl
from jax.experimental.pallas import tpu as pltpu

F32 = jnp.float32
BF16 = jnp.bfloat16

D_MODEL = 1024
NORM_EPS = 1e-6
NEG_INF = -1e30
LOG2E = 1.4426950408889634

MOBA_HEADS = 16
MOBA_HEAD_DIM = 64
MOBA_BLOCK = 256
MOBA_TOPK = 3
MOBA_ONES_ROWS = 16
MOBA_HEAD_GROUP = 8

RWKV_HEADS = 16
RWKV_HEAD_DIM = 64
RWKV_GN_EPS = 64e-5
RWKV_CHUNK = 64

SSD_D_INNER = 2048
SSD_HEAD_DIM = 64
SSD_HEADS = 32
SSD_GROUPS = 4
SSD_STATE = 128
SSD_CONV = 4
SSD_CHUNK = 128
SSD_CONV_DIM = SSD_D_INNER + 2 * SSD_GROUPS * SSD_STATE
SSD_NORM_EPS = 1e-5

CONF_KERNEL = 31
CONF_LN_EPS = 1e-5
CONF_HALO = 32

D_FF = 2816
FFN_CONV = 3
FFN_CHUNK = 256

SUBLANES = 8
VMEM_LIMIT = 56 * 1024 * 1024


def _rms(x, g, eps=NORM_EPS):
    return x * lax.rsqrt(jnp.mean(x * x, axis=-1, keepdims=True) + eps) * g


def _dot(a, b):
    return jnp.dot(a.astype(BF16), b.astype(BF16), preferred_element_type=F32)


def _dot_nt(a, b):
    return lax.dot_general(a.astype(BF16), b.astype(BF16), (((1,), (1,)), ((), ())),
                           preferred_element_type=F32)


def _dot_tn(a, b):
    return lax.dot_general(a.astype(BF16), b.astype(BF16), (((0,), (0,)), ((), ())),
                           preferred_element_type=F32)


def _split2(a):
    hi = a.astype(BF16)
    lo = (a - hi.astype(F32)).astype(BF16)
    return hi, lo


def _split3(a):
    hi = a.astype(BF16)
    r = a - hi.astype(F32)
    mid = r.astype(BF16)
    lo = (r - mid.astype(F32)).astype(BF16)
    return hi, mid, lo


def _cumsum_rows(tril3, b):
    return jnp.dot(tril3, jnp.concatenate(_split3(b), axis=0), preferred_element_type=F32)


def _dot_hilo(a, b2_bf16):
    return jnp.dot(jnp.concatenate(_split2(a), axis=1), b2_bf16, preferred_element_type=F32)


def _const_spec(shape):
    n = len(shape)
    return pl.BlockSpec(shape, lambda *_: (0,) * n, pipeline_mode=pl.Buffered(1))


def _params(n_axes):
    return pltpu.CompilerParams(dimension_semantics=("arbitrary",) * n_axes,
                                vmem_limit_bytes=VMEM_LIMIT)


def _row(v):
    return v.reshape(1, -1).astype(F32)


def _ffn_body(*refs, tiles_per_seq, final, proj):
    if proj:
        x_ref, y_ref, wp_ref, g_ref, wup_ref, cw_ref, cb_ref, wdn_ref, gf_ref, o_ref, halo_ref, act_ref = refs
    else:
        x_ref, g_ref, wup_ref, cw_ref, cb_ref, wdn_ref, gf_ref, o_ref, halo_ref, act_ref = refs
    tm = x_ref.shape[0]
    fc = FFN_CHUNK
    hs = SUBLANES

    @pl.when(pl.program_id(0) % tiles_per_seq == 0)
    def _():
        halo_ref[...] = jnp.zeros_like(halo_ref)

    x = x_ref[...]
    if proj:
        x = x + jnp.dot(y_ref[...].astype(BF16), wp_ref[...], preferred_element_type=F32)
    h = _rms(x, g_ref[...]).astype(BF16)

    r8 = lax.broadcasted_iota(jnp.int32, (hs, fc), 0)

    def conv(u, off):
        prev = halo_ref[:, off:off + fc]
        halo_ref[:, off:off + fc] = u[tm - hs:tm, :]
        w0 = cw_ref[0:1, off:off + fc]
        w1 = cw_ref[1:2, off:off + fc]
        u1 = pltpu.roll(u, 1, 0)
        u2 = pltpu.roll(u, 2, 0)
        y = w0 * u2 + w1 * u1 + cw_ref[2:3, off:off + fc] * u + cb_ref[0:1, off:off + fc]
        fix = (w0 * jnp.where(r8 < 2, pltpu.roll(prev, 2, 0) - u2[0:hs, :], 0.0)
               + w1 * jnp.where(r8 < 1, pltpu.roll(prev, 1, 0) - u1[0:hs, :], 0.0))
        return jnp.concatenate([y[0:hs, :] + fix, y[hs:, :]], axis=0)

    def up(c):
        og, ou = c * fc, D_FF + c * fc
        return (jnp.dot(h, wup_ref[:, og:og + fc], preferred_element_type=F32),
                jnp.dot(h, wup_ref[:, ou:ou + fc], preferred_element_type=F32))

    n_chunks = D_FF // fc
    nxt = up(0)
    for c in range(n_chunks):
        ug, uu = nxt
        if c + 1 < n_chunks:
            nxt = up(c + 1)
        yg = conv(ug, c * fc)
        yu = conv(uu, D_FF + c * fc)
        act_ref[:, c * fc:(c + 1) * fc] = (jax.nn.silu(yg) * yu).astype(BF16)
    y = x + jnp.dot(act_ref[...], wdn_ref[...], preferred_element_type=F32)
    if final:
        y = _rms(y, gf_ref[...])
    o_ref[...] = y


def _layer_spec(shape, layer):
    n = len(shape)
    return pl.BlockSpec((None,) + tuple(shape), lambda *_: (layer,) + (0,) * n, pipeline_mode=pl.Buffered(1))


def _conv_ffn(x2, seq, g, w_up_all, conv_w, conv_b, w_down_all, layer, g_final, final, tm=512, pending=None):
    rows = x2.shape[0]
    assert seq % tm == 0 and D_FF % FFN_CHUNK == 0
    body = functools.partial(_ffn_body, tiles_per_seq=seq // tm, final=final, proj=pending is not None)
    extra_specs, extra_args = [], ()
    if pending is not None:
        y2, w_proj = pending
        extra_specs = [pl.BlockSpec((tm, y2.shape[1]), lambda i: (i, 0)), _const_spec(w_proj.shape)]
        extra_args = (y2, w_proj.astype(BF16))
    return pl.pallas_call(
        body,
        out_shape=jax.ShapeDtypeStruct((rows, D_MODEL), F32),
        grid=(rows // tm,),
        in_specs=[pl.BlockSpec((tm, D_MODEL), lambda i: (i, 0))] + extra_specs + [
            _const_spec((1, D_MODEL)),
            _layer_spec((D_MODEL, 2 * D_FF), layer),
            _const_spec((FFN_CONV, 2 * D_FF)),
            _const_spec((1, 2 * D_FF)),
            _layer_spec((D_FF, D_MODEL), layer),
            _const_spec((1, D_MODEL)),
        ],
        out_specs=pl.BlockSpec((tm, D_MODEL), lambda i: (i, 0)),
        scratch_shapes=[pltpu.VMEM((SUBLANES, 2 * D_FF), F32), pltpu.VMEM((tm, D_FF), BF16)],
        compiler_params=_params(1),
        name="conv_ffn",
    )(x2, *extra_args, _row(g), w_up_all, conv_w.astype(F32), _row(conv_b), w_down_all, _row(g_final))


def _proj_res_body(x_ref, y_ref, w_ref, o_ref):
    o_ref[...] = x_ref[...] + jnp.dot(y_ref[...].astype(BF16), w_ref[...], preferred_element_type=F32)


def _proj_res(x2, y2, w, tm=1024):
    rows, k = y2.shape
    assert rows % tm == 0
    return pl.pallas_call(
        _proj_res_body,
        out_shape=jax.ShapeDtypeStruct((rows, D_MODEL), F32),
        grid=(rows // tm,),
        in_specs=[pl.BlockSpec((tm, D_MODEL), lambda i: (i, 0)),
                  pl.BlockSpec((tm, k), lambda i: (i, 0)),
                  _const_spec((k, D_MODEL))],
        out_specs=pl.BlockSpec((tm, D_MODEL), lambda i: (i, 0)),
        compiler_params=_params(1),
        name="proj_res",
    )(x2, y2, w.astype(BF16))


def _moba_qkv_body(x_ref, g_ref, wt_ref, o_ref, *, nc):
    h = _rms(x_ref[0], g_ref[...]).astype(BF16)
    for n in range(3 * D_MODEL // nc):
        o_ref[0, n * nc:(n + 1) * nc, :] = lax.dot_general(
            wt_ref[n * nc:(n + 1) * nc, :], h, (((1,), (1,)), ((), ())),
            preferred_element_type=F32).astype(BF16)


def _moba_attn_body(q_ref, k_ref, v_ref, o_ref, kaug_ref, vblk_ref, vpair_ref, kmean_ref, *, nb):
    bs = MOBA_BLOCK
    dh = MOBA_HEAD_DIM
    hg = MOBA_HEAD_GROUP
    hh = range(hg)
    iq = pl.program_id(2)

    @pl.when(iq == 0)
    def _():
        blk = lax.broadcasted_iota(jnp.int32, (bs, nb), 1)
        for j in range(nb):
            kr = k_ref[0, :, j * bs:(j + 1) * bs].astype(F32).T
            onehot = jnp.where(blk == j, 1.0, 0.0).astype(BF16)
            for i in hh:
                kri = kr[:, i * dh:(i + 1) * dh]
                kaug_ref[i, j // 2, (j % 2) * bs:(j % 2 + 1) * bs, :] = jnp.concatenate(
                    [kri.astype(BF16), onehot], axis=1)
                kmean_ref[i, j:j + 1, :] = jnp.mean(kri, axis=0, keepdims=True)
                vblk_ref[i, j, 0:dh, :] = v_ref[0, i * dh:(i + 1) * dh, j * bs:(j + 1) * bs]
                vblk_ref[i, j, dh:dh + MOBA_ONES_ROWS, :] = jnp.ones((MOBA_ONES_ROWS, bs), BF16)
                if j % 2 == 0:
                    vpair_ref[i, j // 2, 0:dh, :] = v_ref[0, i * dh:(i + 1) * dh, j * bs:(j + 2) * bs]
                    vpair_ref[i, j // 2, dh:dh + MOBA_ONES_ROWS, :] = jnp.ones((MOBA_ONES_ROWS, 2 * bs), BF16)

    jidx = lax.broadcasted_iota(jnp.int32, (nb, bs), 0)
    qs, qaug = [], []
    for i in hh:
        q_raw = q_ref[0, i * dh:(i + 1) * dh, :]
        km_hi, km_lo = _split2(kmean_ref[i])
        gate = (jnp.dot(km_hi, q_raw, preferred_element_type=F32)
                + jnp.dot(km_lo, q_raw, preferred_element_type=F32))
        cnt = jnp.zeros((nb, bs), jnp.int32)
        for jp in range(nb):
            gj = gate[jp:jp + 1, :]
            beats = (gj > gate) | ((gj == gate) & (jp < jidx))
            cnt = cnt + jnp.where(beats & (jp < iq), 1, 0)
        selm = (cnt < MOBA_TOPK) & (jidx < iq)
        qs.append(q_raw * (dh ** -0.5 * LOG2E))
        qaug.append(jnp.concatenate([qs[i], jnp.where(selm, 0.0, NEG_INF).astype(BF16)], axis=0))

    kpos = lax.broadcasted_iota(jnp.int32, (bs, bs), 0)
    qpos = lax.broadcasted_iota(jnp.int32, (bs, bs), 1)
    causal = kpos <= qpos
    own_rows = pl.ds(pl.multiple_of((iq % 2) * bs, bs), bs)
    s = [jnp.where(causal, jnp.dot(kaug_ref[i, iq // 2, own_rows, :][:, 0:dh], qs[i],
                                   preferred_element_type=F32), NEG_INF)
         for i in hh]
    m0 = [jnp.max(s[i], axis=0, keepdims=True) for i in hh]
    p = [jnp.exp2(s[i] - m0[i]) for i in hh]
    acc0 = [jnp.dot(vblk_ref[i, iq], p[i].astype(BF16), preferred_element_type=F32) for i in hh]

    def step(j, carry):
        m, acc = carry
        sj = [jnp.dot(kaug_ref[i, j], qaug[i], preferred_element_type=F32) for i in hh]
        m_new, acc_new = [], []
        for i in hh:
            mi = jnp.maximum(m[i], jnp.max(sj[i], axis=0, keepdims=True))
            pj = jnp.exp2(sj[i] - mi).astype(BF16)
            acc_new.append(jnp.exp2(m[i] - mi) * acc[i]
                           + jnp.dot(vpair_ref[i, j], pj, preferred_element_type=F32))
            m_new.append(mi)
        return m_new, acc_new

    _, acc = lax.fori_loop(0, (iq + 1) // 2, step, (m0, acc0))
    for i in hh:
        o_ref[0, i * dh:(i + 1) * dh, :] = (acc[i][0:dh, :] / acc[i][dh:dh + 1, :]).astype(BF16)


def _moba_out_body(x_ref, o_ref_in, w_ref, o_ref):
    o_ref[0] = x_ref[0] + lax.dot_general(o_ref_in[0], w_ref[...], (((0,), (0,)), ((), ())),
                                          preferred_element_type=F32)


def _moba(x, g, w_qkv, w_o, ts=1024):
    b, s, d = x.shape
    nb = s // MOBA_BLOCK
    dh = MOBA_HEAD_DIM
    assert s % ts == 0 and s % MOBA_BLOCK == 0 and nb % 2 == 0
    qkvt = pl.pallas_call(
        functools.partial(_moba_qkv_body, nc=512),
        out_shape=jax.ShapeDtypeStruct((b, 3 * d, s), BF16),
        grid=(b, s // ts),
        in_specs=[pl.BlockSpec((1, ts, d), lambda i, j: (i, j, 0)),
                  _const_spec((1, d)),
                  _const_spec((3 * d, d))],
        out_specs=pl.BlockSpec((1, 3 * d, ts), lambda i, j: (i, 0, j)),
        compiler_params=_params(2),
        name="moba_qkv",
    )(x, _row(g), w_qkv.T.astype(BF16))

    hg = MOBA_HEAD_GROUP
    ng = MOBA_HEADS // hg
    gd = hg * dh
    ot = pl.pallas_call(
        functools.partial(_moba_attn_body, nb=nb),
        out_shape=jax.ShapeDtypeStruct((b, d, s), BF16),
        grid=(b, ng, nb),
        in_specs=[pl.BlockSpec((1, gd, MOBA_BLOCK), lambda i, j, k: (i, j, k)),
                  pl.BlockSpec((1, gd, s), lambda i, j, k: (i, ng + j, 0)),
                  pl.BlockSpec((1, gd, s), lambda i, j, k: (i, 2 * ng + j, 0))],
        out_specs=pl.BlockSpec((1, gd, MOBA_BLOCK), lambda i, j, k: (i, j, k)),
        scratch_shapes=[pltpu.VMEM((hg, nb // 2, 2 * MOBA_BLOCK, dh + nb), BF16),
                        pltpu.VMEM((hg, nb, dh + MOBA_ONES_ROWS, MOBA_BLOCK), BF16),
                        pltpu.VMEM((hg, nb // 2, dh + MOBA_ONES_ROWS, 2 * MOBA_BLOCK), BF16),
                        pltpu.VMEM((hg, nb, dh), F32)],
        compiler_params=_params(3),
        name="moba_attn",
    )(qkvt, qkvt, qkvt)

    return pl.pallas_call(
        _moba_out_body,
        out_shape=jax.ShapeDtypeStruct((b, s, d), F32),
        grid=(b, s // ts),
        in_specs=[pl.BlockSpec((1, ts, d), lambda i, j: (i, j, 0)),
                  pl.BlockSpec((1, d, ts), lambda i, j: (i, 0, j)),
                  _const_spec((d, d))],
        out_specs=pl.BlockSpec((1, ts, d), lambda i, j: (i, j, 0)),
        compiler_params=_params(2),
        name="moba_out",
    )(x, ot, w_o.astype(BF16))


def _rwkv_body(x_ref, g_ref, mu_ref, wr_ref, wk_ref, wv_ref, w0_ref, w1_ref, w2_ref, a0_ref, a1_ref,
               a2_ref, g1_ref, g2_ref, kk_ref, ka_ref, rk_ref, gnw_ref, gnb_ref, wo_ref, p_ref, pt_ref,
               o_ref,
               hbuf_ref, st_ref, r_s, lw_s, k_s, v_s, a_s, b_s, y_s, gate_s, bonus_s,
               ch0, t0, mv0, mr0, gt0, ch1, t1, mv1, mr1, gt1, *, tiles_per_seq):
    bufs = ((ch0, t0, mv0, mr0, gt0), (ch1, t1, mv1, mr1, gt1))
    tm = x_ref.shape[0]
    lc = RWKV_CHUNK
    hd = RWKV_HEAD_DIM
    hs = SUBLANES

    @pl.when(pl.program_id(0) % tiles_per_seq == 0)
    def _():
        hbuf_ref[tm:tm + hs, :] = jnp.zeros((hs, D_MODEL), F32)
        st_ref[...] = jnp.zeros_like(st_ref)

    def seg_sum(v):
        return jnp.dot(v.astype(BF16), p_ref[...], preferred_element_type=F32)

    def seg_bcast(v):
        return _dot_hilo(v, pt_ref[...])

    x = x_ref[...]
    hn = _rms(x, g_ref[...])
    hbuf_ref[0:hs, :] = hbuf_ref[tm:tm + hs, :]
    hbuf_ref[hs:hs + tm, :] = hn
    xx = hbuf_ref[hs - 1:hs - 1 + tm, :] - hn

    def mix(i):
        return hn + xx * mu_ref[i:i + 1, :]

    r = _dot(mix(0), wr_ref[...])
    k = _dot(mix(1), wk_ref[...])
    v = _dot(mix(2), wv_ref[...])
    wpre = w0_ref[...] + _dot(jnp.tanh(_dot(mix(3), w1_ref[...])), w2_ref[...])
    lw = -jnp.exp(-jax.nn.softplus(-wpre) - 0.5)
    a = jax.nn.sigmoid(a0_ref[...] + _dot(_dot(mix(4), a1_ref[...]), a2_ref[...]))
    gate_s[...] = _dot(jax.nn.sigmoid(_dot(mix(5), g1_ref[...])), g2_ref[...])

    kk = k * kk_ref[...]
    inv = 1.0 / jnp.maximum(jnp.sqrt(seg_sum(kk * kk)), 1e-12)
    kk = kk * seg_bcast(inv)
    k = k * (1.0 + (a - 1.0) * ka_ref[...])
    bonus_s[...] = seg_bcast(seg_sum(r * k * rk_ref[...])) * v
    r_s[...] = r
    lw_s[...] = lw
    k_s[...] = k
    v_s[...] = v
    a_s[...] = -kk
    b_s[...] = kk * a

    ti = lax.broadcasted_iota(jnp.int32, (lc, lc), 0)
    si = lax.broadcasted_iota(jnp.int32, (lc, lc), 1)
    tril = jnp.where(si <= ti, 1.0, 0.0).astype(BF16)
    tril3 = jnp.concatenate([tril, tril, tril], axis=1)
    low_strict = si < ti
    ti2 = lax.broadcasted_iota(jnp.int32, (lc, 2 * lc), 0)
    si2 = lax.broadcasted_iota(jnp.int32, (lc, 2 * lc), 1)
    low_incl2 = jnp.where(si2 >= lc, si2 - lc, si2) <= ti2
    keep_u = si2 >= lc
    low_strict2 = si2 < ti2

    hh = range(RWKV_HEADS)
    sls = [slice(h * hd, (h + 1) * hd) for h in hh]
    eye2 = jnp.where(si2 - lc == ti2, 1.0, 0.0)

    def prepare(c):
        ch, t_s, mv_s, mr_s, gt_s = bufs[c % 2]
        rows = slice(c * lc, (c + 1) * lc)
        lw_c = lw_s[rows, :]
        lg = _cumsum_rows(tril3, lw_c)
        lg_last = lg[lc - 1:lc, :]
        g_inv = jnp.exp(-lg)
        g_tail = jnp.exp(lg_last - lg)
        gt_s[...] = jnp.broadcast_to(jnp.exp(lg_last), gt_s.shape)
        ch[0, 0:lc, :] = a_s[rows, :] * jnp.exp(lg - lw_c)
        ch[0, lc:2 * lc, :] = r_s[rows, :] * jnp.exp(lg)
        ch[1, 0:lc, :] = b_s[rows, :] * g_inv
        ch[1, lc:2 * lc, :] = k_s[rows, :] * g_inv
        ch[2, 0:lc, :] = b_s[rows, :] * g_tail
        ch[2, lc:2 * lc, :] = k_s[rows, :] * g_tail
        gm = [_dot_nt(ch[0, :, sls[h]], ch[1, :, sls[h]]) for h in hh]
        for h in hh:
            mr_s[h] = jnp.where(low_incl2, gm[h][lc:2 * lc, :], 0.0).astype(BF16)
            mv_s[h] = _dot(jnp.where(low_strict, gm[h][0:lc, lc:2 * lc], 0.0), v_s[rows, sls[h]])
        w = [jnp.where(keep_u, eye2, jnp.where(low_strict2, gm[h][0:lc, :], 0.0)) for h in hh]
        steps = 1
        while steps < lc:
            r = [_dot(w[h][:, 0:lc], w[h]) for h in hh]
            w = [r[h] + jnp.where(keep_u, w[h], 0.0) for h in hh]
            steps *= 2
        for h in hh:
            t_s[h] = w[h][:, lc:].astype(BF16)

    def advance(c):
        ch, t_s, mv_s, mr_s, gt_s = bufs[c % 2]
        rows = slice(c * lc, (c + 1) * lc)
        sh = [st_ref[h] for h in hh]
        ars = [_dot_nt(ch[0, :, sls[h]], sh[h]) for h in hh]
        u = [jnp.dot(t_s[h], (ars[h][0:lc, :] + mv_s[h]).astype(BF16), preferred_element_type=F32)
             for h in hh]
        uv = [jnp.concatenate([u[h].astype(BF16), v_s[rows, sls[h]].astype(BF16)], axis=0) for h in hh]
        for h in hh:
            y_s[rows, sls[h]] = ars[h][lc:2 * lc, :] + jnp.dot(mr_s[h], uv[h], preferred_element_type=F32)
        for h in hh:
            st_ref[h] = sh[h] * gt_s[0:1, sls[h]] + _dot_tn(uv[h], ch[2, :, sls[h]])

    n_chunks = tm // lc
    prepare(0)
    for c in range(n_chunks):
        if c + 1 < n_chunks:
            prepare(c + 1)
        advance(c)

    y = y_s[...]
    mean = seg_bcast(seg_sum(y)) * (1.0 / hd)
    dy = y - mean
    var = seg_bcast(seg_sum(dy * dy)) * (1.0 / hd)
    yn = dy * lax.rsqrt(var + RWKV_GN_EPS) * gnw_ref[...] + gnb_ref[...]
    o_ref[...] = x + _dot((yn + bonus_s[...]) * gate_s[...], wo_ref[...])


def _rwkv7(x2, seq, g, mu, w_rkv, w0, w1, w2, a0, a1, a2, g1, g2, k_k, k_a, r_k, gn_w, gn_b, w_o, tm=256):
    rows = x2.shape[0]
    d = D_MODEL
    assert seq % tm == 0 and tm % RWKV_CHUNK == 0 and RWKV_CHUNK == RWKV_HEAD_DIM
    head_of = jnp.arange(d) // RWKV_HEAD_DIM
    p = (head_of[:, None] == jnp.arange(128)[None, :]).astype(BF16)
    args = (x2, _row(g), mu.astype(F32), w_rkv[0].astype(BF16), w_rkv[1].astype(BF16), w_rkv[2].astype(BF16),
            _row(w0), w1.astype(BF16), w2.astype(BF16), _row(a0), a1.astype(BF16), a2.astype(BF16),
            g1.astype(BF16), g2.astype(BF16), _row(k_k), _row(k_a), _row(r_k), _row(gn_w), _row(gn_b),
            w_o.astype(BF16), p, jnp.concatenate([p.T, p.T], axis=0))
    in_specs = [pl.BlockSpec((tm, d), lambda i: (i, 0))] + [_const_spec(a.shape) for a in args[1:]]
    big = pltpu.VMEM((tm, d), F32)
    lc = RWKV_CHUNK
    return pl.pallas_call(
        functools.partial(_rwkv_body, tiles_per_seq=seq // tm),
        out_shape=jax.ShapeDtypeStruct((rows, d), F32),
        grid=(rows // tm,),
        in_specs=in_specs,
        out_specs=pl.BlockSpec((tm, d), lambda i: (i, 0)),
        scratch_shapes=[pltpu.VMEM((tm + 2 * SUBLANES, d), F32),
                        pltpu.VMEM((RWKV_HEADS, RWKV_HEAD_DIM, RWKV_HEAD_DIM), F32),
                        big, big, big, big, big, big, big, big, big,
                        ] + 2 * [pltpu.VMEM((3, 2 * lc, d), F32),
                                 pltpu.VMEM((RWKV_HEADS, lc, lc), BF16),
                                 pltpu.VMEM((RWKV_HEADS, lc, RWKV_HEAD_DIM), F32),
                                 pltpu.VMEM((RWKV_HEADS, lc, 2 * lc), BF16),
                                 pltpu.VMEM((SUBLANES, d), F32)],
        compiler_params=_params(1),
        name="rwkv7",
    )(*args)


def _ssd_in_body(x_ref, g_ref, w_ref, wdt_ref, zx_ref, dt_ref, *, nc):
    hn = _rms(x_ref[...], g_ref[...])
    h = hn.astype(BF16)
    for n in range(w_ref.shape[1] // nc):
        zx_ref[:, n * nc:(n + 1) * nc] = jnp.dot(h, w_ref[:, n * nc:(n + 1) * nc],
                                                 preferred_element_type=F32)
    h_lo = (hn - h.astype(F32)).astype(BF16)
    w_hi, w_lo = _split2(wdt_ref[...])
    dt_ref[...] = (jnp.dot(h, w_hi, preferred_element_type=F32)
                   + jnp.dot(h, w_lo, preferred_element_type=F32)
                   + jnp.dot(h_lo, w_hi, preferred_element_type=F32))


def _ssd_core_body(zx_ref, dt_ref, cw_ref, cb_ref, dtb_ref, alog_ref, dskip_ref, nw_ref, e_ref, o_ref,
                   cbuf_ref, state_ref, y_s):
    lc = SSD_CHUNK
    di = SSD_D_INNER
    hs = SUBLANES
    gw = di // SSD_GROUPS
    ns = SSD_STATE

    @pl.when(pl.program_id(1) == 0)
    def _():
        cbuf_ref[lc:lc + hs, :] = jnp.zeros((hs, SSD_CONV_DIM), F32)
        state_ref[...] = jnp.zeros_like(state_ref)

    z = zx_ref[:, 0:di]
    xbc_raw = zx_ref[:, di:di + SSD_CONV_DIM]
    cbuf_ref[0:hs, :] = cbuf_ref[lc:lc + hs, :]
    cbuf_ref[hs:hs + lc, :] = xbc_raw
    xbc = cb_ref[...] + cw_ref[SSD_CONV - 1:SSD_CONV, :] * xbc_raw
    win = cbuf_ref[0:hs + lc, :]
    for j in range(SSD_CONV - 1):
        back = SSD_CONV - 1 - j
        xbc = xbc + cw_ref[j:j + 1, :] * pltpu.roll(win, back, 0)[hs:hs + lc, :]
    xbc = jax.nn.silu(xbc)
    xs = xbc[:, 0:di]
    bm = xbc[:, di:di + SSD_GROUPS * ns]
    cm = xbc[:, di + SSD_GROUPS * ns:]

    dt = jax.nn.softplus(dt_ref[...] + dtb_ref[...])
    adt = dt * (-jnp.exp(alog_ref[...]))
    li = lax.broadcasted_iota(jnp.int32, (lc, lc), 0)
    si = lax.broadcasted_iota(jnp.int32, (lc, lc), 1)
    causal = si <= li
    first_head = lax.broadcasted_iota(jnp.int32, (lc, 2 * SSD_HEAD_DIM), 1) < SSD_HEAD_DIM
    tril = jnp.where(causal, 1.0, 0.0).astype(BF16)
    acs = _cumsum_rows(jnp.concatenate([tril, tril, tril], axis=1), adt)
    acs_t = acs.T
    a_last = acs[lc - 1:lc, :]

    def expand(v):
        return _dot_hilo(v, e_ref[...])

    exp_acs_x = expand(jnp.exp(acs))
    xd = xs * expand(dt)
    xdec = xd * expand(jnp.exp(a_last - acs))

    for g in range(SSD_GROUPS):
        bg = bm[:, g * ns:(g + 1) * ns]
        cg = cm[:, g * ns:(g + 1) * ns]
        cb = _dot_nt(cg, bg)
        gsl = slice(g * gw, (g + 1) * gw)
        st_g = state_ref[:, gsl]
        y_off = _dot(cg, st_g) * exp_acs_x[:, gsl]
        def masked_decay(h):
            col = jnp.broadcast_to(acs[:, h:h + 1], (lc, lc))
            row = jnp.broadcast_to(acs_t[h:h + 1, :], (lc, lc))
            return (cb * jnp.where(causal, jnp.exp(col - row), 0.0)).astype(BF16)

        for e in range(0, SSD_HEADS // SSD_GROUPS, 2):
            h = g * (SSD_HEADS // SSD_GROUPS) + e
            psl = slice(h * SSD_HEAD_DIM, (h + 2) * SSD_HEAD_DIM)
            xdp = xd[:, psl]
            xd_bd = jnp.concatenate([jnp.where(first_head, xdp, 0.0), jnp.where(first_head, 0.0, xdp)], axis=0)
            o0 = e * SSD_HEAD_DIM
            y_s[:, psl] = (jnp.dot(jnp.concatenate([masked_decay(h), masked_decay(h + 1)], axis=1),
                                   xd_bd.astype(BF16), preferred_element_type=F32)
                           + y_off[:, o0:o0 + 2 * SSD_HEAD_DIM])
        state_ref[:, gsl] = st_g * exp_acs_x[lc - 1:lc, gsl] + _dot_tn(bg, xdec[:, gsl])

    y = y_s[...] + xs * dskip_ref[...]
    yz = y * jax.nn.silu(z)
    for g in range(SSD_GROUPS):
        gsl = slice(g * gw, (g + 1) * gw)
        yg = yz[:, gsl]
        o_ref[:, gsl] = (yg * lax.rsqrt(jnp.mean(yg * yg, axis=-1, keepdims=True) + SSD_NORM_EPS)
                         * nw_ref[:, gsl])


def _mamba2(x2, batch, seq, g, w_in, conv_w, conv_b, dt_bias, a_log, d_skip, norm_w, w_out, tm=512):
    rows = x2.shape[0]
    d = D_MODEL
    di = SSD_D_INNER
    assert rows % tm == 0 and seq % SSD_CHUNK == 0 and SSD_CHUNK == 2 * SSD_HEAD_DIM
    nzx = di + SSD_CONV_DIM
    zx, dt = pl.pallas_call(
        functools.partial(_ssd_in_body, nc=512),
        out_shape=(jax.ShapeDtypeStruct((rows, nzx), F32), jax.ShapeDtypeStruct((rows, SSD_HEADS), F32)),
        grid=(rows // tm,),
        in_specs=[pl.BlockSpec((tm, d), lambda i: (i, 0)),
                  _const_spec((1, d)),
                  _const_spec((d, nzx)),
                  _const_spec((d, SSD_HEADS))],
        out_specs=(pl.BlockSpec((tm, nzx), lambda i: (i, 0)),
                   pl.BlockSpec((tm, SSD_HEADS), lambda i: (i, 0))),
        compiler_params=_params(1),
        name="ssd_in",
    )(x2, _row(g), w_in[:, :nzx].astype(BF16), w_in[:, nzx:].astype(F32))

    lc = SSD_CHUNK
    nchunk = seq // lc
    expand = (jnp.arange(SSD_HEADS)[:, None] == (jnp.arange(di) // SSD_HEAD_DIM)[None, :]).astype(BF16)
    d_x = jnp.repeat(d_skip.astype(F32), SSD_HEAD_DIM).reshape(1, di)
    args = (zx, dt, conv_w.astype(F32), _row(conv_b), _row(dt_bias), _row(a_log), d_x, _row(norm_w),
            jnp.concatenate([expand, expand], axis=0))
    yz = pl.pallas_call(
        _ssd_core_body,
        out_shape=jax.ShapeDtypeStruct((rows, di), F32),
        grid=(batch, nchunk),
        in_specs=[pl.BlockSpec((lc, nzx), lambda b, c: (b * nchunk + c, 0)),
                  pl.BlockSpec((lc, SSD_HEADS), lambda b, c: (b * nchunk + c, 0))]
                 + [_const_spec(a.shape) for a in args[2:]],
        out_specs=pl.BlockSpec((lc, di), lambda b, c: (b * nchunk + c, 0)),
        scratch_shapes=[pltpu.VMEM((lc + 2 * SUBLANES, SSD_CONV_DIM), F32),
                        pltpu.VMEM((SSD_STATE, di), F32),
                        pltpu.VMEM((lc, di), F32)],
        compiler_params=_params(2),
        name="ssd_core",
    )(*args)
    return x2, (yz, w_out)


def _conf_body(x_ref, g_ref, w1_ref, b1_ref, dw_ref, dwb_ref, lnw_ref, lnb_ref, w2_ref, b2_ref, o_ref,
               ubuf_ref, c_s, *, tiles_per_seq, rb, lanes):
    tm = x_ref.shape[0]
    d = D_MODEL
    hl = CONF_HALO

    @pl.when(pl.program_id(0) % tiles_per_seq == 0)
    def _():
        ubuf_ref[tm:tm + hl, :] = jnp.zeros((hl, d), F32)

    x = x_ref[...]
    h = _rms(x, g_ref[...]).astype(BF16)
    p = jnp.dot(h, w1_ref[...], preferred_element_type=F32) + b1_ref[...]
    u = p[:, 0:d] * jax.nn.sigmoid(p[:, d:2 * d])
    ubuf_ref[0:hl, :] = ubuf_ref[tm:tm + hl, :]
    ubuf_ref[hl:hl + tm, :] = u
    base = hl - (CONF_KERNEL - 1)
    for r0 in range(0, tm, rb):
        for c0 in range(0, d, lanes):
            acc = jnp.broadcast_to(dwb_ref[:, c0:c0 + lanes], (rb, lanes))
            win = ubuf_ref[r0:r0 + rb + hl, c0:c0 + lanes]
            for res in range(SUBLANES):
                offs = [o for o in range(base, base + CONF_KERNEL) if o % SUBLANES == res]
                slab = win if res == 0 else pltpu.roll(win, rb + hl - res, 0)
                for o in offs:
                    a0 = o - res
                    acc = acc + dw_ref[o - base:o - base + 1, c0:c0 + lanes] * slab[a0:a0 + rb, :]
            c_s[r0:r0 + rb, c0:c0 + lanes] = acc
    y = c_s[...]
    mu = jnp.mean(y, axis=-1, keepdims=True)
    dy = y - mu
    var = jnp.mean(dy * dy, axis=-1, keepdims=True)
    zn = dy * lax.rsqrt(var + CONF_LN_EPS) * lnw_ref[...] + lnb_ref[...]
    o_ref[...] = x + _dot(jax.nn.silu(zn), w2_ref[...]) + b2_ref[...]


def _conformer(x2, seq, g, w_pw1, b_pw1, dw_w, dw_b, ln_w, ln_b, w_pw2, b_pw2, tm=512):
    rows = x2.shape[0]
    d = D_MODEL
    assert seq % tm == 0 and CONF_HALO >= CONF_KERNEL - 1 and CONF_HALO % SUBLANES == 0
    args = (x2, _row(g), w_pw1.astype(BF16), _row(b_pw1), dw_w.astype(F32), _row(dw_b), _row(ln_w),
            _row(ln_b), w_pw2.astype(BF16), _row(b_pw2))
    return pl.pallas_call(
        functools.partial(_conf_body, tiles_per_seq=seq // tm, rb=128, lanes=128),
        out_shape=jax.ShapeDtypeStruct((rows, d), F32),
        grid=(rows // tm,),
        in_specs=[pl.BlockSpec((tm, d), lambda i: (i, 0))] + [_const_spec(a.shape) for a in args[1:]],
        out_specs=pl.BlockSpec((tm, d), lambda i: (i, 0)),
        scratch_shapes=[pltpu.VMEM((tm + CONF_HALO, d), F32), pltpu.VMEM((tm, d), F32)],
        compiler_params=_params(1),
        name="conformer",
    )(*args)


def kernel(x, norm_mix, norm_ffn, norm_final, ffn_w_up, ffn_conv_w, ffn_conv_b, ffn_w_down, moba_w_qkv, moba_w_o, rwkv_mu, rwkv_w_rkv, rwkv_w0, rwkv_w1, rwkv_w2, rwkv_a0, rwkv_a1, rwkv_a2, rwkv_g1, rwkv_g2, rwkv_k_k, rwkv_k_a, rwkv_r_k, rwkv_gn_w, rwkv_gn_b, rwkv_w_o, ssd_w_in, ssd_conv_w, ssd_conv_b, ssd_dt_bias, ssd_a_log, ssd_d, ssd_norm_w, ssd_w_out, conf_w_pw1, conf_b_pw1, conf_dw_w, conf_dw_b, conf_ln_w, conf_ln_b, conf_w_pw2, conf_b_pw2):
    b, s, d = x.shape
    depth = norm_mix.shape[0]
    rows = b * s
    w_up_all = ffn_w_up.astype(BF16)
    w_down_all = ffn_w_down.astype(BF16)
    for i in range(depth):
        kind, j = i % 4, i // 4
        pending = None
        if kind == 0:
            x2 = _moba(x.reshape(b, s, d), norm_mix[i], moba_w_qkv[j], moba_w_o[j]).reshape(rows, d)
        elif kind == 1:
            x2 = _rwkv7(x.reshape(rows, d), s, norm_mix[i], rwkv_mu[j], rwkv_w_rkv[j], rwkv_w0[j], rwkv_w1[j],
                        rwkv_w2[j], rwkv_a0[j], rwkv_a1[j], rwkv_a2[j], rwkv_g1[j], rwkv_g2[j], rwkv_k_k[j],
                        rwkv_k_a[j], rwkv_r_k[j], rwkv_gn_w[j], rwkv_gn_b[j], rwkv_w_o[j])
        elif kind == 2:
            x2, pending = _mamba2(x.reshape(rows, d), b, s, norm_mix[i], ssd_w_in[j], ssd_conv_w[j],
                                  ssd_conv_b[j], ssd_dt_bias[j], ssd_a_log[j], ssd_d[j], ssd_norm_w[j],
                                  ssd_w_out[j])
        else:
            x2 = _conformer(x.reshape(rows, d), s, norm_mix[i], conf_w_pw1[j], conf_b_pw1[j], conf_dw_w[j],
                            conf_dw_b[j], conf_ln_w[j], conf_ln_b[j], conf_w_pw2[j], conf_b_pw2[j])
        x = _conv_ffn(x2, s, norm_ffn[i], w_up_all, ffn_conv_w[i], ffn_conv_b[i], w_down_all, i,
                      norm_final, final=(i == depth - 1), pending=pending)
    return x.reshape(b, s, d)
```
